```python
import jax
import jax.numpy as jnp
from jax import lax
import numpy as np

D_MODEL = 1024
BATCH = 8
SEQ = 2048
DEPTH = 1
DEC_BATCH = 128
DEC_SEQ = 4
PAST_LEN = 8192
PAGE_SIZE = 128

A_GROUPS = 8
A_GROUP_DIM = 64
A_WIDTH = A_GROUPS * A_GROUP_DIM
CHUNK = 128
MLA_HEADS = 8
Q_LORA = 256
KV_LORA = 256
QK_NOPE = 64
QK_ROPE = 32
QK_HEAD = QK_NOPE + QK_ROPE
V_HEAD = 64
B_WIDTH = MLA_HEADS * V_HEAD
MIX_WIDTH = A_WIDTH + B_WIDTH
ROPE_THETA = 10000.0
Q_BLOCK = 128
OFF_CQ = 2 * A_WIDTH
OFF_CKV = OFF_CQ + Q_LORA
OFF_KPE = OFF_CKV + KV_LORA
N_IN = OFF_KPE + QK_ROPE
MEM_LEN = 256
MEM_HEADS = 4
MEM_HEAD_DIM = 128
MEM_WIDTH = MEM_HEADS * MEM_HEAD_DIM
D_FF = 2816
CONV_W = 3
EPS = 1e-6
NEG_INF = -1e30

kernel_name = 'hymba_gmlp_mla_memxattn_convffn_step'


def rms_norm(x, g):
    xf = x.astype(jnp.float32)
    y = xf * lax.rsqrt(jnp.mean(xf * xf, axis=-1, keepdims=True) + EPS)
    return (y * g.astype(jnp.float32)).astype(x.dtype)


def layer_norm(x, g, b):
    xf = x.astype(jnp.float32)
    xc = xf - jnp.mean(xf, axis=-1, keepdims=True)
    y = xc * lax.rsqrt(jnp.mean(xc * xc, axis=-1, keepdims=True) + EPS)
    return (y * g.astype(jnp.float32) + b.astype(jnp.float32)).astype(x.dtype)


def split_in(z):
    return (z[..., :A_WIDTH], z[..., A_WIDTH:OFF_CQ], z[..., OFF_CQ:OFF_CKV],
            z[..., OFF_CKV:OFF_KPE], z[..., OFF_KPE:N_IN])


def qk_rope(x, pos):
    half = QK_ROPE // 2
    inv_freq = ROPE_THETA ** (-jnp.arange(half, dtype=jnp.float32) / half)
    ang = pos.astype(jnp.float32)[:, None] * inv_freq[None, :]
    cos = jnp.cos(ang)[:, None, :]
    sin = jnp.sin(ang)[:, None, :]
    x1 = x[..., QK_NOPE:QK_NOPE + half].astype(jnp.float32)
    x2 = x[..., QK_NOPE + half:].astype(jnp.float32)
    rot = jnp.concatenate([x1 * cos - x2 * sin, x2 * cos + x1 * sin], axis=-1).astype(x.dtype)
    return jnp.concatenate([x[..., :QK_NOPE], rot], axis=-1)


def chunk_gmlp(u, v, w_s, b_s, ln_g, ln_b):
    n, t, _ = u.shape
    c = min(t, CHUNK)
    vg = layer_norm(jax.nn.gelu(v).reshape(n, t // c, c, A_GROUPS, A_GROUP_DIM), ln_g, ln_b)
    w = jnp.where(jnp.tril(jnp.ones((c, c), dtype=bool)), w_s[:, :c, :c], 0.0).astype(vg.dtype)
    mixed = jnp.einsum('gts,nmsgd->nmtgd', w, vg) + b_s[:, :c].T[:, :, None].astype(vg.dtype)
    out = jax.nn.gelu(u) * mixed.reshape(n, t, A_WIDTH)
    return out, vg.reshape(n, t, A_GROUPS, A_GROUP_DIM)


def mla_queries(cq, pos, g_q_a, w_uq, g_qk_q):
    n, t, _ = cq.shape
    q = (rms_norm(cq, g_q_a) @ w_uq).reshape(n, t, MLA_HEADS, QK_HEAD)
    return qk_rope(rms_norm(q, g_qk_q), pos)


def mla_keys_values(ckv_n, kpe, pos, w_uk, w_uv, g_qk_k):
    lead = ckv_n.shape[:-1]
    k_nope = (ckv_n @ w_uk).reshape(*lead, MLA_HEADS, QK_NOPE)
    v = (ckv_n @ w_uv).reshape(*lead, MLA_HEADS, V_HEAD)
    k_pe = jnp.broadcast_to(kpe[..., None, :], (*lead, MLA_HEADS, QK_ROPE))
    k = qk_rope(rms_norm(jnp.concatenate([k_nope, k_pe], axis=-1), g_qk_k), pos)
    return k, v


def masked_attend(q, k, v, q_pos, k_pos):
    s = jnp.einsum('...qhd,...khd->...hqk', q, k).astype(jnp.float32) * (QK_HEAD ** -0.5)
    s = jnp.where(k_pos[None, :] <= q_pos[:, None], s, NEG_INF)
    p = jax.nn.softmax(s, axis=-1).astype(v.dtype)
    return jnp.einsum('...hqk,...khd->...qhd', p, v)


def mla_prompt_attend(q, k, v):
    n, s = q.shape[:2]
    nb = s // Q_BLOCK
    pos = jnp.arange(s)
    qb = q.reshape(n, nb, Q_BLOCK, MLA_HEADS, QK_HEAD).swapaxes(0, 1)

    def block(args):
        q_blk, start = args
        return masked_attend(q_blk, k, v, start + jnp.arange(Q_BLOCK), pos)

    o = lax.map(block, (qb, jnp.arange(nb) * Q_BLOCK))
    return o.swapaxes(0, 1).reshape(n, s, B_WIDTH)


def mla_sample_attend(q, ckv_n, kpe, pool_ckv, pool_kpe, page_table, w_uk, w_uv, g_qk_k):
    n, t = q.shape[:2]
    q_pos = PAST_LEN + jnp.arange(t)
    k_pos = jnp.arange(PAST_LEN + t)

    def one(args):
        pages, q1, c1, pe1 = args
        c_all = jnp.concatenate([pool_ckv[pages].reshape(PAST_LEN, KV_LORA), c1.astype(pool_ckv.dtype)], axis=0)
        pe_all = jnp.concatenate([pool_kpe[pages].reshape(PAST_LEN, QK_ROPE), pe1.astype(pool_kpe.dtype)], axis=0)
        k, v = mla_keys_values(c_all, pe_all, k_pos, w_uk, w_uv, g_qk_k)
        return masked_attend(q1, k, v, q_pos, k_pos)

    o = lax.map(one, (page_table, q, ckv_n, kpe))
    return o.reshape(n, t, B_WIDTH)


def merge_out(a_out, b_out, g_out_a, g_out_b, w_o):
    return jnp.concatenate([rms_norm(a_out, g_out_a), rms_norm(b_out, g_out_b)], axis=-1) @ w_o


def mem_kv(mem, g_mem_in, w_mk, w_mv, g_mk):
    n, m, _ = mem.shape
    hm = rms_norm(mem, g_mem_in)
    k = rms_norm((hm @ w_mk).reshape(n, m, MEM_HEADS, MEM_HEAD_DIM), g_mk)
    v = (hm @ w_mv).reshape(n, m, MEM_HEADS, MEM_HEAD_DIM)
    return k, v


def mem_attend(h, mk, mv, w_mq, g_mq, w_mo):
    n, t, _ = h.shape
    q = rms_norm((h @ w_mq).reshape(n, t, MEM_HEADS, MEM_HEAD_DIM), g_mq)
    s = jnp.einsum('nthd,nmhd->nhtm', q, mk.astype(q.dtype)).astype(jnp.float32) * (MEM_HEAD_DIM ** -0.5)
    p = jax.nn.softmax(s, axis=-1).astype(q.dtype)
    o = jnp.einsum('nhtm,nmhd->nthd', p, mv.astype(q.dtype)).reshape(n, t, MEM_WIDTH)
    return o @ w_mo


def conv_ffn(h, buf, w_up, w_conv, b_conv, w_down):
    t = h.shape[1]
    gate, val = jnp.split(h @ w_up, 2, axis=-1)
    gpad = jnp.concatenate([buf.astype(gate.dtype), gate], axis=1)
    conv = b_conv + sum(gpad[:, j:j + t] * w_conv[j] for j in range(CONV_W))
    return (jax.nn.silu(conv) * val) @ w_down, gpad[:, t:]


def setup_inputs(seed: int = 0) -> dict:
    key = jax.random.key(seed)
    ks = iter(jax.random.split(key, 48))

    def nrm(shape, scale=1.0):
        return jax.random.normal(next(ks), shape, jnp.float32) * scale

    def gain(shape):
        return 1.0 + 0.02 * jax.random.normal(next(ks), shape, jnp.float32)

    n_pages = PAST_LEN // PAGE_SIZE
    n_phys = (DEC_BATCH * n_pages * 5) // 4
    L = DEPTH
    inputs = {
        'x_prompt': nrm((BATCH, SEQ, D_MODEL)),
        'x_sample': nrm((DEC_BATCH, DEC_SEQ, D_MODEL)),
        'cache_ckv': nrm((L, n_phys, PAGE_SIZE, KV_LORA)),
        'cache_kpe': nrm((L, n_phys, PAGE_SIZE, QK_ROPE)),
        'cache_mem_k': nrm((L, DEC_BATCH, MEM_LEN, MEM_HEADS, MEM_HEAD_DIM)),
        'cache_mem_v': nrm((L, DEC_BATCH, MEM_LEN, MEM_HEADS, MEM_HEAD_DIM)),
        'state_ffn_conv': nrm((L, DEC_BATCH, CONV_W - 1, D_FF)),
        'page_table': jax.random.permutation(next(ks), n_phys)[:DEC_BATCH * n_pages]
                      .reshape(DEC_BATCH, n_pages).astype(jnp.int32),
        'mem_prompt': nrm((BATCH, MEM_LEN, D_MODEL)),
        'g_mix': gain((L, D_MODEL)),
        'w_in': nrm((L, D_MODEL, N_IN), D_MODEL ** -0.5),
        'ln_v_g': gain((L, A_GROUPS, A_GROUP_DIM)),
        'ln_v_b': nrm((L, A_GROUPS, A_GROUP_DIM), 0.02),
        'w_s': nrm((L, A_GROUPS, CHUNK, CHUNK), CHUNK ** -0.5),
        'b_s': gain((L, A_GROUPS, CHUNK)),
        'g_q_a': gain((L, Q_LORA)),
        'w_uq': nrm((L, Q_LORA, MLA_HEADS * QK_HEAD), Q_LORA ** -0.5),
        'g_kv_a': gain((L, KV_LORA)),
        'w_uk': nrm((L, KV_LORA, MLA_HEADS * QK_NOPE), KV_LORA ** -0.5),
        'w_uv': nrm((L, KV_LORA, MLA_HEADS * V_HEAD), KV_LORA ** -0.5),
        'g_qk_q': gain((L, QK_HEAD)),
        'g_qk_k': gain((L, QK_HEAD)),
        'g_out_a': gain((L, A_WIDTH)),
        'g_out_b': gain((L, B_WIDTH)),
        'w_o': nrm((L, MIX_WIDTH, D_MODEL), MIX_WIDTH ** -0.5),
        'g_mem_x': gain((L, D_MODEL)),
        'g_mem_in': gain((L, D_MODEL)),
        'w_mq': nrm((L, D_MODEL, MEM_WIDTH), D_MODEL ** -0.5),
        'w_mk': nrm((L, D_MODEL, MEM_WIDTH), D_MODEL ** -0.5),
        'w_mv': nrm((L, D_MODEL, MEM_WIDTH), D_MODEL ** -0.5),
        'g_mq': gain((L, MEM_HEAD_DIM)),
        'g_mk': gain((L, MEM_HEAD_DIM)),
        'w_mo': nrm((L, MEM_WIDTH, D_MODEL), MEM_WIDTH ** -0.5),
        'g_ffn': gain((L, D_MODEL)),
        'w_up': nrm((L, D_MODEL, 2 * D_FF), D_MODEL ** -0.5),
        'w_conv': nrm((L, CONV_W, D_FF), CONV_W ** -0.5),
        'b_conv': nrm((L, D_FF), 0.02),
        'w_down': nrm((L, D_FF, D_MODEL), D_FF ** -0.5),
    }
    return inputs


def reference(x_prompt, x_sample, cache_ckv, cache_kpe, cache_mem_k, cache_mem_v, state_ffn_conv,
              page_table, mem_prompt,
              g_mix, w_in, ln_v_g, ln_v_b, w_s, b_s, g_q_a, w_uq, g_kv_a, w_uk, w_uv, g_qk_q, g_qk_k,
              g_out_a, g_out_b, w_o, g_mem_x, g_mem_in, w_mq, w_mk, w_mv, g_mq, g_mk, w_mo,
              g_ffn, w_up, w_conv, b_conv, w_down):
    n_p, s = x_prompt.shape[:2]
    t = x_sample.shape[1]
    pos_p = jnp.arange(s)
    pos_s = PAST_LEN + jnp.arange(t)
    xp, xs = x_prompt, x_sample
    p_ckv, p_kpe, p_mk, p_mv, p_conv = [], [], [], [], []
    s_ckv, s_kpe, s_chunk_v, s_conv = [], [], [], []
    for l in range(DEPTH):
        u, v, cq, ckv, kpe = split_in(rms_norm(xp, g_mix[l]) @ w_in[l])
        a_out, _ = chunk_gmlp(u, v, w_s[l], b_s[l], ln_v_g[l], ln_v_b[l])
        ckv_n = rms_norm(ckv, g_kv_a[l])
        q = mla_queries(cq, pos_p, g_q_a[l], w_uq[l], g_qk_q[l])
        k, vv = mla_keys_values(ckv_n, kpe, pos_p, w_uk[l], w_uv[l], g_qk_k[l])
        b_out = mla_prompt_attend(q, k, vv)
        xp = xp + merge_out(a_out, b_out, g_out_a[l], g_out_b[l], w_o[l])
        p_ckv.append(ckv_n)
        p_kpe.append(kpe)
        mk, mv = mem_kv(mem_prompt, g_mem_in[l], w_mk[l], w_mv[l], g_mk[l])
        xp = xp + mem_attend(rms_norm(xp, g_mem_x[l]), mk, mv, w_mq[l], g_mq[l], w_mo[l])
        p_mk.append(mk)
        p_mv.append(mv)
        y, buf = conv_ffn(rms_norm(xp, g_ffn[l]), jnp.zeros((n_p, CONV_W - 1, D_FF), xp.dtype),
                          w_up[l], w_conv[l], b_conv[l], w_down[l])
        xp = xp + y
        p_conv.append(buf)

        u, v, cq, ckv, kpe = split_in(rms_norm(xs, g_mix[l]) @ w_in[l])
        a_out, v_rows = chunk_gmlp(u, v, w_s[l], b_s[l], ln_v_g[l], ln_v_b[l])
        ckv_n = rms_norm(ckv, g_kv_a[l])
        q = mla_queries(cq, pos_s, g_q_a[l], w_uq[l], g_qk_q[l])
        b_out = mla_sample_attend(q, ckv_n, kpe, cache_ckv[l], cache_kpe[l], page_table,
                                  w_uk[l], w_uv[l], g_qk_k[l])
        xs = xs + merge_out(a_out, b_out, g_out_a[l], g_out_b[l], w_o[l])
        s_ckv.append(ckv_n)
        s_kpe.append(kpe)
        s_chunk_v.append(v_rows)
        xs = xs + mem_attend(rms_norm(xs, g_mem_x[l]), cache_mem_k[l], cache_mem_v[l], w_mq[l], g_mq[l], w_mo[l])
        y, buf = conv_ffn(rms_norm(xs, g_ffn[l]), state_ffn_conv[l], w_up[l], w_conv[l], b_conv[l], w_down[l])
        xs = xs + y
        s_conv.append(buf)
    return (xp, xs, jnp.stack(p_ckv), jnp.stack(p_kpe), jnp.stack(p_mk), jnp.stack(p_mv), jnp.stack(p_conv),
            jnp.stack(s_ckv), jnp.stack(s_kpe), jnp.stack(s_chunk_v), jnp.stack(s_conv))
```

```python
import functools

import numpy as np
import jax
import jax.numpy as jnp
from jax import lax
from jax.experimental import pallas as pl
from jax.experimental.pallas import tpu as pltpu

D_MODEL = 1024
A_GROUPS = 8
A_GROUP_DIM = 64
A_WIDTH = A_GROUPS * A_GROUP_DIM
CHUNK = 128
MLA_HEADS = 8
Q_LORA = 256
KV_LORA = 256
QK_NOPE = 64
QK_ROPE = 32
QK_HEAD = QK_NOPE + QK_ROPE
V_HEAD = 64
B_WIDTH = MLA_HEADS * V_HEAD
ROPE_THETA = 10000.0
PAGE_SIZE = 128
MEM_HEADS = 4
MEM_HEAD_DIM = 128
MEM_WIDTH = MEM_HEADS * MEM_HEAD_DIM
D_FF = 2816
CONV_W = 3
EPS = 1e-6
NEG_INF = -1e30

LANES = 128
HEAD_PAD = LANES
QK_SCALE = QK_HEAD ** -0.5
MEM_SCALE = MEM_HEAD_DIM ** -0.5

TOK_TILE = 512
FF_TILE = 256
N_FF_TILES = D_FF // FF_TILE
ATT_TILE = 256
KV_TILE = 512
MEM_SEQ_TILE = 8
VMEM_LIMIT = 56 * 1024 * 1024

F32 = jnp.float32
BF16 = jnp.bfloat16


def _dot(a, b):
    return jnp.dot(a, b, preferred_element_type=F32)


def _dot_nt(a, b):
    return lax.dot_general(a, b, (((1,), (1,)), ((), ())), preferred_element_type=F32)


def _rms(x):
    return x * lax.rsqrt(jnp.mean(x * x, axis=-1, keepdims=True) + EPS)


def _gelu(x):
    return jax.nn.gelu(x, approximate=True)


def _const_spec(shape):
    nd = len(shape)
    return pl.BlockSpec(shape, lambda *_: (0,) * nd, pipeline_mode=pl.Buffered(1))


def _params(*sem):
    return pltpu.CompilerParams(dimension_semantics=sem, vmem_limit_bytes=VMEM_LIMIT)


def _inproj_kernel(x_ref, gmix_ref, win_ref, lng_ref, lnb_ref, ws_ref, bs_ref, gqa_ref, wq_ref, wqr_ref,
                   gkva_ref, wk_ref, wv_ref, gq_ref, gqs_ref, gk_ref, gks_ref, cos_ref, sin_ref, goa_ref,
                   pq_ref, *rest, sample):
    if sample:
        qlat_ref, qrf_ref, ckvn_ref, kpe_ref, an_ref, vg_ref, zu_s, vgb_s, a_s = rest
    else:
        q_ref, k_ref, v_ref, ckvn_ref, kpe_ref, an_ref, zu_s, vgb_s, a_s = rest
    tm = x_ref.shape[0]
    xn = (_rms(x_ref[...]) * gmix_ref[...]).astype(BF16)

    zu_s[...] = _dot(xn, win_ref[:, 0:A_WIDTH])
    gv = _gelu(_dot(xn, win_ref[:, A_WIDTH:2 * A_WIDTH]))
    low = lax.broadcasted_iota(jnp.int32, (1, LANES), 1) < A_GROUP_DIM
    inv_g = 1.0 / A_GROUP_DIM
    for p in range(A_WIDTH // LANES):
        sl = slice(p * LANES, (p + 1) * LANES)
        t = gv[:, sl]
        s_lo = jnp.sum(jnp.where(low, t, 0.0), axis=-1, keepdims=True)
        s_hi = jnp.sum(jnp.where(low, 0.0, t), axis=-1, keepdims=True)
        tc = t - jnp.where(low, s_lo, s_hi) * inv_g
        tc2 = tc * tc
        v_lo = jnp.sum(jnp.where(low, tc2, 0.0), axis=-1, keepdims=True)
        v_hi = jnp.sum(jnp.where(low, 0.0, tc2), axis=-1, keepdims=True)
        y = tc * lax.rsqrt(jnp.where(low, v_lo, v_hi) * inv_g + EPS) * lng_ref[:, sl] + lnb_ref[:, sl]
        if sample:
            vg_ref[:, sl] = y
        vgb_s[:, sl] = y.astype(BF16)
    for c in range(tm // CHUNK):
        rows = slice(c * CHUNK, (c + 1) * CHUNK)
        for p in range(A_WIDTH // LANES):
            sl = slice(p * LANES, (p + 1) * LANES)
            vp = vgb_s[rows, sl]
            mixed = jnp.where(low, _dot(ws_ref[2 * p], vp), _dot(ws_ref[2 * p + 1], vp)) + bs_ref[:, sl]
            a_s[rows, sl] = _gelu(zu_s[rows, sl]) * mixed
    an_ref[...] = (_rms(a_s[...]) * goa_ref[...]).astype(BF16)

    o_cq = 2 * A_WIDTH
    o_ckv = o_cq + Q_LORA
    o_kpe = o_ckv + KV_LORA
    cqn = (_rms(_dot(xn, win_ref[:, o_cq:o_ckv])) * gqa_ref[...]).astype(BF16)
    ckvn = _rms(_dot(xn, win_ref[:, o_ckv:o_kpe])) * gkva_ref[...]
    ckvn_ref[...] = ckvn
    kpt = _dot(xn, win_ref[:, o_kpe:o_kpe + LANES])
    kpe_ref[...] = kpt[:, QK_NOPE:QK_HEAD]
    cosv = cos_ref[...]
    sinv = sin_ref[...]
    qa = gq_ref[...] * cosv * QK_SCALE
    qb = gqs_ref[...] * sinv * QK_SCALE
    qall = _dot(cqn, wq_ref[...])
    qrall = _dot(cqn, wqr_ref[...])
    inv_d = 1.0 / QK_HEAD
    if not sample:
        cb = ckvn.astype(BF16)
        kpr = _dot(xn, win_ref[:, o_kpe + LANES:o_kpe + 2 * LANES])
        ka = gk_ref[...] * cosv
        kb = gks_ref[...] * sinv
        kall = _dot(cb, wk_ref[...])
        v_ref[...] = _dot(cb, wv_ref[...]).astype(BF16)
    for h in range(MLA_HEADS):
        sl = slice(h * HEAD_PAD, (h + 1) * HEAD_PAD)
        qh = qall[:, sl]
        r = lax.rsqrt(jnp.sum(qh * qh, axis=-1, keepdims=True) * inv_d + EPS)
        qf = r * (qh * qa + qrall[:, sl] * qb)
        if sample:
            qg = (qf * gk_ref[...]).astype(BF16)
            qlat_ref[:, h * KV_LORA:(h + 1) * KV_LORA] = _dot_nt(qg, wk_ref[:, sl]).astype(BF16)
            qrf_ref[:, sl] = _dot(qf.astype(BF16), pq_ref[...]).astype(BF16)
        else:
            q_ref[:, sl] = qf.astype(BF16)
            kh = kall[:, sl] + kpt
            rk = lax.rsqrt(jnp.sum(kh * kh, axis=-1, keepdims=True) * inv_d + EPS)
            k_ref[:, sl] = (rk * (kh * ka + kpr * kb)).astype(BF16)


def _inproj(x, cos, sin, w, *, sample):
    m = x.shape[0]
    tm = TOK_TILE
    n_tiles = m // tm
    n_pos_tiles = cos.shape[0] // tm
    row = lambda width: pl.BlockSpec((tm, width), lambda i: (i, 0))
    pos = pl.BlockSpec((tm, LANES), lambda i: (i % n_pos_tiles, 0))
    consts = [w['g_mix'], w['w_in'], w['ln_g'], w['ln_b'], w['ws'], w['bs'], w['g_q_a'], w['wq'], w['wq_rot'],
              w['g_kv_a'], w['wk'], w['wv'], w['gq'], w['gq_sw'], w['gk_nope'] if sample else w['gk'], w['gk_sw']]
    in_specs = ([row(D_MODEL)] + [_const_spec(c.shape) for c in consts] + [pos, pos]
                + [_const_spec(w['g_out_a'].shape), _const_spec(w['pq'].shape)])
    args = [x] + consts + [cos, sin, w['g_out_a'], w['pq']]
    wide = MLA_HEADS * HEAD_PAD
    if sample:
        out_shape = [jax.ShapeDtypeStruct((m, MLA_HEADS * KV_LORA), BF16), jax.ShapeDtypeStruct((m, wide), BF16),
                     jax.ShapeDtypeStruct((m, KV_LORA), F32), jax.ShapeDtypeStruct((m, QK_ROPE), F32),
                     jax.ShapeDtypeStruct((m, A_WIDTH), BF16), jax.ShapeDtypeStruct((m, A_WIDTH), F32)]
        out_specs = [row(MLA_HEADS * KV_LORA), row(wide), row(KV_LORA), row(QK_ROPE), row(A_WIDTH), row(A_WIDTH)]
    else:
        out_shape = [jax.ShapeDtypeStruct((m, wide), BF16)] * 3 + [
            jax.ShapeDtypeStruct((m, KV_LORA), F32), jax.ShapeDtypeStruct((m, QK_ROPE), F32),
            jax.ShapeDtypeStruct((m, A_WIDTH), BF16)]
        out_specs = [row(wide)] * 3 + [row(KV_LORA), row(QK_ROPE), row(A_WIDTH)]
    return pl.pallas_call(
        functools.partial(_inproj_kernel, sample=sample),
        grid=(n_tiles,), in_specs=in_specs, out_specs=out_specs, out_shape=out_shape,
        scratch_shapes=[pltpu.VMEM((tm, A_WIDTH), F32), pltpu.VMEM((tm, A_WIDTH), BF16),
                        pltpu.VMEM((tm, A_WIDTH), F32)],
        compiler_params=_params("arbitrary"),
        name="inproj_sample" if sample else "inproj_prompt",
    )(*args)


def _flash_kernel(q_ref, k_ref, v_ref, o_ref):
    i = pl.program_id(2)
    t = ATT_TILE
    row = lax.broadcasted_iota(jnp.int32, (t, t), 0)
    col = lax.broadcasted_iota(jnp.int32, (t, t), 1)
    causal = col <= row
    out = jnp.zeros((t, LANES), F32)
    for hh in range(2):
        sl = slice(hh * HEAD_PAD, (hh + 1) * HEAD_PAD)
        qh = q_ref[:, sl]

        def step(j, carry, masked):
            m, l, acc = carry
            start = pl.multiple_of(j * t, t)
            s = _dot_nt(qh, k_ref[pl.ds(start, t), sl])
            if masked:
                s = jnp.where(causal, s, NEG_INF)
            m_new = jnp.maximum(m, jnp.max(s, axis=-1, keepdims=True))
            alpha = jnp.exp(m - m_new)
            p = jnp.exp(s - m_new)
            l = alpha * l + jnp.sum(p, axis=-1, keepdims=True)
            acc = alpha * acc + _dot(p.astype(BF16), v_ref[pl.ds(start, t), sl])
            return m_new, l, acc

        init = (jnp.full((t, 1), NEG_INF, F32), jnp.zeros((t, 1), F32), jnp.zeros((t, LANES), F32))
        carry = lax.fori_loop(0, i, functools.partial(step, masked=False), init)
        m, l, acc = step(i, carry, True)
        out = out + acc / l
    o_ref[...] = out


def _flash(q, k, v, n_batch, seq):
    m = q.shape[0]
    nq = seq // ATT_TILE
    return pl.pallas_call(
        _flash_kernel,
        grid=(n_batch, MLA_HEADS // 2, nq),
        in_specs=[pl.BlockSpec((ATT_TILE, 2 * HEAD_PAD), lambda b, hp, i: (b * nq + i, hp)),
                  pl.BlockSpec((seq, 2 * HEAD_PAD), lambda b, hp, i: (b, hp)),
                  pl.BlockSpec((seq, 2 * HEAD_PAD), lambda b, hp, i: (b, hp))],
        out_specs=pl.BlockSpec((ATT_TILE, LANES), lambda b, hp, i: (b * nq + i, hp)),
        out_shape=jax.ShapeDtypeStruct((m, B_WIDTH), F32),
        compiler_params=_params("arbitrary", "arbitrary", "arbitrary"),
        name="flash_prompt",
    )(q, k, v)


def _paged_kernel(pt_ref, qlat_ref, qrf_ref, cnew_ref, kpnew_ref, wukt_ref, gvec_ref, tpast_ref, tnew_ref,
                  ckv_hbm, kpe_hbm, ctx_ref, cbuf, pbuf, cn_s, kpn_s, sems, *, n_seq, n_pages):
    n = pl.program_id(0)
    slot = n % 2

    def copies(seq, sl, p):
        page = pt_ref[seq, p]
        rows = pl.ds(p * PAGE_SIZE, PAGE_SIZE)
        return (pltpu.make_async_copy(ckv_hbm.at[page], cbuf.at[sl, rows], sems.at[0, sl]),
                pltpu.make_async_copy(kpe_hbm.at[page], pbuf.at[sl, rows], sems.at[1, sl]))

    def start_fetch(seq, sl):
        def body(p, _):
            a, b = copies(seq, sl, p)
            a.start()
            b.start()
            return 0
        lax.fori_loop(0, n_pages, body, 0)

    def wait_fetch(seq, sl):
        def body(p, _):
            a, b = copies(seq, sl, p)
            a.wait()
            b.wait()
            return 0
        lax.fori_loop(0, n_pages, body, 0)

    @pl.when(n == 0)
    def _():
        cn_s[...] = jnp.zeros_like(cn_s)
        kpn_s[...] = jnp.zeros_like(kpn_s)
        start_fetch(0, 0)

    @pl.when(n + 1 < n_seq)
    def _():
        start_fetch(n + 1, 1 - slot)

    nq = qlat_ref.shape[1]
    t_q = nq // MLA_HEADS
    a_mat = jnp.concatenate([wukt_ref[...], qlat_ref[0]], axis=0)
    ones_rows = jnp.where((lax.broadcasted_iota(jnp.int32, (8, LANES), 1) >= 2 * QK_ROPE)
                          & (lax.broadcasted_iota(jnp.int32, (8, LANES), 1) < 3 * QK_ROPE), 1.0, 0.0)
    l_mat = jnp.concatenate([qrf_ref[0], ones_rows.astype(BF16)], axis=0)
    gvec = gvec_ref[...]
    lane = lax.broadcasted_iota(jnp.int32, (1, LANES), 1)
    trig_lanes = lane < 2 * QK_ROPE
    n_nope = MLA_HEADS * QK_NOPE

    def scores(c, kp, table):
        tk = c.shape[0]
        cb = c.astype(BF16)
        m1 = _dot_nt(a_mat, cb)
        knt = m1[:n_nope]
        ssq = jnp.sum((knt * knt).reshape(MLA_HEADS, QK_NOPE, tk), axis=1)
        kpw = jnp.concatenate([kp, kp, kp, kp], axis=-1)
        feat = kpw * jnp.where(trig_lanes, table * gvec, kpw)
        m2 = _dot_nt(l_mat, feat.astype(BF16))
        r = lax.rsqrt((ssq + m2[nq:nq + MLA_HEADS]) * (1.0 / QK_HEAD) + EPS)
        s = (m1[n_nope:] + m2[:nq]).reshape(t_q, MLA_HEADS, tk) * r[None]
        return s.reshape(nq, tk), cb

    def update(carry, s, cb):
        m, l, acc = carry
        m_new = jnp.maximum(m, jnp.max(s, axis=-1, keepdims=True))
        alpha = jnp.exp(m - m_new)
        p = jnp.exp(s - m_new)
        l = alpha * l + jnp.sum(p, axis=-1, keepdims=True)
        acc = alpha * acc + _dot(p.astype(BF16), cb)
        return m_new, l, acc

    wait_fetch(n, slot)

    def tile(j, carry):
        rows = pl.ds(pl.multiple_of(j * KV_TILE, KV_TILE), KV_TILE)
        s, cb = scores(cbuf[slot, rows, :], pbuf[slot, rows, :], tpast_ref[rows, :])
        return update(carry, s, cb)

    init = (jnp.full((nq, 1), NEG_INF, F32), jnp.zeros((nq, 1), F32), jnp.zeros((nq, KV_LORA), F32))
    carry = lax.fori_loop(0, (n_pages * PAGE_SIZE) // KV_TILE, tile, init)

    cn_s[0:t_q, :] = cnew_ref[0]
    kpn_s[0:t_q, :] = kpnew_ref[0]
    s, cb = scores(cn_s[...], kpn_s[...], tnew_ref[...])
    key = lax.broadcasted_iota(jnp.int32, (nq, LANES), 1)
    qry = lax.broadcasted_iota(jnp.int32, (nq, LANES), 0) // MLA_HEADS
    s = jnp.where(key <= qry, s, NEG_INF)
    m, l, acc = update(carry, s, cb)
    ctx_ref[0] = acc / l


def _paged(page_table, qlat, qrf, cnew, kpnew, cache_ckv, cache_kpe, w):
    n_seq, n_pages = page_table.shape
    nq = qlat.shape[1]
    t_q = cnew.shape[1]
    past = n_pages * PAGE_SIZE
    seq_spec = lambda shape: pl.BlockSpec((1,) + shape, lambda n, pt: (n, 0, 0))
    const = lambda shape: pl.BlockSpec(shape, lambda n, pt: (0,) * len(shape), pipeline_mode=pl.Buffered(1))
    grid_spec = pltpu.PrefetchScalarGridSpec(
        num_scalar_prefetch=1, grid=(n_seq,),
        in_specs=[seq_spec((nq, KV_LORA)), seq_spec((nq, LANES)), seq_spec((t_q, KV_LORA)), seq_spec((t_q, QK_ROPE)),
                  const(w['wukt'].shape), const(w['gvec'].shape), const(w['t_past'].shape), const(w['t_new'].shape),
                  pl.BlockSpec(memory_space=pl.ANY), pl.BlockSpec(memory_space=pl.ANY)],
        out_specs=seq_spec((nq, KV_LORA)),
        scratch_shapes=[pltpu.VMEM((2, past, KV_LORA), F32), pltpu.VMEM((2, past, QK_ROPE), F32),
                        pltpu.VMEM((LANES, KV_LORA), F32), pltpu.VMEM((LANES, QK_ROPE), F32),
                        pltpu.SemaphoreType.DMA((2, 2))])
    return pl.pallas_call(
        functools.partial(_paged_kernel, n_seq=n_seq, n_pages=n_pages),
        grid_spec=grid_spec,
        out_shape=jax.ShapeDtypeStruct((n_seq, nq, KV_LORA), F32),
        compiler_params=_params("arbitrary"),
        name="paged_sample",
    )(page_table, qlat, qrf, cnew, kpnew, w['wukt'], w['gvec'], w['t_past'], w['t_new'], cache_ckv, cache_kpe)


def _memkv_kernel(mem_ref, gin_ref, wmk_ref, wmv_ref, gmk_ref, k_ref, v_ref):
    hm = (_rms(mem_ref[...]) * gin_ref[...]).astype(BF16)
    kk = _dot(hm, wmk_ref[...])
    for h in range(MEM_HEADS):
        sl = slice(h * MEM_HEAD_DIM, (h + 1) * MEM_HEAD_DIM)
        k_ref[:, sl] = _rms(kk[:, sl]) * gmk_ref[...]
    v_ref[...] = _dot(hm, wmv_ref[...])


def _memkv(mem, w):
    m = mem.shape[0]
    tm = TOK_TILE
    row = lambda width: pl.BlockSpec((tm, width), lambda i: (i, 0))
    consts = [w['g_mem_in'], w['w_mk'], w['w_mv'], w['g_mk']]
    return pl.pallas_call(
        _memkv_kernel, grid=(m // tm,),
        in_specs=[row(D_MODEL)] + [_const_spec(c.shape) for c in consts],
        out_specs=[row(MEM_WIDTH), row(MEM_WIDTH)],
        out_shape=[jax.ShapeDtypeStruct((m, MEM_WIDTH), F32)] * 2,
        compiler_params=_params("arbitrary"),
        name="memkv_prompt",
    )(mem, *consts)


def _mem_query(x1, gmx_ref, wmq_ref, gmq_ref):
    qm = _dot((_rms(x1) * gmx_ref[...]).astype(BF16), wmq_ref[...])
    heads = []
    for h in range(MEM_HEADS):
        sl = slice(h * MEM_HEAD_DIM, (h + 1) * MEM_HEAD_DIM)
        heads.append((_rms(qm[:, sl]) * (gmq_ref[...] * MEM_SCALE)).astype(BF16))
    return heads


def _merge_prompt_kernel(x_ref, an_ref, b_ref, gob_ref, wo_ref, gmx_ref, wmq_ref, gmq_ref, mk_ref, mv_ref, wmo_ref,
                         y_ref, o_s):
    bn = (_rms(b_ref[...]) * gob_ref[...]).astype(BF16)
    x1 = x_ref[...] + _dot(an_ref[...], wo_ref[0:A_WIDTH, :]) + _dot(bn, wo_ref[A_WIDTH:, :])
    heads = _mem_query(x1, gmx_ref, wmq_ref, gmq_ref)
    for h in range(MEM_HEADS):
        sl = slice(h * MEM_HEAD_DIM, (h + 1) * MEM_HEAD_DIM)
        s = _dot_nt(heads[h], mk_ref[0, :, sl].astype(BF16))
        p = jnp.exp(s - jnp.max(s, axis=-1, keepdims=True))
        o = _dot(p.astype(BF16), mv_ref[0, :, sl].astype(BF16)) / jnp.sum(p, axis=-1, keepdims=True)
        o_s[:, sl] = o.astype(BF16)
    y_ref[...] = x1 + _dot(o_s[...], wmo_ref[...])


def _merge_prompt(x, a_n, b_out, mk, mv, w, seq):
    m = x.shape[0]
    tm = TOK_TILE
    per_seq = seq // tm
    mem_len = mk.shape[1]
    row = lambda width: pl.BlockSpec((tm, width), lambda i: (i, 0))
    mem = pl.BlockSpec((1, mem_len, MEM_WIDTH), lambda i: (i // per_seq, 0, 0))
    c = lambda name: _const_spec(w[name].shape)
    return pl.pallas_call(
        _merge_prompt_kernel, grid=(m // tm,),
        in_specs=[row(D_MODEL), row(A_WIDTH), row(B_WIDTH), c('g_out_b'), c('w_o'), c('g_mem_x'), c('w_mq'),
                  c('g_mq'), mem, mem, c('w_mo')],
        out_specs=row(D_MODEL),
        out_shape=jax.ShapeDtypeStruct((m, D_MODEL), F32),
        scratch_shapes=[pltpu.VMEM((tm, MEM_WIDTH), BF16)],
        compiler_params=_params("arbitrary"),
        name="merge_prompt",
    )(x, a_n, b_out, w['g_out_b'], w['w_o'], w['g_mem_x'], w['w_mq'], w['g_mq'], mk, mv, w['w_mo'])


def _merge_sample_kernel(x_ref, an_ref, ctx_ref, wuvb_ref, gob_ref, wo_ref, gmx_ref, wmq_ref, gmq_ref,
                         x1_ref, qm_ref):
    b = _dot(ctx_ref[...].astype(BF16), wuvb_ref[...])
    bn = (_rms(b) * gob_ref[...]).astype(BF16)
    x1 = x_ref[...] + _dot(an_ref[...], wo_ref[0:A_WIDTH, :]) + _dot(bn, wo_ref[A_WIDTH:, :])
    x1_ref[...] = x1
    heads = _mem_query(x1, gmx_ref, wmq_ref, gmq_ref)
    for h in range(MEM_HEADS):
        qm_ref[:, h * MEM_HEAD_DIM:(h + 1) * MEM_HEAD_DIM] = heads[h]


def _merge_sample(x, a_n, ctx, w):
    m = x.shape[0]
    tm = TOK_TILE
    row = lambda width: pl.BlockSpec((tm, width), lambda i: (i, 0))
    c = lambda name: _const_spec(w[name].shape)
    return pl.pallas_call(
        _merge_sample_kernel, grid=(m // tm,),
        in_specs=[row(D_MODEL), row(A_WIDTH), row(ctx.shape[1]), c('w_uv_heads'), c('g_out_b'), c('w_o'),
                  c('g_mem_x'), c('w_mq'), c('g_mq')],
        out_specs=[row(D_MODEL), row(MEM_WIDTH)],
        out_shape=[jax.ShapeDtypeStruct((m, D_MODEL), F32), jax.ShapeDtypeStruct((m, MEM_WIDTH), BF16)],
        compiler_params=_params("arbitrary"),
        name="merge_sample",
    )(x, a_n, ctx, w['w_uv_heads'], w['g_out_b'], w['w_o'], w['g_mem_x'], w['w_mq'], w['g_mq'])


def _memattn_sample_kernel(q_ref, mk_ref, mv_ref, o_ref):
    for h in range(MEM_HEADS):
        sl = slice(h * MEM_HEAD_DIM, (h + 1) * MEM_HEAD_DIM)
        s = jnp.einsum('ntd,nmd->ntm', q_ref[:, :, sl], mk_ref[:, :, sl].astype(BF16),
                       preferred_element_type=F32)
        p = jnp.exp(s - jnp.max(s, axis=-1, keepdims=True))
        o = jnp.einsum('ntm,nmd->ntd', p.astype(BF16), mv_ref[:, :, sl].astype(BF16),
                       preferred_element_type=F32)
        o_ref[:, :, sl] = (o / jnp.sum(p, axis=-1, keepdims=True)).astype(BF16)


def _memattn_sample(qm, mk, mv):
    n_seq, t_q, _ = qm.shape
    mem_len = mk.shape[1]
    ns = MEM_SEQ_TILE
    return pl.pallas_call(
        _memattn_sample_kernel, grid=(n_seq // ns,),
        in_specs=[pl.BlockSpec((ns, t_q, MEM_WIDTH), lambda i: (i, 0, 0)),
                  pl.BlockSpec((ns, mem_len, MEM_WIDTH), lambda i: (i, 0, 0)),
                  pl.BlockSpec((ns, mem_len, MEM_WIDTH), lambda i: (i, 0, 0))],
        out_specs=pl.BlockSpec((ns, t_q, MEM_WIDTH), lambda i: (i, 0, 0)),
        out_shape=jax.ShapeDtypeStruct((n_seq, t_q, MEM_WIDTH), BF16),
        compiler_params=_params("arbitrary"),
        name="memattn_sample",
    )(qm, mk, mv)


def _ffn_kernel(*refs, sample, tiles_per_seq):
    if sample:
        (x_ref, o_ref, wmo_ref, gffn_ref, wg_ref, wv_ref, wd_ref, wc_ref, bc_ref, p1_ref, p2_ref,
         y_ref, gate_ref, h_s, acc_s, g_s) = refs
        x2 = x_ref[...] + _dot(o_ref[...], wmo_ref[...])
    else:
        (x_ref, gffn_ref, wg_ref, wv_ref, wd_ref, wc_ref, bc_ref, y_ref, tail_ref, h_s, acc_s, g_s, carry_s) = refs
        x2 = x_ref[...]
        first = (pl.program_id(0) % tiles_per_seq) == 0

        @pl.when(pl.program_id(0) == 0)
        def _():
            carry_s[...] = jnp.zeros_like(carry_s)
    tm = x_ref.shape[0]
    h_s[...] = (_rms(x2) * gffn_ref[...]).astype(BF16)
    acc_s[...] = x2
    if sample:
        t_in_seq = lax.broadcasted_iota(jnp.int32, (tm, 1), 0) % tiles_per_seq
        g_s[0:8, :] = jnp.zeros((8, FF_TILE), F32)

    def body(f, _):
        h = h_s[...]
        g = _dot(h, wg_ref[f])
        val = _dot(h, wv_ref[f])
        wc = wc_ref[f]
        if sample:
            gate_ref[f] = g
        else:
            g_s[0:8, :] = jnp.where(first, 0.0, carry_s[f])
            carry_s[f] = g[tm - 8:, :]
            tail_ref[0, f] = g[tm - 8:, :]
        g_s[8:, :] = g
        g1 = g_s[7:7 + tm, :]
        g2 = g_s[6:6 + tm, :]
        if sample:
            g1 = jnp.where(t_in_seq >= 1, g1, p1_ref[f])
            g2 = jnp.where(t_in_seq >= 2, g2, p2_ref[f])
        conv = bc_ref[f] + g2 * wc[0:1, :] + g1 * wc[1:2, :] + g * wc[2:3, :]
        act = (conv / (1.0 + jnp.exp(-conv)) * val).astype(BF16)
        acc_s[...] += _dot(act, wd_ref[f])
        return 0

    lax.fori_loop(0, N_FF_TILES, body, 0)
    y_ref[...] = acc_s[...]


def _ffn_prompt(x2, w, n_batch, seq):
    m = x2.shape[0]
    tm = TOK_TILE
    per_seq = seq // tm
    row = pl.BlockSpec((tm, D_MODEL), lambda i: (i, 0))
    c = lambda name: _const_spec(w[name].shape)
    y, tail = pl.pallas_call(
        functools.partial(_ffn_kernel, sample=False, tiles_per_seq=per_seq), grid=(m // tm,),
        in_specs=[row, c('g_ffn'), c('w_gate'), c('w_val'), c('w_down'), c('w_conv'), c('b_conv')],
        out_specs=[row, pl.BlockSpec((1, N_FF_TILES, 8, FF_TILE), lambda i: (i // per_seq, 0, 0, 0))],
        out_shape=[jax.ShapeDtypeStruct((m, D_MODEL), F32),
                   jax.ShapeDtypeStruct((n_batch, N_FF_TILES, 8, FF_TILE), F32)],
        scratch_shapes=[pltpu.VMEM((tm, D_MODEL), BF16), pltpu.VMEM((tm, D_MODEL), F32),
                        pltpu.VMEM((tm + 8, FF_TILE), F32), pltpu.VMEM((N_FF_TILES, 8, FF_TILE), F32)],
        compiler_params=_params("arbitrary"),
        name="ffn_prompt",
    )(x2, w['g_ffn'], w['w_gate'], w['w_val'], w['w_down'], w['w_conv'], w['b_conv'])
    return y, tail


def _ffn_sample(x1, o, prev1, prev2, w, t_q):
    m = x1.shape[0]
    tm = TOK_TILE
    row = lambda width: pl.BlockSpec((tm, width), lambda i: (i, 0))
    c = lambda name: _const_spec(w[name].shape)
    ftile = pl.BlockSpec((N_FF_TILES, tm, FF_TILE), lambda i: (0, i, 0))
    return pl.pallas_call(
        functools.partial(_ffn_kernel, sample=True, tiles_per_seq=t_q), grid=(m // tm,),
        in_specs=[row(D_MODEL), row(MEM_WIDTH), c('w_mo'), c('g_ffn'), c('w_gate'), c('w_val'), c('w_down'),
                  c('w_conv'), c('b_conv'), ftile, ftile],
        out_specs=[row(D_MODEL), ftile],
        out_shape=[jax.ShapeDtypeStruct((m, D_MODEL), F32), jax.ShapeDtypeStruct((N_FF_TILES, m, FF_TILE), F32)],
        scratch_shapes=[pltpu.VMEM((tm, D_MODEL), BF16), pltpu.VMEM((tm, D_MODEL), F32),
                        pltpu.VMEM((tm + 8, FF_TILE), F32)],
        compiler_params=_params("arbitrary"),
        name="ffn_sample",
    )(x1, o, w['w_mo'], w['g_ffn'], w['w_gate'], w['w_val'], w['w_down'], w['w_conv'], w['b_conv'], prev1, prev2)


def _head_pad(wm, width):
    k = wm.shape[0]
    wm = wm.reshape(k, MLA_HEADS, width)
    return jnp.pad(wm, ((0, 0), (0, 0), (0, HEAD_PAD - width))).reshape(k, MLA_HEADS * HEAD_PAD)


def _rot_half_cols(wm):
    half = QK_ROPE // 2
    return jnp.concatenate([-wm[..., half:], wm[..., :half]], axis=-1)


def _lane_vec(nope, rope):
    return jnp.concatenate([nope, rope, jnp.zeros((LANES - QK_HEAD,), F32)])[None, :]


def _rope_tables(pos):
    half = QK_ROPE // 2
    inv_freq = ROPE_THETA ** (-jnp.arange(half, dtype=F32) / half)
    ang = pos.astype(F32)[:, None] * inv_freq[None, :]
    return jnp.cos(ang), jnp.sin(ang)


def _head_tile_tables(pos):
    cos, sin = _rope_tables(pos)
    n = pos.shape[0]
    cos_t = jnp.concatenate([jnp.ones((n, QK_NOPE), F32), cos, cos, jnp.zeros((n, LANES - QK_HEAD), F32)], axis=1)
    sin_t = jnp.concatenate([jnp.zeros((n, QK_NOPE), F32), sin, sin, jnp.zeros((n, LANES - QK_HEAD), F32)], axis=1)
    return cos_t, sin_t


def _feature_table(pos, rows):
    cos, sin = _rope_tables(pos)
    t = jnp.concatenate([cos, cos, sin, sin, jnp.zeros((pos.shape[0], LANES - 2 * QK_ROPE), F32)], axis=1)
    return jnp.pad(t, ((0, rows - pos.shape[0]), (0, 0)))


def _query_feature_perm():
    half = QK_ROPE // 2
    p = np.zeros((LANES, LANES), np.float32)
    for j in range(QK_ROPE):
        p[QK_NOPE + j, j] = 1.0
    for j in range(half):
        p[QK_NOPE + half + j, QK_ROPE + j] = 1.0
        p[QK_NOPE + j, QK_ROPE + half + j] = -1.0
    return jnp.asarray(p, BF16)


def _prep_weights(l, t_q, g_mix, w_in, ln_v_g, ln_v_b, w_s, b_s, g_q_a, w_uq, g_kv_a, w_uk, w_uv, g_qk_q, g_qk_k,
                  g_out_a, g_out_b, w_o, g_mem_x, g_mem_in, w_mq, w_mk, w_mv, g_mq, g_mk, w_mo, g_ffn, w_up,
                  w_conv, b_conv, w_down):
    half = QK_ROPE // 2
    o_kpe = 2 * A_WIDTH + Q_LORA + KV_LORA
    vec = lambda a: a.reshape(1, -1).astype(F32)
    w = {}
    wkpe = w_in[l][:, o_kpe:]
    lane_tile = lambda blk: jnp.pad(blk, ((0, 0), (QK_NOPE, LANES - QK_HEAD)))
    w['w_in'] = jnp.concatenate([w_in[l][:, :o_kpe], lane_tile(wkpe), lane_tile(_rot_half_cols(wkpe))],
                                axis=1).astype(BF16)
    w['g_mix'] = vec(g_mix[l])
    w['ln_g'] = vec(ln_v_g[l])
    w['ln_b'] = vec(ln_v_b[l])
    w['g_q_a'] = vec(g_q_a[l])
    w['g_kv_a'] = vec(g_kv_a[l])
    w['g_out_a'] = vec(g_out_a[l])
    w['g_out_b'] = vec(g_out_b[l])
    wq = w_uq[l].reshape(Q_LORA, MLA_HEADS, QK_HEAD)
    wq_rot = jnp.concatenate([jnp.zeros((Q_LORA, MLA_HEADS, QK_NOPE), F32), _rot_half_cols(wq[..., QK_NOPE:])], axis=-1)
    w['wq'] = _head_pad(wq.reshape(Q_LORA, -1), QK_HEAD).astype(BF16)
    w['wq_rot'] = _head_pad(wq_rot.reshape(Q_LORA, -1), QK_HEAD).astype(BF16)
    w['wk'] = _head_pad(w_uk[l], QK_NOPE).astype(BF16)
    wv = w_uv[l].reshape(KV_LORA, MLA_HEADS // 2, 2, V_HEAD)
    z = jnp.zeros_like(wv[:, :, 0])
    w['wv'] = jnp.stack([jnp.concatenate([wv[:, :, 0], z], -1), jnp.concatenate([z, wv[:, :, 1]], -1)],
                        axis=2).reshape(KV_LORA, MLA_HEADS * HEAD_PAD).astype(BF16)
    gq, gk = g_qk_q[l], g_qk_k[l]
    sw = lambda g: jnp.concatenate([g[QK_NOPE + half:], g[QK_NOPE:QK_NOPE + half]])
    zeros_n = jnp.zeros((QK_NOPE,), F32)
    w['gq'] = _lane_vec(gq[:QK_NOPE], gq[QK_NOPE:])
    w['gq_sw'] = _lane_vec(zeros_n, sw(gq))
    w['gk'] = _lane_vec(gk[:QK_NOPE], gk[QK_NOPE:])
    w['gk_sw'] = _lane_vec(zeros_n, sw(gk))
    w['gk_nope'] = _lane_vec(gk[:QK_NOPE], jnp.zeros((QK_ROPE,), F32))
    w['pq'] = _query_feature_perm()
    w['wukt'] = w_uk[l].T.astype(BF16)
    gr = gk[QK_NOPE:]
    w['gvec'] = jnp.concatenate([gr, gr, jnp.zeros((LANES - 2 * QK_ROPE,), F32)])[None, :]
    wuv = w_uv[l].reshape(KV_LORA, MLA_HEADS, V_HEAD)
    eye = jnp.eye(MLA_HEADS, dtype=F32)
    w['w_uv_heads'] = (wuv[None, :, :, :] * eye[:, None, :, None]).reshape(MLA_HEADS * KV_LORA, B_WIDTH).astype(BF16)
    tril = jnp.tril(jnp.ones((CHUNK, CHUNK), F32))
    w['ws_prompt'] = (w_s[l] * tril).astype(BF16)
    w['bs_prompt'] = jnp.repeat(b_s[l].T, A_GROUP_DIM, axis=1)
    reps = CHUNK // t_q
    blk = w_s[l][:, :t_q, :t_q] * jnp.tril(jnp.ones((t_q, t_q), F32))
    w['ws_sample'] = jnp.einsum('ij,gts->gitjs', jnp.eye(reps, dtype=F32), blk).reshape(A_GROUPS, CHUNK, CHUNK).astype(BF16)
    w['bs_sample'] = jnp.repeat(jnp.tile(b_s[l][:, :t_q].T, (reps, 1)), A_GROUP_DIM, axis=1)
    w['w_o'] = w_o[l].astype(BF16)
    w['g_mem_x'] = vec(g_mem_x[l])
    w['g_mem_in'] = vec(g_mem_in[l])
    w['w_mq'] = w_mq[l].astype(BF16)
    w['w_mk'] = w_mk[l].astype(BF16)
    w['w_mv'] = w_mv[l].astype(BF16)
    w['g_mq'] = vec(g_mq[l])
    w['g_mk'] = vec(g_mk[l])
    w['w_mo'] = w_mo[l].astype(BF16)
    w['g_ffn'] = vec(g_ffn[l])
    ftiles = lambda a: a.reshape(a.shape[0], N_FF_TILES, FF_TILE).transpose(1, 0, 2)
    w['w_gate'] = ftiles(w_up[l][:, :D_FF]).astype(BF16)
    w['w_val'] = ftiles(w_up[l][:, D_FF:]).astype(BF16)
    w['w_down'] = w_down[l].reshape(N_FF_TILES, FF_TILE, D_MODEL).astype(BF16)
    w['w_conv'] = ftiles(w_conv[l])
    w['b_conv'] = ftiles(b_conv[l][None, :])
    return w


def kernel(x_prompt, x_sample, cache_ckv, cache_kpe, cache_mem_k, cache_mem_v, state_ffn_conv, page_table, mem_prompt, g_mix, w_in, ln_v_g, ln_v_b, w_s, b_s, g_q_a, w_uq, g_kv_a, w_uk, w_uv, g_qk_q, g_qk_k, g_out_a, g_out_b, w_o, g_mem_x, g_mem_in, w_mq, w_mk, w_mv, g_mq, g_mk, w_mo, g_ffn, w_up, w_conv, b_conv, w_down):
    n_p, seq, _ = x_prompt.shape
    n_s, t_q, _ = x_sample.shape
    depth = g_mix.shape[0]
    mem_len = mem_prompt.shape[1]
    n_pages = page_table.shape[1]
    past = n_pages * PAGE_SIZE

    pos_p = jnp.arange(seq)
    pos_s = past + jnp.arange(t_q)
    cos_p, sin_p = _head_tile_tables(pos_p)
    cos_s, sin_s = _head_tile_tables(jnp.tile(pos_s, TOK_TILE // t_q))
    t_past = _feature_table(jnp.arange(past), past)
    t_new = _feature_table(pos_s, LANES)

    xp = x_prompt.reshape(n_p * seq, D_MODEL)
    xs = x_sample.reshape(n_s * t_q, D_MODEL)
    outs = {k: [] for k in ('p_ckv', 'p_kpe', 'p_mk', 'p_mv', 'p_conv', 's_ckv', 's_kpe', 's_chunk_v', 's_conv')}
    for l in range(depth):
        w = _prep_weights(l, t_q, g_mix, w_in, ln_v_g, ln_v_b, w_s, b_s, g_q_a, w_uq, g_kv_a, w_uk, w_uv, g_qk_q,
                          g_qk_k, g_out_a, g_out_b, w_o, g_mem_x, g_mem_in, w_mq, w_mk, w_mv, g_mq, g_mk, w_mo,
                          g_ffn, w_up, w_conv, b_conv, w_down)
        w['t_past'], w['t_new'] = t_past, t_new

        wp = dict(w, ws=w['ws_prompt'], bs=w['bs_prompt'])
        q, k, v, ckvn, kpe, a_n = _inproj(xp, cos_p, sin_p, wp, sample=False)
        b_out = _flash(q, k, v, n_p, seq)
        mk, mv = _memkv(mem_prompt.reshape(n_p * mem_len, D_MODEL), w)
        mk3 = mk.reshape(n_p, mem_len, MEM_WIDTH)
        mv3 = mv.reshape(n_p, mem_len, MEM_WIDTH)
        x2 = _merge_prompt(xp, a_n, b_out, mk3, mv3, w, seq)
        xp, tail = _ffn_prompt(x2, w, n_p, seq)
        outs['p_ckv'].append(ckvn.reshape(n_p, seq, KV_LORA))
        outs['p_kpe'].append(kpe.reshape(n_p, seq, QK_ROPE))
        outs['p_mk'].append(mk.reshape(n_p, mem_len, MEM_HEADS, MEM_HEAD_DIM))
        outs['p_mv'].append(mv.reshape(n_p, mem_len, MEM_HEADS, MEM_HEAD_DIM))
        outs['p_conv'].append(tail[:, :, 8 - (CONV_W - 1):, :].transpose(0, 2, 1, 3).reshape(n_p, CONV_W - 1, D_FF))

        wsm = dict(w, ws=w['ws_sample'], bs=w['bs_sample'])
        qlat, qrf, ckvn_s, kpe_s, a_ns, vg = _inproj(xs, cos_s, sin_s, wsm, sample=True)
        nq = t_q * MLA_HEADS
        ctx = _paged(page_table, qlat.reshape(n_s, nq, KV_LORA), qrf.reshape(n_s, nq, LANES),
                     ckvn_s.reshape(n_s, t_q, KV_LORA), kpe_s.reshape(n_s, t_q, QK_ROPE),
                     cache_ckv[l], cache_kpe[l], w)
        x1, qm = _merge_sample(xs, a_ns, ctx.reshape(n_s * t_q, MLA_HEADS * KV_LORA), w)
        o = _memattn_sample(qm.reshape(n_s, t_q, MEM_WIDTH),
                            cache_mem_k[l].reshape(n_s, mem_len, MEM_WIDTH),
                            cache_mem_v[l].reshape(n_s, mem_len, MEM_WIDTH))
        st = state_ffn_conv[l]
        zero = jnp.zeros_like(st[:, :1])
        prev1 = jnp.concatenate([st[:, 1:2]] + [zero] * (t_q - 1), axis=1)
        prev2 = jnp.concatenate([st[:, 0:1], st[:, 1:2]] + [zero] * (t_q - 2), axis=1)
        ftile = lambda a: a.reshape(n_s * t_q, N_FF_TILES, FF_TILE).transpose(1, 0, 2)
        xs, gate = _ffn_sample(x1, o.reshape(n_s * t_q, MEM_WIDTH), ftile(prev1), ftile(prev2), w, t_q)
        gate = gate.transpose(1, 0, 2).reshape(n_s, t_q, D_FF)
        outs['s_ckv'].append(ckvn_s.reshape(n_s, t_q, KV_LORA))
        outs['s_kpe'].append(kpe_s.reshape(n_s, t_q, QK_ROPE))
        outs['s_chunk_v'].append(vg.reshape(n_s, t_q, A_GROUPS, A_GROUP_DIM))
        outs['s_conv'].append(gate[:, t_q - (CONV_W - 1):, :])

    return (xp.reshape(n_p, seq, D_MODEL), xs.reshape(n_s, t_q, D_MODEL),
            jnp.stack(outs['p_ckv']), jnp.stack(outs['p_kpe']), jnp.stack(outs['p_mk']), jnp.stack(outs['p_mv']),
            jnp.stack(outs['p_conv']), jnp.stack(outs['s_ckv']), jnp.stack(outs['s_kpe']),
            jnp.stack(outs['s_chunk_v']), jnp.stack(outs['s_conv']))
```

```python
import functools

import numpy as np
import jax
import jax.numpy as jnp
from jax import lax
from jax.experimental import pallas as pl
from jax.experimental.pallas import tpu as pltpu

D_MODEL = 1024
A_GROUPS = 8
A_GROUP_DIM = 64
A_WIDTH = A_GROUPS * A_GROUP_DIM
CHUNK = 128
MLA_HEADS = 8
Q_LORA = 256
KV_LORA = 256
QK_NOPE = 64
QK_ROPE = 32
QK_HEAD = QK_NOPE + QK_ROPE
V_HEAD = 64
B_WIDTH = MLA_HEADS * V_HEAD
ROPE_THETA = 10000.0
PAGE_SIZE = 128
MEM_HEADS = 4
MEM_HEAD_DIM = 128
MEM_WIDTH = MEM_HEADS * MEM_HEAD_DIM
D_FF = 2816
CONV_W = 3
EPS = 1e-6
NEG_INF = -1e30

LANES = 128
HEAD_PAD = LANES
QK_SCALE = QK_HEAD ** -0.5
MEM_SCALE = MEM_HEAD_DIM ** -0.5

TOK_TILE = 512
FF_TILE = 256
N_FF_TILES = D_FF // FF_TILE
ATT_TILE = 256
KV_TILE = 1024
MEM_SEQ_TILE = 8
VMEM_LIMIT = 56 * 1024 * 1024

F32 = jnp.float32
BF16 = jnp.bfloat16


def _dot(a, b):
    return jnp.dot(a, b, preferred_element_type=F32)


def _dot_nt(a, b):
    return lax.dot_general(a, b, (((1,), (1,)), ((), ())), preferred_element_type=F32)


def _rms(x):
    return x * lax.rsqrt(jnp.mean(x * x, axis=-1, keepdims=True) + EPS)


def _gelu(x):
    return jax.nn.gelu(x, approximate=True)


def _const_spec(shape):
    nd = len(shape)
    return pl.BlockSpec(shape, lambda *_: (0,) * nd, pipeline_mode=pl.Buffered(1))


def _params(*sem):
    return pltpu.CompilerParams(dimension_semantics=sem, vmem_limit_bytes=VMEM_LIMIT)


def _inproj_kernel(x_ref, gmix_ref, win_ref, lng_ref, lnb_ref, ws_ref, bs_ref, gqa_ref, wq_ref, wqr_ref,
                   gkva_ref, wk_ref, wv_ref, gq_ref, gqs_ref, gk_ref, gks_ref, cos_ref, sin_ref, goa_ref,
                   pq_ref, *rest, sample):
    if sample:
        qlat_ref, qrf_ref, ckvn_ref, kpe_ref, an_ref, vg_ref, zu_s, vgb_s, a_s = rest
    else:
        q_ref, k_ref, v_ref, ckvn_ref, kpe_ref, an_ref, zu_s, vgb_s, a_s = rest
    tm = x_ref.shape[0]
    xn = (_rms(x_ref[...]) * gmix_ref[...]).astype(BF16)

    zu_s[...] = _dot(xn, win_ref[:, 0:A_WIDTH])
    gv = _gelu(_dot(xn, win_ref[:, A_WIDTH:2 * A_WIDTH]))
    low = lax.broadcasted_iota(jnp.int32, (1, LANES), 1) < A_GROUP_DIM
    inv_g = 1.0 / A_GROUP_DIM
    for p in range(A_WIDTH // LANES):
        sl = slice(p * LANES, (p + 1) * LANES)
        t = gv[:, sl]
        s_lo = jnp.sum(jnp.where(low, t, 0.0), axis=-1, keepdims=True)
        s_hi = jnp.sum(jnp.where(low, 0.0, t), axis=-1, keepdims=True)
        tc = t - jnp.where(low, s_lo, s_hi) * inv_g
        tc2 = tc * tc
        v_lo = jnp.sum(jnp.where(low, tc2, 0.0), axis=-1, keepdims=True)
        v_hi = jnp.sum(jnp.where(low, 0.0, tc2), axis=-1, keepdims=True)
        y = tc * lax.rsqrt(jnp.where(low, v_lo, v_hi) * inv_g + EPS) * lng_ref[:, sl] + lnb_ref[:, sl]
        if sample:
            vg_ref[:, sl] = y
        vgb_s[:, sl] = y.astype(BF16)
    for c in range(tm // CHUNK):
        rows = slice(c * CHUNK, (c + 1) * CHUNK)
        for p in range(A_WIDTH // LANES):
            sl = slice(p * LANES, (p + 1) * LANES)
            vp = vgb_s[rows, sl]
            mixed = jnp.where(low, _dot(ws_ref[2 * p], vp), _dot(ws_ref[2 * p + 1], vp)) + bs_ref[:, sl]
            a_s[rows, sl] = _gelu(zu_s[rows, sl]) * mixed
    an_ref[...] = (_rms(a_s[...]) * goa_ref[...]).astype(BF16)

    o_cq = 2 * A_WIDTH
    o_ckv = o_cq + Q_LORA
    o_kpe = o_ckv + KV_LORA
    cqn = (_rms(_dot(xn, win_ref[:, o_cq:o_ckv])) * gqa_ref[...]).astype(BF16)
    ckvn = _rms(_dot(xn, win_ref[:, o_ckv:o_kpe])) * gkva_ref[...]
    ckvn_ref[...] = ckvn
    kpt = _dot(xn, win_ref[:, o_kpe:o_kpe + LANES])
    kpe_ref[...] = kpt[:, QK_NOPE:QK_HEAD]
    cosv = cos_ref[...]
    sinv = sin_ref[...]
    qa = gq_ref[...] * cosv * QK_SCALE
    qb = gqs_ref[...] * sinv * QK_SCALE
    qall = _dot(cqn, wq_ref[...])
    qrall = _dot(cqn, wqr_ref[...])
    inv_d = 1.0 / QK_HEAD
    if not sample:
        cb = ckvn.astype(BF16)
        kpr = _dot(xn, win_ref[:, o_kpe + LANES:o_kpe + 2 * LANES])
        ka = gk_ref[...] * cosv
        kb = gks_ref[...] * sinv
        kall = _dot(cb, wk_ref[...])
        vt = _dot_nt(wv_ref[...], cb).astype(BF16)
        for c in range(tm // ATT_TILE):
            v_ref[c] = vt[:, c * ATT_TILE:(c + 1) * ATT_TILE]
    for h in range(MLA_HEADS):
        sl = slice(h * HEAD_PAD, (h + 1) * HEAD_PAD)
        qh = qall[:, sl]
        r = lax.rsqrt(jnp.sum(qh * qh, axis=-1, keepdims=True) * inv_d + EPS)
        qf = r * (qh * qa + qrall[:, sl] * qb)
        if sample:
            qg = (qf * gk_ref[...]).astype(BF16)
            qlat_ref[:, h * KV_LORA:(h + 1) * KV_LORA] = _dot_nt(qg, wk_ref[:, sl]).astype(BF16)
            qrf_ref[:, sl] = _dot(qf.astype(BF16), pq_ref[...]).astype(BF16)
        else:
            q_ref[:, sl] = qf.astype(BF16)
            kh = kall[:, sl] + kpt
            rk = lax.rsqrt(jnp.sum(kh * kh, axis=-1, keepdims=True) * inv_d + EPS)
            k_ref[:, sl] = (rk * (kh * ka + kpr * kb)).astype(BF16)


def _inproj(x, cos, sin, w, *, sample):
    m = x.shape[0]
    tm = TOK_TILE
    n_tiles = m // tm
    n_pos_tiles = cos.shape[0] // tm
    row = lambda width: pl.BlockSpec((tm, width), lambda i: (i, 0))
    pos = pl.BlockSpec((tm, LANES), lambda i: (i % n_pos_tiles, 0))
    consts = [w['g_mix'], w['w_in'], w['ln_g'], w['ln_b'], w['ws'], w['bs'], w['g_q_a'], w['wq'], w['wq_rot'],
              w['g_kv_a'], w['wk'], w['wvt'], w['gq'], w['gq_sw'], w['gk_nope'] if sample else w['gk'], w['gk_sw']]
    in_specs = ([row(D_MODEL)] + [_const_spec(c.shape) for c in consts] + [pos, pos]
                + [_const_spec(w['g_out_a'].shape), _const_spec(w['pq'].shape)])
    args = [x] + consts + [cos, sin, w['g_out_a'], w['pq']]
    wide = MLA_HEADS * HEAD_PAD
    if sample:
        out_shape = [jax.ShapeDtypeStruct((m, MLA_HEADS * KV_LORA), BF16), jax.ShapeDtypeStruct((m, wide), BF16),
                     jax.ShapeDtypeStruct((m, KV_LORA), F32), jax.ShapeDtypeStruct((m, QK_ROPE), F32),
                     jax.ShapeDtypeStruct((m, A_WIDTH), BF16), jax.ShapeDtypeStruct((m, A_WIDTH), F32)]
        out_specs = [row(MLA_HEADS * KV_LORA), row(wide), row(KV_LORA), row(QK_ROPE), row(A_WIDTH), row(A_WIDTH)]
    else:
        att_tiles = tm // ATT_TILE
        out_shape = [jax.ShapeDtypeStruct((m, wide), BF16)] * 2 + [
            jax.ShapeDtypeStruct((m // ATT_TILE, B_WIDTH, ATT_TILE), BF16),
            jax.ShapeDtypeStruct((m, KV_LORA), F32), jax.ShapeDtypeStruct((m, QK_ROPE), F32),
            jax.ShapeDtypeStruct((m, A_WIDTH), BF16)]
        out_specs = [row(wide)] * 2 + [pl.BlockSpec((att_tiles, B_WIDTH, ATT_TILE), lambda i: (i, 0, 0)),
                                       row(KV_LORA), row(QK_ROPE), row(A_WIDTH)]
    return pl.pallas_call(
        functools.partial(_inproj_kernel, sample=sample),
        grid=(n_tiles,), in_specs=in_specs, out_specs=out_specs, out_shape=out_shape,
        scratch_shapes=[pltpu.VMEM((tm, A_WIDTH), F32), pltpu.VMEM((tm, A_WIDTH), BF16),
                        pltpu.VMEM((tm, A_WIDTH), F32)],
        compiler_params=_params("arbitrary"),
        name="inproj_sample" if sample else "inproj_prompt",
    )(*args)


def _flash_kernel(q_ref, k_ref, vt_ref, gob_ref, o_ref):
    i = pl.program_id(1)
    t = ATT_TILE
    causal = lax.broadcasted_iota(jnp.int32, (t, t), 0) <= lax.broadcasted_iota(jnp.int32, (t, t), 1)

    def step(j, carry, masked):
        start = pl.multiple_of(j * t, t)
        vt = vt_ref[j]
        heads = [slice(h * HEAD_PAD, (h + 1) * HEAD_PAD) for h in range(MLA_HEADS)]
        scores = [_dot_nt(k_ref[pl.ds(start, t), sl], q_ref[:, sl]) for sl in heads]
        stats = []
        for h in range(MLA_HEADS):
            m, l, _ = carry[h]
            s = jnp.where(causal, scores[h], NEG_INF) if masked else scores[h]
            m_new = jnp.maximum(m, jnp.max(s, axis=0, keepdims=True))
            alpha = jnp.exp(m - m_new)
            p = jnp.exp(s - m_new)
            stats.append((m_new, alpha * l + jnp.sum(p, axis=0, keepdims=True), alpha, p.astype(BF16)))
        return tuple((m_new, l, alpha * carry[h][2] + _dot(vt[h * V_HEAD:(h + 1) * V_HEAD, :], p))
                     for h, (m_new, l, alpha, p) in enumerate(stats))

    init = tuple((jnp.full((1, t), NEG_INF, F32), jnp.zeros((1, t), F32), jnp.zeros((V_HEAD, t), F32))
                 for _ in range(MLA_HEADS))
    carry = lax.fori_loop(0, i, functools.partial(step, masked=False), init)
    carry = step(i, carry, True)
    o_t = jnp.concatenate([acc / l for _, l, acc in carry], axis=0)
    bn_t = o_t * lax.rsqrt(jnp.mean(o_t * o_t, axis=0, keepdims=True) + EPS) * gob_ref[...]
    o_ref[...] = bn_t.T.astype(BF16)


def _flash(q, k, vt, g_out_b_col, n_batch, seq):
    m = q.shape[0]
    nq = seq // ATT_TILE
    wide = MLA_HEADS * HEAD_PAD
    return pl.pallas_call(
        _flash_kernel,
        grid=(n_batch, nq),
        in_specs=[pl.BlockSpec((ATT_TILE, wide), lambda b, i: (b * nq + i, 0)),
                  pl.BlockSpec((seq, wide), lambda b, i: (b, 0)),
                  pl.BlockSpec((nq, B_WIDTH, ATT_TILE), lambda b, i: (b, 0, 0)),
                  pl.BlockSpec((B_WIDTH, 1), lambda b, i: (0, 0))],
        out_specs=pl.BlockSpec((ATT_TILE, B_WIDTH), lambda b, i: (b * nq + i, 0)),
        out_shape=jax.ShapeDtypeStruct((m, B_WIDTH), BF16),
        compiler_params=_params("arbitrary", "arbitrary"),
        name="flash_prompt",
    )(q, k, vt, g_out_b_col)


def _paged_kernel(pt_ref, qlat_ref, qrf_ref, cnew_ref, kpnew_ref, wukt_ref, gcol_ref, tpast_ref, tnew_ref,
                  ckv_hbm, kpe_hbm, ctx_ref, cbuf, pbuf, cn_s, tg_s, sems, *, n_seq, n_pages):
    n = pl.program_id(0)
    slot = n % 2
    ppt = KV_TILE // PAGE_SIZE
    n_tiles = n_pages // ppt

    def start_fetch(seq, sl):
        def body(p, _):
            page = pt_ref[seq, p]
            pltpu.make_async_copy(ckv_hbm.at[page], cbuf.at[sl, p], sems.at[0, sl]).start()
            pltpu.make_async_copy(kpe_hbm.at[page], pbuf.at[sl, p], sems.at[1, sl]).start()
            return 0
        lax.fori_loop(0, n_pages, body, 0, unroll=8)

    def wait_fetch(sl):
        pltpu.make_async_copy(ckv_hbm.at[pl.ds(0, n_pages)], cbuf.at[sl], sems.at[0, sl]).wait()
        pltpu.make_async_copy(kpe_hbm.at[pl.ds(0, n_pages)], pbuf.at[sl], sems.at[1, sl]).wait()

    @pl.when(n == 0)
    def _():
        cn_s[...] = jnp.zeros_like(cn_s)
        for t in range(n_tiles):
            tg_s[t] = tpast_ref[t] * gcol_ref[...]
        start_fetch(0, 0)

    @pl.when(n + 1 < n_seq)
    def _():
        start_fetch(n + 1, 1 - slot)

    nq = qlat_ref.shape[1]
    t_q = nq // MLA_HEADS
    n_nope = MLA_HEADS * QK_NOPE
    a_mat = jnp.concatenate([wukt_ref[...], qlat_ref[0]], axis=0)
    n_feat = 3 * QK_ROPE
    ones_rows = jnp.where(lax.broadcasted_iota(jnp.int32, (16, n_feat), 1) >= 2 * QK_ROPE, 1.0, 0.0)
    l_mat = jnp.concatenate([qrf_ref[0][:, :n_feat], ones_rows.astype(BF16)], axis=0)

    def scores(cb, kpt, tg):
        tk = cb.shape[0]
        m1 = _dot_nt(a_mat, cb)
        knt = m1[:n_nope]
        ssq = jnp.sum((knt * knt).reshape(MLA_HEADS, QK_NOPE, tk), axis=1)
        feat = jnp.concatenate([kpt * tg[:QK_ROPE], kpt * tg[QK_ROPE:], kpt * kpt], axis=0)
        m2 = _dot(l_mat, feat.astype(BF16))
        r = lax.rsqrt((ssq + m2[nq:nq + MLA_HEADS]) * (1.0 / QK_HEAD) + EPS)
        s = (m1[n_nope:] + m2[:nq]).reshape(t_q, MLA_HEADS, tk) * r[None]
        return s.reshape(nq, tk)

    def update(carry, s, cb):
        m, l, acc = carry
        m_new = jnp.maximum(m, jnp.max(s, axis=-1, keepdims=True))
        alpha = jnp.exp(m - m_new)
        p = jnp.exp(s - m_new)
        l = alpha * l + jnp.sum(p, axis=-1, keepdims=True)
        acc = alpha * acc + _dot(p.astype(BF16), cb)
        return m_new, l, acc

    def latents(j):
        return cbuf[slot, pl.ds(j * ppt, ppt)].reshape(KV_TILE, KV_LORA).astype(BF16)

    def tile_scores(j):
        kp = pbuf[slot, pl.ds(j * ppt, ppt)]
        kpt = jnp.concatenate([kp[i] for i in range(ppt)], axis=1)
        return scores(latents(j), kpt, tg_s[j])

    wait_fetch(slot)

    def body(j, carry):
        s_cur, stats = carry
        s_next = tile_scores(j + 1)
        return s_next, update(stats, s_cur, latents(j))

    init = (jnp.full((nq, 1), NEG_INF, F32), jnp.zeros((nq, 1), F32), jnp.zeros((nq, KV_LORA), F32))
    s_last, stats = lax.fori_loop(0, n_tiles - 1, body, (tile_scores(0), init))

    cn_s[0:t_q, :] = cnew_ref[0]
    cb_new = cn_s[...].astype(BF16)
    s_new = scores(cb_new, kpnew_ref[0], tnew_ref[...] * gcol_ref[...])
    key = lax.broadcasted_iota(jnp.int32, (nq, LANES), 1)
    qry = lax.broadcasted_iota(jnp.int32, (nq, LANES), 0) // MLA_HEADS
    s_new = jnp.where(key <= qry, s_new, NEG_INF)
    stats = update(stats, s_last, latents(n_tiles - 1))
    m, l, acc = update(stats, s_new, cb_new)
    ctx_ref[0] = acc / l


def _paged(page_table, qlat, qrf, cnew, kpnew_t, cache_ckv, cache_kpe_t, w):
    n_seq, n_pages = page_table.shape
    nq = qlat.shape[1]
    t_q = cnew.shape[1]
    seq_spec = lambda shape: pl.BlockSpec((1,) + shape, lambda n, pt: (n, 0, 0))
    const = lambda shape: pl.BlockSpec(shape, lambda n, pt: (0,) * len(shape), pipeline_mode=pl.Buffered(1))
    grid_spec = pltpu.PrefetchScalarGridSpec(
        num_scalar_prefetch=1, grid=(n_seq,),
        in_specs=[seq_spec((nq, KV_LORA)), seq_spec((nq, LANES)), seq_spec((t_q, KV_LORA)), seq_spec((QK_ROPE, LANES)),
                  const(w['wukt'].shape), const(w['gcol'].shape), const(w['t_past'].shape), const(w['t_new'].shape),
                  pl.BlockSpec(memory_space=pl.ANY), pl.BlockSpec(memory_space=pl.ANY)],
        out_specs=seq_spec((nq, KV_LORA)),
        scratch_shapes=[pltpu.VMEM((2, n_pages, PAGE_SIZE, KV_LORA), F32),
                        pltpu.VMEM((2, n_pages, QK_ROPE, PAGE_SIZE), F32),
                        pltpu.VMEM((LANES, KV_LORA), F32), pltpu.VMEM(w['t_past'].shape, F32),
                        pltpu.SemaphoreType.DMA((2, 2))])
    return pl.pallas_call(
        functools.partial(_paged_kernel, n_seq=n_seq, n_pages=n_pages),
        grid_spec=grid_spec,
        out_shape=jax.ShapeDtypeStruct((n_seq, nq, KV_LORA), F32),
        compiler_params=_params("arbitrary"),
        name="paged_sample",
    )(page_table, qlat, qrf, cnew, kpnew_t, w['wukt'], w['gcol'], w['t_past'], w['t_new'], cache_ckv, cache_kpe_t)


def _memkv_kernel(mem_ref, gin_ref, wmk_ref, wmv_ref, gmk_ref, k_ref, v_ref):
    hm = (_rms(mem_ref[...]) * gin_ref[...]).astype(BF16)
    kk = _dot(hm, wmk_ref[...])
    for h in range(MEM_HEADS):
        sl = slice(h * MEM_HEAD_DIM, (h + 1) * MEM_HEAD_DIM)
        k_ref[:, sl] = _rms(kk[:, sl]) * gmk_ref[...]
    v_ref[...] = _dot(hm, wmv_ref[...])


def _memkv(mem, w):
    m = mem.shape[0]
    tm = TOK_TILE
    row = lambda width: pl.BlockSpec((tm, width), lambda i: (i, 0))
    consts = [w['g_mem_in'], w['w_mk'], w['w_mv'], w['g_mk']]
    return pl.pallas_call(
        _memkv_kernel, grid=(m // tm,),
        in_specs=[row(D_MODEL)] + [_const_spec(c.shape) for c in consts],
        out_specs=[row(MEM_WIDTH), row(MEM_WIDTH)],
        out_shape=[jax.ShapeDtypeStruct((m, MEM_WIDTH), F32)] * 2,
        compiler_params=_params("arbitrary"),
        name="memkv_prompt",
    )(mem, *consts)


def _mem_query(x1, gmx_ref, wmq_ref, gmq_ref):
    qm = _dot((_rms(x1) * gmx_ref[...]).astype(BF16), wmq_ref[...])
    heads = []
    for h in range(MEM_HEADS):
        sl = slice(h * MEM_HEAD_DIM, (h + 1) * MEM_HEAD_DIM)
        heads.append((_rms(qm[:, sl]) * (gmq_ref[...] * MEM_SCALE)).astype(BF16))
    return heads


def _merge_prompt_kernel(x_ref, an_ref, bn_ref, wo_ref, gmx_ref, wmq_ref, gmq_ref, mk_ref, mv_ref, wmo_ref,
                         y_ref, o_s):
    x1 = x_ref[...] + _dot(an_ref[...], wo_ref[0:A_WIDTH, :]) + _dot(bn_ref[...], wo_ref[A_WIDTH:, :])
    heads = _mem_query(x1, gmx_ref, wmq_ref, gmq_ref)
    for h in range(MEM_HEADS):
        sl = slice(h * MEM_HEAD_DIM, (h + 1) * MEM_HEAD_DIM)
        s = _dot_nt(heads[h], mk_ref[0, :, sl].astype(BF16))
        p = jnp.exp(s - jnp.max(s, axis=-1, keepdims=True))
        o = _dot(p.astype(BF16), mv_ref[0, :, sl].astype(BF16)) / jnp.sum(p, axis=-1, keepdims=True)
        o_s[:, sl] = o.astype(BF16)
    y_ref[...] = x1 + _dot(o_s[...], wmo_ref[...])


def _merge_prompt(x, a_n, b_n, mk, mv, w, seq):
    m = x.shape[0]
    tm = TOK_TILE
    per_seq = seq // tm
    mem_len = mk.shape[1]
    row = lambda width: pl.BlockSpec((tm, width), lambda i: (i, 0))
    mem = pl.BlockSpec((1, mem_len, MEM_WIDTH), lambda i: (i // per_seq, 0, 0))
    c = lambda name: _const_spec(w[name].shape)
    return pl.pallas_call(
        _merge_prompt_kernel, grid=(m // tm,),
        in_specs=[row(D_MODEL), row(A_WIDTH), row(B_WIDTH), c('w_o'), c('g_mem_x'), c('w_mq'),
                  c('g_mq'), mem, mem, c('w_mo')],
        out_specs=row(D_MODEL),
        out_shape=jax.ShapeDtypeStruct((m, D_MODEL), F32),
        scratch_shapes=[pltpu.VMEM((tm, MEM_WIDTH), BF16)],
        compiler_params=_params("arbitrary"),
        name="merge_prompt",
    )(x, a_n, b_n, w['w_o'], w['g_mem_x'], w['w_mq'], w['g_mq'], mk, mv, w['w_mo'])


def _merge_sample_kernel(x_ref, an_ref, ctx_ref, wuvb_ref, gob_ref, wo_ref, gmx_ref, wmq_ref, gmq_ref,
                         x1_ref, qm_ref):
    b = _dot(ctx_ref[...].astype(BF16), wuvb_ref[...])
    bn = (_rms(b) * gob_ref[...]).astype(BF16)
    x1 = x_ref[...] + _dot(an_ref[...], wo_ref[0:A_WIDTH, :]) + _dot(bn, wo_ref[A_WIDTH:, :])
    x1_ref[...] = x1
    heads = _mem_query(x1, gmx_ref, wmq_ref, gmq_ref)
    for h in range(MEM_HEADS):
        qm_ref[:, h * MEM_HEAD_DIM:(h + 1) * MEM_HEAD_DIM] = heads[h]


def _merge_sample(x, a_n, ctx, w):
    m = x.shape[0]
    tm = TOK_TILE
    row = lambda width: pl.BlockSpec((tm, width), lambda i: (i, 0))
    c = lambda name: _const_spec(w[name].shape)
    return pl.pallas_call(
        _merge_sample_kernel, grid=(m // tm,),
        in_specs=[row(D_MODEL), row(A_WIDTH), row(ctx.shape[1]), c('w_uv_heads'), c('g_out_b'), c('w_o'),
                  c('g_mem_x'), c('w_mq'), c('g_mq')],
        out_specs=[row(D_MODEL), row(MEM_WIDTH)],
        out_shape=[jax.ShapeDtypeStruct((m, D_MODEL), F32), jax.ShapeDtypeStruct((m, MEM_WIDTH), BF16)],
        compiler_params=_params("arbitrary"),
        name="merge_sample",
    )(x, a_n, ctx, w['w_uv_heads'], w['g_out_b'], w['w_o'], w['g_mem_x'], w['w_mq'], w['g_mq'])


def _memattn_sample_kernel(q_ref, mk_ref, mv_ref, o_ref):
    nrow, nkey = q_ref.shape[1], mk_ref.shape[1]
    s = jnp.einsum('nqd,nkd->nqk', q_ref[...], mk_ref[...].astype(BF16), preferred_element_type=F32)
    same_head = (lax.broadcasted_iota(jnp.int32, (1, nrow, nkey), 1) % MEM_HEADS
                 == lax.broadcasted_iota(jnp.int32, (1, nrow, nkey), 2) % MEM_HEADS)
    s = jnp.where(same_head, s, NEG_INF)
    p = jnp.exp(s - jnp.max(s, axis=-1, keepdims=True))
    o = jnp.einsum('nqk,nkd->nqd', p.astype(BF16), mv_ref[...].astype(BF16), preferred_element_type=F32)
    o_ref[...] = (o / jnp.sum(p, axis=-1, keepdims=True)).astype(BF16)


def _memattn_sample(qm, mk, mv, first_seq):
    n_seq, nrow, _ = qm.shape
    nkey = mk.shape[1]
    ns = MEM_SEQ_TILE
    first_blk = first_seq // ns
    return pl.pallas_call(
        _memattn_sample_kernel, grid=(n_seq // ns,),
        in_specs=[pl.BlockSpec((ns, nrow, MEM_HEAD_DIM), lambda i: (i, 0, 0)),
                  pl.BlockSpec((ns, nkey, MEM_HEAD_DIM), lambda i: (first_blk + i, 0, 0)),
                  pl.BlockSpec((ns, nkey, MEM_HEAD_DIM), lambda i: (first_blk + i, 0, 0))],
        out_specs=pl.BlockSpec((ns, nrow, MEM_HEAD_DIM), lambda i: (i, 0, 0)),
        out_shape=jax.ShapeDtypeStruct((n_seq, nrow, MEM_HEAD_DIM), BF16),
        compiler_params=_params("arbitrary"),
        name="memattn_sample",
    )(qm, mk, mv)


def _ffn_kernel(*refs, sample, tiles_per_seq):
    if sample:
        (x_ref, o_ref, wmo_ref, gffn_ref, wup_ref, wd_ref, wc_ref, bc_ref, p1_ref, p2_ref,
         y_ref, gate_ref, h_s, g_s, act_s) = refs
        x2 = x_ref[...] + _dot(o_ref[...], wmo_ref[...])
        g_s[0:8, :] = jnp.zeros((8, D_FF), F32)
    else:
        (x_ref, gffn_ref, wup_ref, wd_ref, wc_ref, bc_ref, y_ref, tail_ref, h_s, g_s, act_s, carry_s) = refs
        x2 = x_ref[...]

        @pl.when(pl.program_id(0) == 0)
        def _():
            carry_s[...] = jnp.zeros_like(carry_s)
        g_s[0:8, :] = jnp.where((pl.program_id(0) % tiles_per_seq) == 0, 0.0, carry_s[...])
    tm = x_ref.shape[0]
    y_ref[...] = x2
    h_s[...] = (_rms(x2) * gffn_ref[...]).astype(BF16)
    if sample:
        t_in_seq = lax.broadcasted_iota(jnp.int32, (tm, 1), 0) % tiles_per_seq
    for f in range(N_FF_TILES):
        sl = slice(f * FF_TILE, (f + 1) * FF_TILE)
        h = h_s[...]
        g = _dot(h, wup_ref[:, sl])
        val = _dot(h, wup_ref[:, D_FF + f * FF_TILE:D_FF + (f + 1) * FF_TILE])
        g_s[8:, sl] = g
        g1 = g_s[7:7 + tm, sl]
        g2 = g_s[6:6 + tm, sl]
        if sample:
            gate_ref[:, sl] = g
            g1 = jnp.where(t_in_seq >= 1, g1, p1_ref[:, sl])
            g2 = jnp.where(t_in_seq >= 2, g2, p2_ref[:, sl])
        conv = bc_ref[:, sl] + g2 * wc_ref[0:1, sl] + g1 * wc_ref[1:2, sl] + g * wc_ref[2:3, sl]
        act_s[:, sl] = (conv / (1.0 + jnp.exp(-conv)) * val).astype(BF16)
    if not sample:
        tail = g_s[tm:tm + 8, :]
        carry_s[...] = tail
        tail_ref[0] = tail
    y_ref[...] += _dot(act_s[...], wd_ref[...])


def _ffn_scratch(tm):
    return [pltpu.VMEM((tm, D_MODEL), BF16), pltpu.VMEM((tm + 8, D_FF), F32), pltpu.VMEM((tm, D_FF), BF16)]


def _ffn_prompt(x2, w, n_batch, seq):
    m = x2.shape[0]
    tm = TOK_TILE
    per_seq = seq // tm
    row = pl.BlockSpec((tm, D_MODEL), lambda i: (i, 0))
    c = lambda name: _const_spec(w[name].shape)
    return pl.pallas_call(
        functools.partial(_ffn_kernel, sample=False, tiles_per_seq=per_seq), grid=(m // tm,),
        in_specs=[row, c('g_ffn'), c('w_up'), c('w_down'), c('w_conv'), c('b_conv')],
        out_specs=[row, pl.BlockSpec((1, 8, D_FF), lambda i: (i // per_seq, 0, 0))],
        out_shape=[jax.ShapeDtypeStruct((m, D_MODEL), F32), jax.ShapeDtypeStruct((n_batch, 8, D_FF), F32)],
        scratch_shapes=_ffn_scratch(tm) + [pltpu.VMEM((8, D_FF), F32)],
        compiler_params=_params("arbitrary"),
        name="ffn_prompt",
    )(x2, w['g_ffn'], w['w_up'], w['w_down'], w['w_conv'], w['b_conv'])


def _ffn_sample(x1, o, prev1, prev2, w, t_q):
    m = x1.shape[0]
    tm = TOK_TILE
    row = lambda width: pl.BlockSpec((tm, width), lambda i: (i, 0))
    c = lambda name: _const_spec(w[name].shape)
    return pl.pallas_call(
        functools.partial(_ffn_kernel, sample=True, tiles_per_seq=t_q), grid=(m // tm,),
        in_specs=[row(D_MODEL), row(MEM_WIDTH), c('w_mo'), c('g_ffn'), c('w_up'), c('w_down'),
                  c('w_conv'), c('b_conv'), row(D_FF), row(D_FF)],
        out_specs=[row(D_MODEL), row(D_FF)],
        out_shape=[jax.ShapeDtypeStruct((m, D_MODEL), F32), jax.ShapeDtypeStruct((m, D_FF), F32)],
        scratch_shapes=_ffn_scratch(tm),
        compiler_params=_params("arbitrary"),
        name="ffn_sample",
    )(x1, o, w['w_mo'], w['g_ffn'], w['w_up'], w['w_down'], w['w_conv'], w['b_conv'], prev1, prev2)


def _head_pad(wm, width):
    k = wm.shape[0]
    wm = wm.reshape(k, MLA_HEADS, width)
    return jnp.pad(wm, ((0, 0), (0, 0), (0, HEAD_PAD - width))).reshape(k, MLA_HEADS * HEAD_PAD)


def _rot_half_cols(wm):
    half = QK_ROPE // 2
    return jnp.concatenate([-wm[..., half:], wm[..., :half]], axis=-1)


def _lane_vec(nope, rope):
    return jnp.concatenate([nope, rope, jnp.zeros((LANES - QK_HEAD,), F32)])[None, :]


def _rope_tables(pos):
    half = QK_ROPE // 2
    inv_freq = ROPE_THETA ** (-jnp.arange(half, dtype=F32) / half)
    ang = pos.astype(F32)[:, None] * inv_freq[None, :]
    return jnp.cos(ang), jnp.sin(ang)


def _head_tile_tables(pos):
    cos, sin = _rope_tables(pos)
    n = pos.shape[0]
    cos_t = jnp.concatenate([jnp.ones((n, QK_NOPE), F32), cos, cos, jnp.zeros((n, LANES - QK_HEAD), F32)], axis=1)
    sin_t = jnp.concatenate([jnp.zeros((n, QK_NOPE), F32), sin, sin, jnp.zeros((n, LANES - QK_HEAD), F32)], axis=1)
    return cos_t, sin_t


def _feature_table(pos, cols):
    cos, sin = _rope_tables(pos)
    t = jnp.concatenate([cos, cos, sin, sin], axis=1).T
    return jnp.pad(t, ((0, 0), (0, cols - pos.shape[0])))


def _query_feature_perm():
    half = QK_ROPE // 2
    p = np.zeros((LANES, LANES), np.float32)
    for j in range(QK_ROPE):
        p[QK_NOPE + j, j] = 1.0
    for j in range(half):
        p[QK_NOPE + half + j, QK_ROPE + j] = 1.0
        p[QK_NOPE + j, QK_ROPE + half + j] = -1.0
    return jnp.asarray(p, BF16)


def _prep_weights(l, t_q, g_mix, w_in, ln_v_g, ln_v_b, w_s, b_s, g_q_a, w_uq, g_kv_a, w_uk, w_uv, g_qk_q, g_qk_k,
                  g_out_a, g_out_b, w_o, g_mem_x, g_mem_in, w_mq, w_mk, w_mv, g_mq, g_mk, w_mo, g_ffn, w_up,
                  w_conv, b_conv, w_down):
    half = QK_ROPE // 2
    o_kpe = 2 * A_WIDTH + Q_LORA + KV_LORA
    vec = lambda a: a.reshape(1, -1).astype(F32)
    w = {}
    wkpe = w_in[l][:, o_kpe:]
    lane_tile = lambda blk: jnp.pad(blk, ((0, 0), (QK_NOPE, LANES - QK_HEAD)))
    w['w_in'] = jnp.concatenate([w_in[l][:, :o_kpe], lane_tile(wkpe), lane_tile(_rot_half_cols(wkpe))],
                                axis=1).astype(BF16)
    w['g_mix'] = vec(g_mix[l])
    w['ln_g'] = vec(ln_v_g[l])
    w['ln_b'] = vec(ln_v_b[l])
    w['g_q_a'] = vec(g_q_a[l])
    w['g_kv_a'] = vec(g_kv_a[l])
    w['g_out_a'] = vec(g_out_a[l])
    w['g_out_b'] = vec(g_out_b[l])
    w['g_out_b_col'] = g_out_b[l][:, None]
    wq = w_uq[l].reshape(Q_LORA, MLA_HEADS, QK_HEAD)
    wq_rot = jnp.concatenate([jnp.zeros((Q_LORA, MLA_HEADS, QK_NOPE), F32), _rot_half_cols(wq[..., QK_NOPE:])], axis=-1)
    w['wq'] = _head_pad(wq.reshape(Q_LORA, -1), QK_HEAD).astype(BF16)
    w['wq_rot'] = _head_pad(wq_rot.reshape(Q_LORA, -1), QK_HEAD).astype(BF16)
    w['wk'] = _head_pad(w_uk[l], QK_NOPE).astype(BF16)
    w['wvt'] = w_uv[l].T.astype(BF16)
    gq, gk = g_qk_q[l], g_qk_k[l]
    sw = lambda g: jnp.concatenate([g[QK_NOPE + half:], g[QK_NOPE:QK_NOPE + half]])
    zeros_n = jnp.zeros((QK_NOPE,), F32)
    w['gq'] = _lane_vec(gq[:QK_NOPE], gq[QK_NOPE:])
    w['gq_sw'] = _lane_vec(zeros_n, sw(gq))
    w['gk'] = _lane_vec(gk[:QK_NOPE], gk[QK_NOPE:])
    w['gk_sw'] = _lane_vec(zeros_n, sw(gk))
    w['gk_nope'] = _lane_vec(gk[:QK_NOPE], jnp.zeros((QK_ROPE,), F32))
    w['pq'] = _query_feature_perm()
    w['wukt'] = w_uk[l].T.astype(BF16)
    gr = gk[QK_NOPE:]
    w['gcol'] = jnp.concatenate([gr, gr])[:, None]
    wuv = w_uv[l].reshape(KV_LORA, MLA_HEADS, V_HEAD)
    eye = jnp.eye(MLA_HEADS, dtype=F32)
    w['w_uv_heads'] = (wuv[None, :, :, :] * eye[:, None, :, None]).reshape(MLA_HEADS * KV_LORA, B_WIDTH).astype(BF16)
    tril = jnp.tril(jnp.ones((CHUNK, CHUNK), F32))
    w['ws_prompt'] = (w_s[l] * tril).astype(BF16)
    w['bs_prompt'] = jnp.repeat(b_s[l].T, A_GROUP_DIM, axis=1)
    reps = CHUNK // t_q
    blk = w_s[l][:, :t_q, :t_q] * jnp.tril(jnp.ones((t_q, t_q), F32))
    w['ws_sample'] = jnp.einsum('ij,gts->gitjs', jnp.eye(reps, dtype=F32), blk).reshape(A_GROUPS, CHUNK, CHUNK).astype(BF16)
    w['bs_sample'] = jnp.repeat(jnp.tile(b_s[l][:, :t_q].T, (reps, 1)), A_GROUP_DIM, axis=1)
    w['w_o'] = w_o[l].astype(BF16)
    w['g_mem_x'] = vec(g_mem_x[l])
    w['g_mem_in'] = vec(g_mem_in[l])
    w['w_mq'] = w_mq[l].astype(BF16)
    w['w_mk'] = w_mk[l].astype(BF16)
    w['w_mv'] = w_mv[l].astype(BF16)
    w['g_mq'] = vec(g_mq[l])
    w['g_mk'] = vec(g_mk[l])
    w['w_mo'] = w_mo[l].astype(BF16)
    w['g_ffn'] = vec(g_ffn[l])
    w['w_up'] = w_up[l].astype(BF16)
    w['w_down'] = w_down[l].astype(BF16)
    w['w_conv'] = w_conv[l]
    w['b_conv'] = vec(b_conv[l])
    return w


def kernel(x_prompt, x_sample, cache_ckv, cache_kpe, cache_mem_k, cache_mem_v, state_ffn_conv, page_table, mem_prompt, g_mix, w_in, ln_v_g, ln_v_b, w_s, b_s, g_q_a, w_uq, g_kv_a, w_uk, w_uv, g_qk_q, g_qk_k, g_out_a, g_out_b, w_o, g_mem_x, g_mem_in, w_mq, w_mk, w_mv, g_mq, g_mk, w_mo, g_ffn, w_up, w_conv, b_conv, w_down):
    n_p, seq, _ = x_prompt.shape
    n_s, t_q, _ = x_sample.shape
    depth = g_mix.shape[0]
    mem_len = mem_prompt.shape[1]
    n_pages = page_table.shape[1]
    past = n_pages * PAGE_SIZE

    pos_p = jnp.arange(seq)
    pos_s = past + jnp.arange(t_q)
    cos_p, sin_p = _head_tile_tables(pos_p)
    cos_s, sin_s = _head_tile_tables(jnp.tile(pos_s, TOK_TILE // t_q))
    n_kv_tiles = past // KV_TILE
    t_past = _feature_table(jnp.arange(past), past).reshape(2 * QK_ROPE, n_kv_tiles, KV_TILE).transpose(1, 0, 2)
    t_new = _feature_table(pos_s, LANES)
    n_phys = cache_ckv.shape[1]
    ckv_pages = cache_ckv.reshape(depth * n_phys, PAGE_SIZE, KV_LORA)
    kpe_pages_t = jnp.swapaxes(cache_kpe, -1, -2).reshape(depth * n_phys, QK_ROPE, PAGE_SIZE)
    mem_rows = mem_len * MEM_HEADS
    mem_k_rows = cache_mem_k.reshape(depth * n_s, mem_rows, MEM_HEAD_DIM)
    mem_v_rows = cache_mem_v.reshape(depth * n_s, mem_rows, MEM_HEAD_DIM)

    xp = x_prompt.reshape(n_p * seq, D_MODEL)
    xs = x_sample.reshape(n_s * t_q, D_MODEL)
    outs = {k: [] for k in ('p_ckv', 'p_kpe', 'p_mk', 'p_mv', 'p_conv', 's_ckv', 's_kpe', 's_chunk_v', 's_conv')}
    for l in range(depth):
        w = _prep_weights(l, t_q, g_mix, w_in, ln_v_g, ln_v_b, w_s, b_s, g_q_a, w_uq, g_kv_a, w_uk, w_uv, g_qk_q,
                          g_qk_k, g_out_a, g_out_b, w_o, g_mem_x, g_mem_in, w_mq, w_mk, w_mv, g_mq, g_mk, w_mo,
                          g_ffn, w_up, w_conv, b_conv, w_down)
        w['t_past'], w['t_new'] = t_past, t_new

        wp = dict(w, ws=w['ws_prompt'], bs=w['bs_prompt'])
        q, k, vt, ckvn, kpe, a_n = _inproj(xp, cos_p, sin_p, wp, sample=False)
        b_n = _flash(q, k, vt, w['g_out_b_col'], n_p, seq)
        mk, mv = _memkv(mem_prompt.reshape(n_p * mem_len, D_MODEL), w)
        mk3 = mk.reshape(n_p, mem_len, MEM_WIDTH)
        mv3 = mv.reshape(n_p, mem_len, MEM_WIDTH)
        x2 = _merge_prompt(xp, a_n, b_n, mk3, mv3, w, seq)
        xp, tail = _ffn_prompt(x2, w, n_p, seq)
        outs['p_ckv'].append(ckvn.reshape(n_p, seq, KV_LORA))
        outs['p_kpe'].append(kpe.reshape(n_p, seq, QK_ROPE))
        outs['p_mk'].append(mk.reshape(n_p, mem_len, MEM_HEADS, MEM_HEAD_DIM))
        outs['p_mv'].append(mv.reshape(n_p, mem_len, MEM_HEADS, MEM_HEAD_DIM))
        outs['p_conv'].append(tail[:, 8 - (CONV_W - 1):, :])

        wsm = dict(w, ws=w['ws_sample'], bs=w['bs_sample'])
        qlat, qrf, ckvn_s, kpe_s, a_ns, vg = _inproj(xs, cos_s, sin_s, wsm, sample=True)
        nq = t_q * MLA_HEADS
        kpnew_t = jnp.pad(jnp.swapaxes(kpe_s.reshape(n_s, t_q, QK_ROPE), 1, 2), ((0, 0), (0, 0), (0, LANES - t_q)))
        ctx = _paged(page_table + l * n_phys, qlat.reshape(n_s, nq, KV_LORA), qrf.reshape(n_s, nq, LANES),
                     ckvn_s.reshape(n_s, t_q, KV_LORA), kpnew_t, ckv_pages, kpe_pages_t, w)
        x1, qm = _merge_sample(xs, a_ns, ctx.reshape(n_s * t_q, MLA_HEADS * KV_LORA), w)
        o = _memattn_sample(qm.reshape(n_s, t_q * MEM_HEADS, MEM_HEAD_DIM), mem_k_rows, mem_v_rows, l * n_s)
        st = state_ffn_conv[l]
        zero = jnp.zeros_like(st[:, :1])
        prev1 = jnp.concatenate([st[:, 1:2]] + [zero] * (t_q - 1), axis=1)
        prev2 = jnp.concatenate([st[:, 0:1], st[:, 1:2]] + [zero] * (t_q - 2), axis=1)
        flat = lambda a: a.reshape(n_s * t_q, D_FF)
        xs, gate = _ffn_sample(x1, o.reshape(n_s * t_q, MEM_WIDTH), flat(prev1), flat(prev2), w, t_q)
        gate = gate.reshape(n_s, t_q, D_FF)
        outs['s_ckv'].append(ckvn_s.reshape(n_s, t_q, KV_LORA))
        outs['s_kpe'].append(kpe_s.reshape(n_s, t_q, QK_ROPE))
        outs['s_chunk_v'].append(vg.reshape(n_s, t_q, A_GROUPS, A_GROUP_DIM))
        outs['s_conv'].append(gate[:, t_q - (CONV_W - 1):, :])

    return (xp.reshape(n_p, seq, D_MODEL), xs.reshape(n_s, t_q, D_MODEL),
            jnp.stack(outs['p_ckv']), jnp.stack(outs['p_kpe']), jnp.stack(outs['p_mk']), jnp.stack(outs['p_mv']),
            jnp.stack(outs['p_conv']), jnp.stack(outs['s_ckv']), jnp.stack(outs['s_kpe']),
            jnp.stack(outs['s_chunk_v']), jnp.stack(outs['s_conv']))
```

```python
import functools

import numpy as np
import jax
import jax.numpy as jnp
from jax import lax
from jax.experimental import pallas as pl
from jax.experimental.pallas import tpu as pltpu

D_MODEL = 1024
A_GROUPS = 8
A_GROUP_DIM = 64
A_WIDTH = A_GROUPS * A_GROUP_DIM
CHUNK = 128
MLA_HEADS = 8
Q_LORA = 256
KV_LORA = 256
QK_NOPE = 64
QK_ROPE = 32
QK_HEAD = QK_NOPE + QK_ROPE
V_HEAD = 64
B_WIDTH = MLA_HEADS * V_HEAD
ROPE_THETA = 10000.0
PAGE_SIZE = 128
MEM_HEADS = 4
MEM_HEAD_DIM = 128
MEM_WIDTH = MEM_HEADS * MEM_HEAD_DIM
D_FF = 2816
CONV_W = 3
EPS = 1e-6
NEG_INF = -1e30

LANES = 128
HEAD_PAD = LANES
QK_SCALE = QK_HEAD ** -0.5
LOG2E = 1.4426950408889634
V_AUG = V_HEAD + 16
MEM_SCALE = MEM_HEAD_DIM ** -0.5

TOK_TILE = 512
FF_TILE = 256
N_FF_TILES = D_FF // FF_TILE
ATT_TILE = 256
KV_TILE = 1024
KV_SUB = 256
MEM_SEQ_TILE = 8
VMEM_LIMIT = 56 * 1024 * 1024

F32 = jnp.float32
BF16 = jnp.bfloat16


def _dot(a, b):
    return jnp.dot(a, b, preferred_element_type=F32)


def _dot_nt(a, b):
    return lax.dot_general(a, b, (((1,), (1,)), ((), ())), preferred_element_type=F32)


def _rms(x):
    return x * lax.rsqrt(jnp.mean(x * x, axis=-1, keepdims=True) + EPS)


def _gelu(x):
    return jax.nn.gelu(x, approximate=True)


def _const_spec(shape):
    nd = len(shape)
    return pl.BlockSpec(shape, lambda *_: (0,) * nd, pipeline_mode=pl.Buffered(1))


def _params(*sem):
    return pltpu.CompilerParams(dimension_semantics=sem, vmem_limit_bytes=VMEM_LIMIT)


def _inproj_kernel(x_ref, gmix_ref, win_ref, lng_ref, lnb_ref, ws_ref, bs_ref, gqa_ref, wq_ref, wqr_ref,
                   gkva_ref, wk_ref, wv_ref, gq_ref, gqs_ref, gk_ref, gks_ref, cos_ref, sin_ref, goa_ref,
                   pq_ref, *rest, sample):
    if sample:
        qlat_ref, qrf_ref, ckvn_ref, kpe_ref, an_ref, vg_ref, zu_s, vgb_s, a_s = rest
    else:
        q_ref, k_ref, v_ref, ckvn_ref, kpe_ref, an_ref, zu_s, vgb_s, a_s = rest
    tm = x_ref.shape[0]
    xn = (_rms(x_ref[...]) * gmix_ref[...]).astype(BF16)

    zu_s[...] = _dot(xn, win_ref[:, 0:A_WIDTH])
    gv = _gelu(_dot(xn, win_ref[:, A_WIDTH:2 * A_WIDTH]))
    low = lax.broadcasted_iota(jnp.int32, (1, LANES), 1) < A_GROUP_DIM
    inv_g = 1.0 / A_GROUP_DIM
    for p in range(A_WIDTH // LANES):
        sl = slice(p * LANES, (p + 1) * LANES)
        t = gv[:, sl]
        s_lo = jnp.sum(jnp.where(low, t, 0.0), axis=-1, keepdims=True)
        s_hi = jnp.sum(jnp.where(low, 0.0, t), axis=-1, keepdims=True)
        tc = t - jnp.where(low, s_lo, s_hi) * inv_g
        tc2 = tc * tc
        v_lo = jnp.sum(jnp.where(low, tc2, 0.0), axis=-1, keepdims=True)
        v_hi = jnp.sum(jnp.where(low, 0.0, tc2), axis=-1, keepdims=True)
        y = tc * lax.rsqrt(jnp.where(low, v_lo, v_hi) * inv_g + EPS) * lng_ref[:, sl] + lnb_ref[:, sl]
        if sample:
            vg_ref[:, sl] = y
        vgb_s[:, sl] = y.astype(BF16)
    for c in range(tm // CHUNK):
        rows = slice(c * CHUNK, (c + 1) * CHUNK)
        for p in range(A_WIDTH // LANES):
            sl = slice(p * LANES, (p + 1) * LANES)
            vp = vgb_s[rows, sl]
            mixed = jnp.where(low, _dot(ws_ref[2 * p], vp), _dot(ws_ref[2 * p + 1], vp)) + bs_ref[:, sl]
            a_s[rows, sl] = _gelu(zu_s[rows, sl]) * mixed
    an_ref[...] = (_rms(a_s[...]) * goa_ref[...]).astype(BF16)

    o_cq = 2 * A_WIDTH
    o_ckv = o_cq + Q_LORA
    o_kpe = o_ckv + KV_LORA
    cqn = (_rms(_dot(xn, win_ref[:, o_cq:o_ckv])) * gqa_ref[...]).astype(BF16)
    ckvn = _rms(_dot(xn, win_ref[:, o_ckv:o_kpe])) * gkva_ref[...]
    ckvn_ref[...] = ckvn
    kpt = _dot(xn, win_ref[:, o_kpe:o_kpe + LANES])
    kpe_ref[...] = kpt[:, QK_NOPE:QK_HEAD]
    cosv = cos_ref[...]
    sinv = sin_ref[...]
    q_scale = QK_SCALE if sample else QK_SCALE * LOG2E
    qa = gq_ref[...] * cosv * q_scale
    qb = gqs_ref[...] * sinv * q_scale
    qall = _dot(cqn, wq_ref[...])
    qrall = _dot(cqn, wqr_ref[...])
    inv_d = 1.0 / QK_HEAD
    if not sample:
        cb = ckvn.astype(BF16)
        kpr = _dot(xn, win_ref[:, o_kpe + LANES:o_kpe + 2 * LANES])
        ka = gk_ref[...] * cosv
        kb = gks_ref[...] * sinv
        kall = _dot(cb, wk_ref[...])
        vt = _dot_nt(wv_ref[...], cb).astype(BF16)
        ones = jnp.ones((V_AUG - V_HEAD, ATT_TILE), BF16)
        for c in range(tm // ATT_TILE):
            cols = slice(c * ATT_TILE, (c + 1) * ATT_TILE)
            for h in range(MLA_HEADS):
                v_ref[c, h * V_AUG:h * V_AUG + V_HEAD, :] = vt[h * V_HEAD:(h + 1) * V_HEAD, cols]
                v_ref[c, h * V_AUG + V_HEAD:(h + 1) * V_AUG, :] = ones
    for h in range(MLA_HEADS):
        sl = slice(h * HEAD_PAD, (h + 1) * HEAD_PAD)
        qh = qall[:, sl]
        r = lax.rsqrt(jnp.sum(qh * qh, axis=-1, keepdims=True) * inv_d + EPS)
        qf = r * (qh * qa + qrall[:, sl] * qb)
        if sample:
            qg = (qf * gk_ref[...]).astype(BF16)
            qlat_ref[:, h * KV_LORA:(h + 1) * KV_LORA] = _dot_nt(qg, wk_ref[:, sl]).astype(BF16)
            qrf_ref[:, sl] = _dot(qf.astype(BF16), pq_ref[...]).astype(BF16)
        else:
            q_ref[:, sl] = qf.astype(BF16)
            kh = kall[:, sl] + kpt
            rk = lax.rsqrt(jnp.sum(kh * kh, axis=-1, keepdims=True) * inv_d + EPS)
            k_ref[:, sl] = (rk * (kh * ka + kpr * kb)).astype(BF16)


def _inproj(x, cos, sin, w, *, sample):
    m = x.shape[0]
    tm = TOK_TILE
    n_tiles = m // tm
    n_pos_tiles = cos.shape[0] // tm
    row = lambda width: pl.BlockSpec((tm, width), lambda i: (i, 0))
    pos = pl.BlockSpec((tm, LANES), lambda i: (i % n_pos_tiles, 0))
    consts = [w['g_mix'], w['w_in'], w['ln_g'], w['ln_b'], w['ws'], w['bs'], w['g_q_a'], w['wq'], w['wq_rot'],
              w['g_kv_a'], w['wk'], w['wvt'], w['gq'], w['gq_sw'], w['gk_nope'] if sample else w['gk'], w['gk_sw']]
    in_specs = ([row(D_MODEL)] + [_const_spec(c.shape) for c in consts] + [pos, pos]
                + [_const_spec(w['g_out_a'].shape), _const_spec(w['pq'].shape)])
    args = [x] + consts + [cos, sin, w['g_out_a'], w['pq']]
    wide = MLA_HEADS * HEAD_PAD
    if sample:
        out_shape = [jax.ShapeDtypeStruct((m, MLA_HEADS * KV_LORA), BF16), jax.ShapeDtypeStruct((m, wide), BF16),
                     jax.ShapeDtypeStruct((m, KV_LORA), F32), jax.ShapeDtypeStruct((m, QK_ROPE), F32),
                     jax.ShapeDtypeStruct((m, A_WIDTH), BF16), jax.ShapeDtypeStruct((m, A_WIDTH), F32)]
        out_specs = [row(MLA_HEADS * KV_LORA), row(wide), row(KV_LORA), row(QK_ROPE), row(A_WIDTH), row(A_WIDTH)]
    else:
        att_tiles = tm // ATT_TILE
        out_shape = [jax.ShapeDtypeStruct((m, wide), BF16)] * 2 + [
            jax.ShapeDtypeStruct((m // ATT_TILE, MLA_HEADS * V_AUG, ATT_TILE), BF16),
            jax.ShapeDtypeStruct((m, KV_LORA), F32), jax.ShapeDtypeStruct((m, QK_ROPE), F32),
            jax.ShapeDtypeStruct((m, A_WIDTH), BF16)]
        out_specs = [row(wide)] * 2 + [pl.BlockSpec((att_tiles, MLA_HEADS * V_AUG, ATT_TILE), lambda i: (i, 0, 0)),
                                       row(KV_LORA), row(QK_ROPE), row(A_WIDTH)]
    return pl.pallas_call(
        functools.partial(_inproj_kernel, sample=sample),
        grid=(n_tiles,), in_specs=in_specs, out_specs=out_specs, out_shape=out_shape,
        scratch_shapes=[pltpu.VMEM((tm, A_WIDTH), F32), pltpu.VMEM((tm, A_WIDTH), BF16),
                        pltpu.VMEM((tm, A_WIDTH), F32)],
        compiler_params=_params("arbitrary"),
        name="inproj_sample" if sample else "inproj_prompt",
    )(*args)


def _flash_kernel(q_ref, k_ref, vt_ref, gob_ref, o_ref):
    i = pl.program_id(1)
    t = ATT_TILE
    causal = lax.broadcasted_iota(jnp.int32, (t, t), 0) <= lax.broadcasted_iota(jnp.int32, (t, t), 1)

    def step(j, carry, masked):
        start = pl.multiple_of(j * t, t)
        vt = vt_ref[j]
        heads = [slice(h * HEAD_PAD, (h + 1) * HEAD_PAD) for h in range(MLA_HEADS)]
        scores = [_dot_nt(k_ref[pl.ds(start, t), sl], q_ref[:, sl]) for sl in heads]
        stats = []
        for h in range(MLA_HEADS):
            m = carry[h][0]
            s = jnp.where(causal, scores[h], NEG_INF) if masked else scores[h]
            m_new = jnp.maximum(m, jnp.max(s, axis=0, keepdims=True))
            stats.append((m_new, jnp.exp2(m - m_new), jnp.exp2(s - m_new).astype(BF16)))
        new = []
        for h, (m_new, alpha, p) in enumerate(stats):
            pv = _dot(vt[h * V_AUG:(h + 1) * V_AUG, :], p)
            _, l, acc = carry[h]
            new.append((m_new, alpha * l + pv[V_HEAD:V_HEAD + 1], alpha * acc + pv[:V_HEAD]))
        return tuple(new)

    init = tuple((jnp.full((1, t), NEG_INF, F32), jnp.zeros((1, t), F32), jnp.zeros((V_HEAD, t), F32))
                 for _ in range(MLA_HEADS))
    carry = lax.fori_loop(0, i, functools.partial(step, masked=False), init)
    carry = step(i, carry, True)
    o_t = jnp.concatenate([acc / l for _, l, acc in carry], axis=0)
    bn_t = o_t * lax.rsqrt(jnp.mean(o_t * o_t, axis=0, keepdims=True) + EPS) * gob_ref[...]
    o_ref[...] = bn_t.T.astype(BF16)


def _flash(q, k, vt, g_out_b_col, n_batch, seq):
    m = q.shape[0]
    nq = seq // ATT_TILE
    wide = MLA_HEADS * HEAD_PAD
    return pl.pallas_call(
        _flash_kernel,
        grid=(n_batch, nq),
        in_specs=[pl.BlockSpec((ATT_TILE, wide), lambda b, i: (b * nq + i, 0)),
                  pl.BlockSpec((seq, wide), lambda b, i: (b, 0)),
                  pl.BlockSpec((nq, MLA_HEADS * V_AUG, ATT_TILE), lambda b, i: (b, 0, 0)),
                  pl.BlockSpec((B_WIDTH, 1), lambda b, i: (0, 0))],
        out_specs=pl.BlockSpec((ATT_TILE, B_WIDTH), lambda b, i: (b * nq + i, 0)),
        out_shape=jax.ShapeDtypeStruct((m, B_WIDTH), BF16),
        compiler_params=_params("arbitrary", "arbitrary"),
        name="flash_prompt",
    )(q, k, vt, g_out_b_col)


def _paged_kernel(pt_ref, qlat_ref, qrf_ref, qlat_next_ref, qrf_next_ref, cnew_ref, kpnew_ref, wukt_ref, gcol_ref,
                  tpast_ref, tnew_ref, ckv_hbm, kpe_hbm, ctx_ref, cbuf, pbuf, cn_s, tg_s, s0_s, sems, *, n_seq, n_pages):
    n = pl.program_id(0)
    slot = n % 2
    ppt = KV_TILE // PAGE_SIZE
    n_tiles = n_pages // ppt

    def start_fetch(seq, sl):
        def body(p, _):
            page = pt_ref[seq, p]
            pltpu.make_async_copy(ckv_hbm.at[page], cbuf.at[sl, p], sems.at[0, sl]).start()
            pltpu.make_async_copy(kpe_hbm.at[page], pbuf.at[sl, p], sems.at[1, sl]).start()
            return 0
        lax.fori_loop(0, n_pages, body, 0, unroll=8)

    def wait_fetch(sl):
        pltpu.make_async_copy(ckv_hbm.at[pl.ds(0, n_pages)], cbuf.at[sl], sems.at[0, sl]).wait()
        pltpu.make_async_copy(kpe_hbm.at[pl.ds(0, n_pages)], pbuf.at[sl], sems.at[1, sl]).wait()

    nq = qlat_ref.shape[1]
    t_q = nq // MLA_HEADS
    n_nope = MLA_HEADS * QK_NOPE
    n_feat = 3 * QK_ROPE
    ones_rows = jnp.where(lax.broadcasted_iota(jnp.int32, (16, n_feat), 1) >= 2 * QK_ROPE, 1.0, 0.0).astype(BF16)

    def query_mats(ql_ref, qr_ref):
        a_mat = jnp.concatenate([wukt_ref[...], ql_ref[0]], axis=0)
        l_mat = jnp.concatenate([qr_ref[0][:, :n_feat], ones_rows], axis=0)
        return a_mat, l_mat

    def scores(mats, cb, kpt, tg):
        a_mat, l_mat = mats
        tk = cb.shape[0]
        m1 = _dot_nt(a_mat, cb)
        knt = m1[:n_nope]
        ssq = jnp.sum((knt * knt).reshape(MLA_HEADS, QK_NOPE, tk), axis=1)
        feat = jnp.concatenate([kpt * tg[:QK_ROPE], kpt * tg[QK_ROPE:], kpt * kpt], axis=0)
        m2 = _dot(l_mat, feat.astype(BF16))
        r = lax.rsqrt((ssq + m2[nq:nq + MLA_HEADS]) * (1.0 / QK_HEAD) + EPS)
        s = (m1[n_nope:] + m2[:nq]).reshape(t_q, MLA_HEADS, tk) * r[None]
        return s.reshape(nq, tk)

    def update(carry, s, cb):
        m, l, acc = carry
        m_new = jnp.maximum(m, jnp.max(s, axis=-1, keepdims=True))
        alpha = jnp.exp(m - m_new)
        p = jnp.exp(s - m_new)
        l = alpha * l + jnp.sum(p, axis=-1, keepdims=True)
        acc = alpha * acc + _dot(p.astype(BF16), cb)
        return m_new, l, acc

    pps = KV_SUB // PAGE_SIZE

    def tile_scores(mats, sl, t):
        parts, cbs = [], []
        for sub in range(KV_TILE // KV_SUB):
            p0 = t * ppt + sub * pps
            cb = cbuf[sl, p0:p0 + pps].reshape(KV_SUB, KV_LORA).astype(BF16)
            kpt = jnp.concatenate([pbuf[sl, p0 + i] for i in range(pps)], axis=1)
            parts.append(scores(mats, cb, kpt, tg_s[t, :, sub * KV_SUB:(sub + 1) * KV_SUB]))
            cbs.append(cb)
        return jnp.concatenate(parts, axis=1), jnp.concatenate(cbs, axis=0)

    @pl.when(n == 0)
    def _():
        cn_s[...] = jnp.zeros_like(cn_s)
        for t in range(n_tiles):
            tg_s[t] = tpast_ref[t] * gcol_ref[...]
        start_fetch(0, 0)

    has_next = n + 1 < n_seq

    @pl.when(has_next)
    def _():
        start_fetch(n + 1, 1 - slot)

    @pl.when(n == 0)
    def _():
        wait_fetch(0)
        s0_s[...] = tile_scores(query_mats(qlat_ref, qrf_ref), 0, 0)[0]

    cur = query_mats(qlat_ref, qrf_ref)
    nslot = jnp.where(has_next, 1 - slot, slot)
    s_cur = s0_s[...]
    cb_cur = cbuf[slot, 0:ppt].reshape(KV_TILE, KV_LORA).astype(BF16)
    stats = (jnp.full((nq, 1), NEG_INF, F32), jnp.zeros((nq, 1), F32), jnp.zeros((nq, KV_LORA), F32))
    for t in range(n_tiles):
        if t + 1 < n_tiles:
            s_next, cb_next = tile_scores(cur, slot, t + 1)
        else:
            @pl.when(has_next)
            def _():
                wait_fetch(1 - slot)
            s_next, cb_next = tile_scores(query_mats(qlat_next_ref, qrf_next_ref), nslot, 0)
        stats = update(stats, s_cur, cb_cur)
        s_cur, cb_cur = s_next, cb_next
    s0_s[...] = s_cur

    cn_s[0:t_q, :] = cnew_ref[0]
    cb_new = cn_s[...].astype(BF16)
    s_new = scores(cur, cb_new, kpnew_ref[0], tnew_ref[...] * gcol_ref[...])
    key = lax.broadcasted_iota(jnp.int32, (nq, LANES), 1)
    qry = lax.broadcasted_iota(jnp.int32, (nq, LANES), 0) // MLA_HEADS
    m, l, acc = update(stats, jnp.where(key <= qry, s_new, NEG_INF), cb_new)
    ctx_ref[0] = acc / l


def _paged(page_table, qlat, qrf, cnew, kpnew_t, cache_ckv, cache_kpe_t, w):
    n_seq, n_pages = page_table.shape
    nq = qlat.shape[1]
    t_q = cnew.shape[1]
    seq_spec = lambda shape: pl.BlockSpec((1,) + shape, lambda n, pt: (n, 0, 0))
    next_spec = lambda shape: pl.BlockSpec((1,) + shape, lambda n, pt: (jnp.minimum(n + 1, n_seq - 1), 0, 0))
    const = lambda shape: pl.BlockSpec(shape, lambda n, pt: (0,) * len(shape), pipeline_mode=pl.Buffered(1))
    grid_spec = pltpu.PrefetchScalarGridSpec(
        num_scalar_prefetch=1, grid=(n_seq,),
        in_specs=[seq_spec((nq, KV_LORA)), seq_spec((nq, LANES)), next_spec((nq, KV_LORA)), next_spec((nq, LANES)),
                  seq_spec((t_q, KV_LORA)), seq_spec((QK_ROPE, LANES)),
                  const(w['wukt'].shape), const(w['gcol'].shape), const(w['t_past'].shape), const(w['t_new'].shape),
                  pl.BlockSpec(memory_space=pl.ANY), pl.BlockSpec(memory_space=pl.ANY)],
        out_specs=seq_spec((nq, KV_LORA)),
        scratch_shapes=[pltpu.VMEM((2, n_pages, PAGE_SIZE, KV_LORA), F32),
                        pltpu.VMEM((2, n_pages, QK_ROPE, PAGE_SIZE), F32),
                        pltpu.VMEM((LANES, KV_LORA), F32), pltpu.VMEM(w['t_past'].shape, F32),
                        pltpu.VMEM((nq, KV_TILE), F32), pltpu.SemaphoreType.DMA((2, 2))])
    return pl.pallas_call(
        functools.partial(_paged_kernel, n_seq=n_seq, n_pages=n_pages),
        grid_spec=grid_spec,
        out_shape=jax.ShapeDtypeStruct((n_seq, nq, KV_LORA), F32),
        compiler_params=_params("arbitrary"),
        name="paged_sample",
    )(page_table, qlat, qrf, qlat, qrf, cnew, kpnew_t, w['wukt'], w['gcol'], w['t_past'], w['t_new'],
      cache_ckv, cache_kpe_t)


def _memkv_kernel(mem_ref, gin_ref, wmk_ref, wmv_ref, gmk_ref, k_ref, v_ref):
    hm = (_rms(mem_ref[...]) * gin_ref[...]).astype(BF16)
    kk = _dot(hm, wmk_ref[...])
    for h in range(MEM_HEADS):
        sl = slice(h * MEM_HEAD_DIM, (h + 1) * MEM_HEAD_DIM)
        k_ref[:, sl] = _rms(kk[:, sl]) * gmk_ref[...]
    v_ref[...] = _dot(hm, wmv_ref[...])


def _memkv(mem, w):
    m = mem.shape[0]
    tm = TOK_TILE
    row = lambda width: pl.BlockSpec((tm, width), lambda i: (i, 0))
    consts = [w['g_mem_in'], w['w_mk'], w['w_mv'], w['g_mk']]
    return pl.pallas_call(
        _memkv_kernel, grid=(m // tm,),
        in_specs=[row(D_MODEL)] + [_const_spec(c.shape) for c in consts],
        out_specs=[row(MEM_WIDTH), row(MEM_WIDTH)],
        out_shape=[jax.ShapeDtypeStruct((m, MEM_WIDTH), F32)] * 2,
        compiler_params=_params("arbitrary"),
        name="memkv_prompt",
    )(mem, *consts)


def _mem_query(x1, gmx_ref, wmq_ref, gmq_ref):
    qm = _dot((_rms(x1) * gmx_ref[...]).astype(BF16), wmq_ref[...])
    heads = []
    for h in range(MEM_HEADS):
        sl = slice(h * MEM_HEAD_DIM, (h + 1) * MEM_HEAD_DIM)
        heads.append((_rms(qm[:, sl]) * (gmq_ref[...] * MEM_SCALE)).astype(BF16))
    return heads


def _merge_prompt_kernel(x_ref, an_ref, bn_ref, wo_ref, gmx_ref, wmq_ref, gmq_ref, mk_ref, mv_ref, wmo_ref,
                         y_ref, o_s):
    x1 = x_ref[...] + _dot(an_ref[...], wo_ref[0:A_WIDTH, :]) + _dot(bn_ref[...], wo_ref[A_WIDTH:, :])
    heads = _mem_query(x1, gmx_ref, wmq_ref, gmq_ref)
    for h in range(MEM_HEADS):
        sl = slice(h * MEM_HEAD_DIM, (h + 1) * MEM_HEAD_DIM)
        s = _dot_nt(heads[h], mk_ref[0, :, sl].astype(BF16))
        p = jnp.exp(s - jnp.max(s, axis=-1, keepdims=True))
        o = _dot(p.astype(BF16), mv_ref[0, :, sl].astype(BF16)) / jnp.sum(p, axis=-1, keepdims=True)
        o_s[:, sl] = o.astype(BF16)
    y_ref[...] = x1 + _dot(o_s[...], wmo_ref[...])


def _merge_prompt(x, a_n, b_n, mk, mv, w, seq):
    m = x.shape[0]
    tm = TOK_TILE
    per_seq = seq // tm
    mem_len = mk.shape[1]
    row = lambda width: pl.BlockSpec((tm, width), lambda i: (i, 0))
    mem = pl.BlockSpec((1, mem_len, MEM_WIDTH), lambda i: (i // per_seq, 0, 0))
    c = lambda name: _const_spec(w[name].shape)
    return pl.pallas_call(
        _merge_prompt_kernel, grid=(m // tm,),
        in_specs=[row(D_MODEL), row(A_WIDTH), row(B_WIDTH), c('w_o'), c('g_mem_x'), c('w_mq'),
                  c('g_mq'), mem, mem, c('w_mo')],
        out_specs=row(D_MODEL),
        out_shape=jax.ShapeDtypeStruct((m, D_MODEL), F32),
        scratch_shapes=[pltpu.VMEM((tm, MEM_WIDTH), BF16)],
        compiler_params=_params("arbitrary"),
        name="merge_prompt",
    )(x, a_n, b_n, w['w_o'], w['g_mem_x'], w['w_mq'], w['g_mq'], mk, mv, w['w_mo'])


def _merge_sample_kernel(x_ref, an_ref, ctx_ref, wuvb_ref, gob_ref, wo_ref, gmx_ref, wmq_ref, gmq_ref,
                         x1_ref, qm_ref):
    b = _dot(ctx_ref[...].astype(BF16), wuvb_ref[...])
    bn = (_rms(b) * gob_ref[...]).astype(BF16)
    x1 = x_ref[...] + _dot(an_ref[...], wo_ref[0:A_WIDTH, :]) + _dot(bn, wo_ref[A_WIDTH:, :])
    x1_ref[...] = x1
    heads = _mem_query(x1, gmx_ref, wmq_ref, gmq_ref)
    for h in range(MEM_HEADS):
        qm_ref[:, h * MEM_HEAD_DIM:(h + 1) * MEM_HEAD_DIM] = heads[h]


def _merge_sample(x, a_n, ctx, w):
    m = x.shape[0]
    tm = TOK_TILE
    row = lambda width: pl.BlockSpec((tm, width), lambda i: (i, 0))
    c = lambda name: _const_spec(w[name].shape)
    return pl.pallas_call(
        _merge_sample_kernel, grid=(m // tm,),
        in_specs=[row(D_MODEL), row(A_WIDTH), row(ctx.shape[1]), c('w_uv_heads'), c('g_out_b'), c('w_o'),
                  c('g_mem_x'), c('w_mq'), c('g_mq')],
        out_specs=[row(D_MODEL), row(MEM_WIDTH)],
        out_shape=[jax.ShapeDtypeStruct((m, D_MODEL), F32), jax.ShapeDtypeStruct((m, MEM_WIDTH), BF16)],
        compiler_params=_params("arbitrary"),
        name="merge_sample",
    )(x, a_n, ctx, w['w_uv_heads'], w['g_out_b'], w['w_o'], w['g_mem_x'], w['w_mq'], w['g_mq'])


def _memattn_sample_kernel(q_ref, mk_ref, mv_ref, o_ref):
    nrow, nkey = q_ref.shape[1], mk_ref.shape[1]
    s = jnp.einsum('nqd,nkd->nqk', q_ref[...], mk_ref[...].astype(BF16), preferred_element_type=F32)
    same_head = (lax.broadcasted_iota(jnp.int32, (1, nrow, nkey), 1) % MEM_HEADS
                 == lax.broadcasted_iota(jnp.int32, (1, nrow, nkey), 2) % MEM_HEADS)
    s = jnp.where(same_head, s, NEG_INF)
    p = jnp.exp(s - jnp.max(s, axis=-1, keepdims=True))
    o = jnp.einsum('nqk,nkd->nqd', p.astype(BF16), mv_ref[...].astype(BF16), preferred_element_type=F32)
    o_ref[...] = (o / jnp.sum(p, axis=-1, keepdims=True)).astype(BF16)


def _memattn_sample(qm, mk, mv, first_seq):
    n_seq, nrow, _ = qm.shape
    nkey = mk.shape[1]
    ns = MEM_SEQ_TILE
    first_blk = first_seq // ns
    return pl.pallas_call(
        _memattn_sample_kernel, grid=(n_seq // ns,),
        in_specs=[pl.BlockSpec((ns, nrow, MEM_HEAD_DIM), lambda i: (i, 0, 0)),
                  pl.BlockSpec((ns, nkey, MEM_HEAD_DIM), lambda i: (first_blk + i, 0, 0)),
                  pl.BlockSpec((ns, nkey, MEM_HEAD_DIM), lambda i: (first_blk + i, 0, 0))],
        out_specs=pl.BlockSpec((ns, nrow, MEM_HEAD_DIM), lambda i: (i, 0, 0)),
        out_shape=jax.ShapeDtypeStruct((n_seq, nrow, MEM_HEAD_DIM), BF16),
        compiler_params=_params("arbitrary"),
        name="memattn_sample",
    )(qm, mk, mv)


def _ffn_kernel(*refs, sample, tiles_per_seq):
    if sample:
        (x_ref, o_ref, wmo_ref, gffn_ref, wup_ref, wd_ref, wc_ref, bc_ref, p1_ref, p2_ref,
         y_ref, gate_ref, h_s, g_s, act_s) = refs
        x2 = x_ref[...] + _dot(o_ref[...], wmo_ref[...])
        g_s[0:8, :] = jnp.zeros((8, D_FF), F32)
    else:
        (x_ref, gffn_ref, wup_ref, wd_ref, wc_ref, bc_ref, y_ref, tail_ref, h_s, g_s, act_s, carry_s) = refs
        x2 = x_ref[...]

        @pl.when(pl.program_id(0) == 0)
        def _():
            carry_s[...] = jnp.zeros_like(carry_s)
        g_s[0:8, :] = jnp.where((pl.program_id(0) % tiles_per_seq) == 0, 0.0, carry_s[...])
    tm = x_ref.shape[0]
    y_ref[...] = x2
    h_s[...] = (_rms(x2) * gffn_ref[...]).astype(BF16)
    if sample:
        t_in_seq = lax.broadcasted_iota(jnp.int32, (tm, 1), 0) % tiles_per_seq
    for f in range(N_FF_TILES):
        sl = slice(f * FF_TILE, (f + 1) * FF_TILE)
        h = h_s[...]
        g = _dot(h, wup_ref[:, sl])
        val = _dot(h, wup_ref[:, D_FF + f * FF_TILE:D_FF + (f + 1) * FF_TILE])
        g_s[8:, sl] = g
        g1 = g_s[7:7 + tm, sl]
        g2 = g_s[6:6 + tm, sl]
        if sample:
            gate_ref[:, sl] = g
            g1 = jnp.where(t_in_seq >= 1, g1, p1_ref[:, sl])
            g2 = jnp.where(t_in_seq >= 2, g2, p2_ref[:, sl])
        conv = bc_ref[:, sl] + g2 * wc_ref[0:1, sl] + g1 * wc_ref[1:2, sl] + g * wc_ref[2:3, sl]
        act_s[:, sl] = (conv / (1.0 + jnp.exp(-conv)) * val).astype(BF16)
    if not sample:
        tail = g_s[tm:tm + 8, :]
        carry_s[...] = tail
        tail_ref[0] = tail
    y_ref[...] += _dot(act_s[...], wd_ref[...])


def _ffn_scratch(tm):
    return [pltpu.VMEM((tm, D_MODEL), BF16), pltpu.VMEM((tm + 8, D_FF), F32), pltpu.VMEM((tm, D_FF), BF16)]


def _ffn_prompt(x2, w, n_batch, seq):
    m = x2.shape[0]
    tm = TOK_TILE
    per_seq = seq // tm
    row = pl.BlockSpec((tm, D_MODEL), lambda i: (i, 0))
    c = lambda name: _const_spec(w[name].shape)
    return pl.pallas_call(
        functools.partial(_ffn_kernel, sample=False, tiles_per_seq=per_seq), grid=(m // tm,),
        in_specs=[row, c('g_ffn'), c('w_up'), c('w_down'), c('w_conv'), c('b_conv')],
        out_specs=[row, pl.BlockSpec((1, 8, D_FF), lambda i: (i // per_seq, 0, 0))],
        out_shape=[jax.ShapeDtypeStruct((m, D_MODEL), F32), jax.ShapeDtypeStruct((n_batch, 8, D_FF), F32)],
        scratch_shapes=_ffn_scratch(tm) + [pltpu.VMEM((8, D_FF), F32)],
        compiler_params=_params("arbitrary"),
        name="ffn_prompt",
    )(x2, w['g_ffn'], w['w_up'], w['w_down'], w['w_conv'], w['b_conv'])


def _ffn_sample(x1, o, prev1, prev2, w, t_q):
    m = x1.shape[0]
    tm = TOK_TILE
    row = lambda width: pl.BlockSpec((tm, width), lambda i: (i, 0))
    c = lambda name: _const_spec(w[name].shape)
    return pl.pallas_call(
        functools.partial(_ffn_kernel, sample=True, tiles_per_seq=t_q), grid=(m // tm,),
        in_specs=[row(D_MODEL), row(MEM_WIDTH), c('w_mo'), c('g_ffn'), c('w_up'), c('w_down'),
                  c('w_conv'), c('b_conv'), row(D_FF), row(D_FF)],
        out_specs=[row(D_MODEL), row(D_FF)],
        out_shape=[jax.ShapeDtypeStruct((m, D_MODEL), F32), jax.ShapeDtypeStruct((m, D_FF), F32)],
        scratch_shapes=_ffn_scratch(tm),
        compiler_params=_params("arbitrary"),
        name="ffn_sample",
    )(x1, o, w['w_mo'], w['g_ffn'], w['w_up'], w['w_down'], w['w_conv'], w['b_conv'], prev1, prev2)


def _head_pad(wm, width):
    k = wm.shape[0]
    wm = wm.reshape(k, MLA_HEADS, width)
    return jnp.pad(wm, ((0, 0), (0, 0), (0, HEAD_PAD - width))).reshape(k, MLA_HEADS * HEAD_PAD)


def _rot_half_cols(wm):
    half = QK_ROPE // 2
    return jnp.concatenate([-wm[..., half:], wm[..., :half]], axis=-1)


def _lane_vec(nope, rope):
    return jnp.concatenate([nope, rope, jnp.zeros((LANES - QK_HEAD,), F32)])[None, :]


def _rope_tables(pos):
    half = QK_ROPE // 2
    inv_freq = ROPE_THETA ** (-jnp.arange(half, dtype=F32) / half)
    ang = pos.astype(F32)[:, None] * inv_freq[None, :]
    return jnp.cos(ang), jnp.sin(ang)


def _head_tile_tables(pos):
    cos, sin = _rope_tables(pos)
    n = pos.shape[0]
    cos_t = jnp.concatenate([jnp.ones((n, QK_NOPE), F32), cos, cos, jnp.zeros((n, LANES - QK_HEAD), F32)], axis=1)
    sin_t = jnp.concatenate([jnp.zeros((n, QK_NOPE), F32), sin, sin, jnp.zeros((n, LANES - QK_HEAD), F32)], axis=1)
    return cos_t, sin_t


def _feature_table(pos, cols):
    cos, sin = _rope_tables(pos)
    t = jnp.concatenate([cos, cos, sin, sin], axis=1).T
    return jnp.pad(t, ((0, 0), (0, cols - pos.shape[0])))


def _query_feature_perm():
    half = QK_ROPE // 2
    p = np.zeros((LANES, LANES), np.float32)
    for j in range(QK_ROPE):
        p[QK_NOPE + j, j] = 1.0
    for j in range(half):
        p[QK_NOPE + half + j, QK_ROPE + j] = 1.0
        p[QK_NOPE + j, QK_ROPE + half + j] = -1.0
    return jnp.asarray(p, BF16)


def _prep_weights(l, t_q, g_mix, w_in, ln_v_g, ln_v_b, w_s, b_s, g_q_a, w_uq, g_kv_a, w_uk, w_uv, g_qk_q, g_qk_k,
                  g_out_a, g_out_b, w_o, g_mem_x, g_mem_in, w_mq, w_mk, w_mv, g_mq, g_mk, w_mo, g_ffn, w_up,
                  w_conv, b_conv, w_down):
    half = QK_ROPE // 2
    o_kpe = 2 * A_WIDTH + Q_LORA + KV_LORA
    vec = lambda a: a.reshape(1, -1).astype(F32)
    w = {}
    wkpe = w_in[l][:, o_kpe:]
    lane_tile = lambda blk: jnp.pad(blk, ((0, 0), (QK_NOPE, LANES - QK_HEAD)))
    w['w_in'] = jnp.concatenate([w_in[l][:, :o_kpe], lane_tile(wkpe), lane_tile(_rot_half_cols(wkpe))],
                                axis=1).astype(BF16)
    w['g_mix'] = vec(g_mix[l])
    w['ln_g'] = vec(ln_v_g[l])
    w['ln_b'] = vec(ln_v_b[l])
    w['g_q_a'] = vec(g_q_a[l])
    w['g_kv_a'] = vec(g_kv_a[l])
    w['g_out_a'] = vec(g_out_a[l])
    w['g_out_b'] = vec(g_out_b[l])
    w['g_out_b_col'] = g_out_b[l][:, None]
    wq = w_uq[l].reshape(Q_LORA, MLA_HEADS, QK_HEAD)
    wq_rot = jnp.concatenate([jnp.zeros((Q_LORA, MLA_HEADS, QK_NOPE), F32), _rot_half_cols(wq[..., QK_NOPE:])], axis=-1)
    w['wq'] = _head_pad(wq.reshape(Q_LORA, -1), QK_HEAD).astype(BF16)
    w['wq_rot'] = _head_pad(wq_rot.reshape(Q_LORA, -1), QK_HEAD).astype(BF16)
    w['wk'] = _head_pad(w_uk[l], QK_NOPE).astype(BF16)
    w['wvt'] = w_uv[l].T.astype(BF16)
    gq, gk = g_qk_q[l], g_qk_k[l]
    sw = lambda g: jnp.concatenate([g[QK_NOPE + half:], g[QK_NOPE:QK_NOPE + half]])
    zeros_n = jnp.zeros((QK_NOPE,), F32)
    w['gq'] = _lane_vec(gq[:QK_NOPE], gq[QK_NOPE:])
    w['gq_sw'] = _lane_vec(zeros_n, sw(gq))
    w['gk'] = _lane_vec(gk[:QK_NOPE], gk[QK_NOPE:])
    w['gk_sw'] = _lane_vec(zeros_n, sw(gk))
    w['gk_nope'] = _lane_vec(gk[:QK_NOPE], jnp.zeros((QK_ROPE,), F32))
    w['pq'] = _query_feature_perm()
    w['wukt'] = w_uk[l].T.astype(BF16)
    gr = gk[QK_NOPE:]
    w['gcol'] = jnp.concatenate([gr, gr])[:, None]
    wuv = w_uv[l].reshape(KV_LORA, MLA_HEADS, V_HEAD)
    eye = jnp.eye(MLA_HEADS, dtype=F32)
    w['w_uv_heads'] = (wuv[None, :, :, :] * eye[:, None, :, None]).reshape(MLA_HEADS * KV_LORA, B_WIDTH).astype(BF16)
    tril = jnp.tril(jnp.ones((CHUNK, CHUNK), F32))
    w['ws_prompt'] = (w_s[l] * tril).astype(BF16)
    w['bs_prompt'] = jnp.repeat(b_s[l].T, A_GROUP_DIM, axis=1)
    reps = CHUNK // t_q
    blk = w_s[l][:, :t_q, :t_q] * jnp.tril(jnp.ones((t_q, t_q), F32))
    w['ws_sample'] = jnp.einsum('ij,gts->gitjs', jnp.eye(reps, dtype=F32), blk).reshape(A_GROUPS, CHUNK, CHUNK).astype(BF16)
    w['bs_sample'] = jnp.repeat(jnp.tile(b_s[l][:, :t_q].T, (reps, 1)), A_GROUP_DIM, axis=1)
    w['w_o'] = w_o[l].astype(BF16)
    w['g_mem_x'] = vec(g_mem_x[l])
    w['g_mem_in'] = vec(g_mem_in[l])
    w['w_mq'] = w_mq[l].astype(BF16)
    w['w_mk'] = w_mk[l].astype(BF16)
    w['w_mv'] = w_mv[l].astype(BF16)
    w['g_mq'] = vec(g_mq[l])
    w['g_mk'] = vec(g_mk[l])
    w['w_mo'] = w_mo[l].astype(BF16)
    w['g_ffn'] = vec(g_ffn[l])
    w['w_up'] = w_up[l].astype(BF16)
    w['w_down'] = w_down[l].astype(BF16)
    w['w_conv'] = w_conv[l]
    w['b_conv'] = vec(b_conv[l])
    return w


def kernel(x_prompt, x_sample, cache_ckv, cache_kpe, cache_mem_k, cache_mem_v, state_ffn_conv, page_table, mem_prompt, g_mix, w_in, ln_v_g, ln_v_b, w_s, b_s, g_q_a, w_uq, g_kv_a, w_uk, w_uv, g_qk_q, g_qk_k, g_out_a, g_out_b, w_o, g_mem_x, g_mem_in, w_mq, w_mk, w_mv, g_mq, g_mk, w_mo, g_ffn, w_up, w_conv, b_conv, w_down):
    n_p, seq, _ = x_prompt.shape
    n_s, t_q, _ = x_sample.shape
    depth = g_mix.shape[0]
    mem_len = mem_prompt.shape[1]
    n_pages = page_table.shape[1]
    past = n_pages * PAGE_SIZE

    pos_p = jnp.arange(seq)
    pos_s = past + jnp.arange(t_q)
    cos_p, sin_p = _head_tile_tables(pos_p)
    cos_s, sin_s = _head_tile_tables(jnp.tile(pos_s, TOK_TILE // t_q))
    n_kv_tiles = past // KV_TILE
    t_past = _feature_table(jnp.arange(past), past).reshape(2 * QK_ROPE, n_kv_tiles, KV_TILE).transpose(1, 0, 2)
    t_new = _feature_table(pos_s, LANES)
    n_phys = cache_ckv.shape[1]
    ckv_pages = cache_ckv.reshape(depth * n_phys, PAGE_SIZE, KV_LORA)
    kpe_pages_t = jnp.swapaxes(cache_kpe, -1, -2).reshape(depth * n_phys, QK_ROPE, PAGE_SIZE)
    mem_rows = mem_len * MEM_HEADS
    mem_k_rows = cache_mem_k.reshape(depth * n_s, mem_rows, MEM_HEAD_DIM)
    mem_v_rows = cache_mem_v.reshape(depth * n_s, mem_rows, MEM_HEAD_DIM)

    xp = x_prompt.reshape(n_p * seq, D_MODEL)
    xs = x_sample.reshape(n_s * t_q, D_MODEL)
    outs = {k: [] for k in ('p_ckv', 'p_kpe', 'p_mk', 'p_mv', 'p_conv', 's_ckv', 's_kpe', 's_chunk_v', 's_conv')}
    for l in range(depth):
        w = _prep_weights(l, t_q, g_mix, w_in, ln_v_g, ln_v_b, w_s, b_s, g_q_a, w_uq, g_kv_a, w_uk, w_uv, g_qk_q,
                          g_qk_k, g_out_a, g_out_b, w_o, g_mem_x, g_mem_in, w_mq, w_mk, w_mv, g_mq, g_mk, w_mo,
                          g_ffn, w_up, w_conv, b_conv, w_down)
        w['t_past'], w['t_new'] = t_past, t_new

        wp = dict(w, ws=w['ws_prompt'], bs=w['bs_prompt'])
        q, k, vt, ckvn, kpe, a_n = _inproj(xp, cos_p, sin_p, wp, sample=False)
        b_n = _flash(q, k, vt, w['g_out_b_col'], n_p, seq)
        mk, mv = _memkv(mem_prompt.reshape(n_p * mem_len, D_MODEL), w)
        mk3 = mk.reshape(n_p, mem_len, MEM_WIDTH)
        mv3 = mv.reshape(n_p, mem_len, MEM_WIDTH)
        x2 = _merge_prompt(xp, a_n, b_n, mk3, mv3, w, seq)
        xp, tail = _ffn_prompt(x2, w, n_p, seq)
        outs['p_ckv'].append(ckvn.reshape(n_p, seq, KV_LORA))
        outs['p_kpe'].append(kpe.reshape(n_p, seq, QK_ROPE))
        outs['p_mk'].append(mk.reshape(n_p, mem_len, MEM_HEADS, MEM_HEAD_DIM))
        outs['p_mv'].append(mv.reshape(n_p, mem_len, MEM_HEADS, MEM_HEAD_DIM))
        outs['p_conv'].append(tail[:, 8 - (CONV_W - 1):, :])

        wsm = dict(w, ws=w['ws_sample'], bs=w['bs_sample'])
        qlat, qrf, ckvn_s, kpe_s, a_ns, vg = _inproj(xs, cos_s, sin_s, wsm, sample=True)
        nq = t_q * MLA_HEADS
        kpnew_t = jnp.pad(jnp.swapaxes(kpe_s.reshape(n_s, t_q, QK_ROPE), 1, 2), ((0, 0), (0, 0), (0, LANES - t_q)))
        ctx = _paged(page_table + l * n_phys, qlat.reshape(n_s, nq, KV_LORA), qrf.reshape(n_s, nq, LANES),
                     ckvn_s.reshape(n_s, t_q, KV_LORA), kpnew_t, ckv_pages, kpe_pages_t, w)
        x1, qm = _merge_sample(xs, a_ns, ctx.reshape(n_s * t_q, MLA_HEADS * KV_LORA), w)
        o = _memattn_sample(qm.reshape(n_s, t_q * MEM_HEADS, MEM_HEAD_DIM), mem_k_rows, mem_v_rows, l * n_s)
        st = state_ffn_conv[l]
        zero = jnp.zeros_like(st[:, :1])
        prev1 = jnp.concatenate([st[:, 1:2]] + [zero] * (t_q - 1), axis=1)
        prev2 = jnp.concatenate([st[:, 0:1], st[:, 1:2]] + [zero] * (t_q - 2), axis=1)
        flat = lambda a: a.reshape(n_s * t_q, D_FF)
        xs, gate = _ffn_sample(x1, o.reshape(n_s * t_q, MEM_WIDTH), flat(prev1), flat(prev2), w, t_q)
        gate = gate.reshape(n_s, t_q, D_FF)
        outs['s_ckv'].append(ckvn_s.reshape(n_s, t_q, KV_LORA))
        outs['s_kpe'].append(kpe_s.reshape(n_s, t_q, QK_ROPE))
        outs['s_chunk_v'].append(vg.reshape(n_s, t_q, A_GROUPS, A_GROUP_DIM))
        outs['s_conv'].append(gate[:, t_q - (CONV_W - 1):, :])

    return (xp.reshape(n_p, seq, D_MODEL), xs.reshape(n_s, t_q, D_MODEL),
            jnp.stack(outs['p_ckv']), jnp.stack(outs['p_kpe']), jnp.stack(outs['p_mk']), jnp.stack(outs['p_mv']),
            jnp.stack(outs['p_conv']), jnp.stack(outs['s_ckv']), jnp.stack(outs['s_kpe']),
            jnp.stack(outs['s_chunk_v']), jnp.stack(outs['s_conv']))
```

```python
import functools

import numpy as np
import jax
import jax.numpy as jnp
from jax import lax
from jax.experimental import pallas as pl
from jax.experimental.pallas import tpu as pltpu

D_MODEL = 1024
A_GROUPS = 8
A_GROUP_DIM = 64
A_WIDTH = A_GROUPS * A_GROUP_DIM
CHUNK = 128
MLA_HEADS = 8
Q_LORA = 256
KV_LORA = 256
QK_NOPE = 64
QK_ROPE = 32
QK_HEAD = QK_NOPE + QK_ROPE
V_HEAD = 64
B_WIDTH = MLA_HEADS * V_HEAD
ROPE_THETA = 10000.0
PAGE_SIZE = 128
MEM_HEADS = 4
MEM_HEAD_DIM = 128
MEM_WIDTH = MEM_HEADS * MEM_HEAD_DIM
D_FF = 2816
CONV_W = 3
EPS = 1e-6
NEG_INF = -1e30

LANES = 128
HEAD_PAD = LANES
QK_SCALE = QK_HEAD ** -0.5
LOG2E = 1.4426950408889634
V_AUG = V_HEAD + 16
MEM_SCALE = MEM_HEAD_DIM ** -0.5

TOK_TILE = 512
FF_TILE = 256
N_FF_TILES = D_FF // FF_TILE
ATT_TILE = 256
KV_TILE = 1024
KV_SUB = 256
MEM_SEQ_TILE = 8
VMEM_LIMIT = 56 * 1024 * 1024

F32 = jnp.float32
BF16 = jnp.bfloat16


def _dot(a, b):
    return jnp.dot(a, b, preferred_element_type=F32)


def _dot_nt(a, b):
    return lax.dot_general(a, b, (((1,), (1,)), ((), ())), preferred_element_type=F32)


def _rms(x):
    return x * lax.rsqrt(jnp.mean(x * x, axis=-1, keepdims=True) + EPS)


def _gelu(x):
    return jax.nn.gelu(x, approximate=True)


def _const_spec(shape):
    nd = len(shape)
    return pl.BlockSpec(shape, lambda *_: (0,) * nd, pipeline_mode=pl.Buffered(1))


def _params(*sem):
    return pltpu.CompilerParams(dimension_semantics=sem, vmem_limit_bytes=VMEM_LIMIT)


def _inproj_kernel(x_ref, gmix_ref, win_ref, lng_ref, lnb_ref, ws_ref, bs_ref, gqa_ref, wq_ref, wqr_ref,
                   gkva_ref, wk_ref, wv_ref, gq_ref, gqs_ref, gk_ref, gks_ref, cos_ref, sin_ref, goa_ref,
                   pq_ref, *rest, sample):
    if sample:
        qlat_ref, qrf_ref, ckvn_ref, kpe_ref, an_ref, vg_ref, zu_s, vgb_s, a_s = rest
    else:
        q_ref, k_ref, v_ref, ckvn_ref, kpe_ref, an_ref, zu_s, vgb_s, a_s = rest
    tm = x_ref.shape[0]
    xn = (_rms(x_ref[...]) * gmix_ref[...]).astype(BF16)

    o_cq = 2 * A_WIDTH
    o_ckv = o_cq + Q_LORA
    o_kpe = o_ckv + KV_LORA
    zv = _dot(xn, win_ref[:, A_WIDTH:2 * A_WIDTH])
    cq = _dot(xn, win_ref[:, o_cq:o_ckv])
    ckv = _dot(xn, win_ref[:, o_ckv:o_kpe])
    kpt = _dot(xn, win_ref[:, o_kpe:o_kpe + LANES])
    if not sample:
        kpr = _dot(xn, win_ref[:, o_kpe + LANES:o_kpe + 2 * LANES])
    zu_s[...] = _dot(xn, win_ref[:, 0:A_WIDTH])

    cqn = (_rms(cq) * gqa_ref[...]).astype(BF16)
    ckvn = _rms(ckv) * gkva_ref[...]
    ckvn_ref[...] = ckvn
    kpe_ref[...] = kpt[:, QK_NOPE:QK_HEAD]
    qall = _dot(cqn, wq_ref[...])
    qrall = _dot(cqn, wqr_ref[...])
    if not sample:
        cb = ckvn.astype(BF16)
        kall = _dot(cb, wk_ref[...])
        vt = _dot_nt(wv_ref[...], cb).astype(BF16)
        ones = jnp.ones((V_AUG - V_HEAD, ATT_TILE), BF16)
        for c in range(tm // ATT_TILE):
            cols = slice(c * ATT_TILE, (c + 1) * ATT_TILE)
            for h in range(MLA_HEADS):
                v_ref[c, h * V_AUG:h * V_AUG + V_HEAD, :] = vt[h * V_HEAD:(h + 1) * V_HEAD, cols]
                v_ref[c, h * V_AUG + V_HEAD:(h + 1) * V_AUG, :] = ones

    gv = _gelu(zv)
    low = lax.broadcasted_iota(jnp.int32, (1, LANES), 1) < A_GROUP_DIM
    inv_g = 1.0 / A_GROUP_DIM
    for p in range(A_WIDTH // LANES):
        sl = slice(p * LANES, (p + 1) * LANES)
        t = gv[:, sl]
        s_lo = jnp.sum(jnp.where(low, t, 0.0), axis=-1, keepdims=True)
        s_hi = jnp.sum(jnp.where(low, 0.0, t), axis=-1, keepdims=True)
        tc = t - jnp.where(low, s_lo, s_hi) * inv_g
        tc2 = tc * tc
        v_lo = jnp.sum(jnp.where(low, tc2, 0.0), axis=-1, keepdims=True)
        v_hi = jnp.sum(jnp.where(low, 0.0, tc2), axis=-1, keepdims=True)
        y = tc * lax.rsqrt(jnp.where(low, v_lo, v_hi) * inv_g + EPS) * lng_ref[:, sl] + lnb_ref[:, sl]
        if sample:
            vg_ref[:, sl] = y
        vgb_s[:, sl] = y.astype(BF16)
    for c in range(tm // CHUNK):
        rows = slice(c * CHUNK, (c + 1) * CHUNK)
        for p in range(A_WIDTH // LANES):
            sl = slice(p * LANES, (p + 1) * LANES)
            vp = vgb_s[rows, sl]
            mixed = jnp.where(low, _dot(ws_ref[2 * p], vp), _dot(ws_ref[2 * p + 1], vp)) + bs_ref[:, sl]
            a_s[rows, sl] = _gelu(zu_s[rows, sl]) * mixed
    an_ref[...] = (_rms(a_s[...]) * goa_ref[...]).astype(BF16)

    cosv = cos_ref[...]
    sinv = sin_ref[...]
    q_scale = QK_SCALE if sample else QK_SCALE * LOG2E
    qa = gq_ref[...] * cosv * q_scale
    qb = gqs_ref[...] * sinv * q_scale
    inv_d = 1.0 / QK_HEAD
    if not sample:
        ka = gk_ref[...] * cosv
        kb = gks_ref[...] * sinv
    for h in range(MLA_HEADS):
        sl = slice(h * HEAD_PAD, (h + 1) * HEAD_PAD)
        qh = qall[:, sl]
        r = lax.rsqrt(jnp.sum(qh * qh, axis=-1, keepdims=True) * inv_d + EPS)
        qf = r * (qh * qa + qrall[:, sl] * qb)
        if sample:
            qg = (qf * gk_ref[...]).astype(BF16)
            qlat_ref[:, h * KV_LORA:(h + 1) * KV_LORA] = _dot_nt(qg, wk_ref[:, sl]).astype(BF16)
            qrf_ref[:, sl] = _dot(qf.astype(BF16), pq_ref[...]).astype(BF16)
        else:
            q_ref[:, sl] = qf.astype(BF16)
            kh = kall[:, sl] + kpt
            rk = lax.rsqrt(jnp.sum(kh * kh, axis=-1, keepdims=True) * inv_d + EPS)
            k_ref[:, sl] = (rk * (kh * ka + kpr * kb)).astype(BF16)


def _inproj(x, cos, sin, w, *, sample):
    m = x.shape[0]
    tm = TOK_TILE
    n_tiles = m // tm
    n_pos_tiles = cos.shape[0] // tm
    row = lambda width: pl.BlockSpec((tm, width), lambda i: (i, 0))
    pos = pl.BlockSpec((tm, LANES), lambda i: (i % n_pos_tiles, 0))
    consts = [w['g_mix'], w['w_in'], w['ln_g'], w['ln_b'], w['ws'], w['bs'], w['g_q_a'], w['wq'], w['wq_rot'],
              w['g_kv_a'], w['wk'], w['wvt'], w['gq'], w['gq_sw'], w['gk_nope'] if sample else w['gk'], w['gk_sw']]
    in_specs = ([row(D_MODEL)] + [_const_spec(c.shape) for c in consts] + [pos, pos]
                + [_const_spec(w['g_out_a'].shape), _const_spec(w['pq'].shape)])
    args = [x] + consts + [cos, sin, w['g_out_a'], w['pq']]
    wide = MLA_HEADS * HEAD_PAD
    if sample:
        out_shape = [jax.ShapeDtypeStruct((m, MLA_HEADS * KV_LORA), BF16), jax.ShapeDtypeStruct((m, wide), BF16),
                     jax.ShapeDtypeStruct((m, KV_LORA), F32), jax.ShapeDtypeStruct((m, QK_ROPE), F32),
                     jax.ShapeDtypeStruct((m, A_WIDTH), BF16), jax.ShapeDtypeStruct((m, A_WIDTH), F32)]
        out_specs = [row(MLA_HEADS * KV_LORA), row(wide), row(KV_LORA), row(QK_ROPE), row(A_WIDTH), row(A_WIDTH)]
    else:
        att_tiles = tm // ATT_TILE
        out_shape = [jax.ShapeDtypeStruct((m, wide), BF16)] * 2 + [
            jax.ShapeDtypeStruct((m // ATT_TILE, MLA_HEADS * V_AUG, ATT_TILE), BF16),
            jax.ShapeDtypeStruct((m, KV_LORA), F32), jax.ShapeDtypeStruct((m, QK_ROPE), F32),
            jax.ShapeDtypeStruct((m, A_WIDTH), BF16)]
        out_specs = [row(wide)] * 2 + [pl.BlockSpec((att_tiles, MLA_HEADS * V_AUG, ATT_TILE), lambda i: (i, 0, 0)),
                                       row(KV_LORA), row(QK_ROPE), row(A_WIDTH)]
    return pl.pallas_call(
        functools.partial(_inproj_kernel, sample=sample),
        grid=(n_tiles,), in_specs=in_specs, out_specs=out_specs, out_shape=out_shape,
        scratch_shapes=[pltpu.VMEM((tm, A_WIDTH), F32), pltpu.VMEM((tm, A_WIDTH), BF16),
                        pltpu.VMEM((tm, A_WIDTH), F32)],
        compiler_params=_params("arbitrary"),
        name="inproj_sample" if sample else "inproj_prompt",
    )(*args)


def _flash_kernel(q_ref, k_ref, vt_ref, gob_ref, o_ref):
    i = pl.program_id(1)
    t = ATT_TILE
    causal = lax.broadcasted_iota(jnp.int32, (t, t), 0) <= lax.broadcasted_iota(jnp.int32, (t, t), 1)

    def step(j, carry, masked):
        start = pl.multiple_of(j * t, t)
        vt = vt_ref[j]
        heads = [slice(h * HEAD_PAD, (h + 1) * HEAD_PAD) for h in range(MLA_HEADS)]
        scores = [_dot_nt(k_ref[pl.ds(start, t), sl], q_ref[:, sl]) for sl in heads]
        stats = []
        for h in range(MLA_HEADS):
            m = carry[h][0]
            s = jnp.where(causal, scores[h], NEG_INF) if masked else scores[h]
            m_new = jnp.maximum(m, jnp.max(s, axis=0, keepdims=True))
            stats.append((m_new, jnp.exp2(m - m_new), jnp.exp2(s - m_new).astype(BF16)))
        new = []
        for h, (m_new, alpha, p) in enumerate(stats):
            pv = _dot(vt[h * V_AUG:(h + 1) * V_AUG, :], p)
            _, l, acc = carry[h]
            new.append((m_new, alpha * l + pv[V_HEAD:V_HEAD + 1], alpha * acc + pv[:V_HEAD]))
        return tuple(new)

    init = tuple((jnp.full((1, t), NEG_INF, F32), jnp.zeros((1, t), F32), jnp.zeros((V_HEAD, t), F32))
                 for _ in range(MLA_HEADS))
    carry = lax.fori_loop(0, i, functools.partial(step, masked=False), init)
    carry = step(i, carry, True)
    o_t = jnp.concatenate([acc / l for _, l, acc in carry], axis=0)
    bn_t = o_t * lax.rsqrt(jnp.mean(o_t * o_t, axis=0, keepdims=True) + EPS) * gob_ref[...]
    o_ref[...] = bn_t.T.astype(BF16)


def _flash(q, k, vt, g_out_b_col, n_batch, seq):
    m = q.shape[0]
    nq = seq // ATT_TILE
    wide = MLA_HEADS * HEAD_PAD
    return pl.pallas_call(
        _flash_kernel,
        grid=(n_batch, nq),
        in_specs=[pl.BlockSpec((ATT_TILE, wide), lambda b, i: (b * nq + i, 0)),
                  pl.BlockSpec((seq, wide), lambda b, i: (b, 0)),
                  pl.BlockSpec((nq, MLA_HEADS * V_AUG, ATT_TILE), lambda b, i: (b, 0, 0)),
                  pl.BlockSpec((B_WIDTH, 1), lambda b, i: (0, 0))],
        out_specs=pl.BlockSpec((ATT_TILE, B_WIDTH), lambda b, i: (b * nq + i, 0)),
        out_shape=jax.ShapeDtypeStruct((m, B_WIDTH), BF16),
        compiler_params=_params("arbitrary", "arbitrary"),
        name="flash_prompt",
    )(q, k, vt, g_out_b_col)


def _paged_kernel(pt_ref, qlat_ref, qrf_ref, qlat_next_ref, qrf_next_ref, cnew_ref, kpnew_ref, wukt_ref, gcol_ref,
                  tpast_ref, tnew_ref, ckv_hbm, kpe_hbm, ctx_ref, cbuf, pbuf, cn_s, tg_s, s0_s, sems, *, n_seq, n_pages):
    n = pl.program_id(0)
    slot = n % 2
    ppt = KV_TILE // PAGE_SIZE
    n_tiles = n_pages // ppt

    def start_fetch(seq, sl):
        def body(p, _):
            page = pt_ref[seq, p]
            pltpu.make_async_copy(ckv_hbm.at[page], cbuf.at[sl, p], sems.at[0, sl]).start()
            pltpu.make_async_copy(kpe_hbm.at[page], pbuf.at[sl, p], sems.at[1, sl]).start()
            return 0
        lax.fori_loop(0, n_pages, body, 0, unroll=8)

    def wait_fetch(sl):
        pltpu.make_async_copy(ckv_hbm.at[pl.ds(0, n_pages)], cbuf.at[sl], sems.at[0, sl]).wait()
        pltpu.make_async_copy(kpe_hbm.at[pl.ds(0, n_pages)], pbuf.at[sl], sems.at[1, sl]).wait()

    nq = qlat_ref.shape[1]
    t_q = nq // MLA_HEADS
    n_nope = MLA_HEADS * QK_NOPE
    n_feat = 3 * QK_ROPE
    ones_rows = jnp.where(lax.broadcasted_iota(jnp.int32, (16, n_feat), 1) >= 2 * QK_ROPE, 1.0, 0.0).astype(BF16)

    def query_mats(ql_ref, qr_ref):
        a_mat = jnp.concatenate([wukt_ref[...], ql_ref[0]], axis=0)
        l_mat = jnp.concatenate([qr_ref[0][:, :n_feat], ones_rows], axis=0)
        return a_mat, l_mat

    def scores(mats, cb, kpt, tg):
        a_mat, l_mat = mats
        tk = cb.shape[0]
        m1 = _dot_nt(a_mat, cb)
        knt = m1[:n_nope]
        ssq = jnp.sum((knt * knt).reshape(MLA_HEADS, QK_NOPE, tk), axis=1)
        feat = jnp.concatenate([kpt * tg[:QK_ROPE], kpt * tg[QK_ROPE:], kpt * kpt], axis=0)
        m2 = _dot(l_mat, feat.astype(BF16))
        r = lax.rsqrt((ssq + m2[nq:nq + MLA_HEADS]) * (1.0 / QK_HEAD) + EPS)
        s = (m1[n_nope:] + m2[:nq]).reshape(t_q, MLA_HEADS, tk) * r[None]
        return s.reshape(nq, tk)

    def update(carry, s, cb):
        m, l, acc = carry
        m_new = jnp.maximum(m, jnp.max(s, axis=-1, keepdims=True))
        alpha = jnp.exp(m - m_new)
        p = jnp.exp(s - m_new)
        l = alpha * l + jnp.sum(p, axis=-1, keepdims=True)
        acc = alpha * acc + _dot(p.astype(BF16), cb)
        return m_new, l, acc

    pps = KV_SUB // PAGE_SIZE

    def tile_scores(mats, sl, t):
        parts, cbs = [], []
        for sub in range(KV_TILE // KV_SUB):
            p0 = t * ppt + sub * pps
            cb = cbuf[sl, p0:p0 + pps].reshape(KV_SUB, KV_LORA).astype(BF16)
            kpt = jnp.concatenate([pbuf[sl, p0 + i] for i in range(pps)], axis=1)
            parts.append(scores(mats, cb, kpt, tg_s[t, :, sub * KV_SUB:(sub + 1) * KV_SUB]))
            cbs.append(cb)
        return jnp.concatenate(parts, axis=1), jnp.concatenate(cbs, axis=0)

    @pl.when(n == 0)
    def _():
        cn_s[...] = jnp.zeros_like(cn_s)
        for t in range(n_tiles):
            tg_s[t] = tpast_ref[t] * gcol_ref[...]
        start_fetch(0, 0)

    has_next = n + 1 < n_seq

    @pl.when(has_next)
    def _():
        start_fetch(n + 1, 1 - slot)

    @pl.when(n == 0)
    def _():
        wait_fetch(0)
        s0_s[...] = tile_scores(query_mats(qlat_ref, qrf_ref), 0, 0)[0]

    cur = query_mats(qlat_ref, qrf_ref)
    nslot = jnp.where(has_next, 1 - slot, slot)
    s_cur = s0_s[...]
    cb_cur = cbuf[slot, 0:ppt].reshape(KV_TILE, KV_LORA).astype(BF16)
    stats = (jnp.full((nq, 1), NEG_INF, F32), jnp.zeros((nq, 1), F32), jnp.zeros((nq, KV_LORA), F32))
    for t in range(n_tiles):
        if t + 1 < n_tiles:
            s_next, cb_next = tile_scores(cur, slot, t + 1)
        else:
            @pl.when(has_next)
            def _():
                wait_fetch(1 - slot)
            s_next, cb_next = tile_scores(query_mats(qlat_next_ref, qrf_next_ref), nslot, 0)
        stats = update(stats, s_cur, cb_cur)
        s_cur, cb_cur = s_next, cb_next
    s0_s[...] = s_cur

    cn_s[0:t_q, :] = cnew_ref[0]
    cb_new = cn_s[...].astype(BF16)
    s_new = scores(cur, cb_new, kpnew_ref[0], tnew_ref[...] * gcol_ref[...])
    key = lax.broadcasted_iota(jnp.int32, (nq, LANES), 1)
    qry = lax.broadcasted_iota(jnp.int32, (nq, LANES), 0) // MLA_HEADS
    m, l, acc = update(stats, jnp.where(key <= qry, s_new, NEG_INF), cb_new)
    ctx_ref[0] = acc / l


def _paged(page_table, qlat, qrf, cnew, kpnew_t, cache_ckv, cache_kpe_t, w):
    n_seq, n_pages = page_table.shape
    nq = qlat.shape[1]
    t_q = cnew.shape[1]
    seq_spec = lambda shape: pl.BlockSpec((1,) + shape, lambda n, pt: (n, 0, 0))
    next_spec = lambda shape: pl.BlockSpec((1,) + shape, lambda n, pt: (jnp.minimum(n + 1, n_seq - 1), 0, 0))
    const = lambda shape: pl.BlockSpec(shape, lambda n, pt: (0,) * len(shape), pipeline_mode=pl.Buffered(1))
    grid_spec = pltpu.PrefetchScalarGridSpec(
        num_scalar_prefetch=1, grid=(n_seq,),
        in_specs=[seq_spec((nq, KV_LORA)), seq_spec((nq, LANES)), next_spec((nq, KV_LORA)), next_spec((nq, LANES)),
                  seq_spec((t_q, KV_LORA)), seq_spec((QK_ROPE, LANES)),
                  const(w['wukt'].shape), const(w['gcol'].shape), const(w['t_past'].shape), const(w['t_new'].shape),
                  pl.BlockSpec(memory_space=pl.ANY), pl.BlockSpec(memory_space=pl.ANY)],
        out_specs=seq_spec((nq, KV_LORA)),
        scratch_shapes=[pltpu.VMEM((2, n_pages, PAGE_SIZE, KV_LORA), F32),
                        pltpu.VMEM((2, n_pages, QK_ROPE, PAGE_SIZE), F32),
                        pltpu.VMEM((LANES, KV_LORA), F32), pltpu.VMEM(w['t_past'].shape, F32),
                        pltpu.VMEM((nq, KV_TILE), F32), pltpu.SemaphoreType.DMA((2, 2))])
    return pl.pallas_call(
        functools.partial(_paged_kernel, n_seq=n_seq, n_pages=n_pages),
        grid_spec=grid_spec,
        out_shape=jax.ShapeDtypeStruct((n_seq, nq, KV_LORA), F32),
        compiler_params=_params("arbitrary"),
        name="paged_sample",
    )(page_table, qlat, qrf, qlat, qrf, cnew, kpnew_t, w['wukt'], w['gcol'], w['t_past'], w['t_new'],
      cache_ckv, cache_kpe_t)


def _memkv_kernel(mem_ref, gin_ref, wmk_ref, wmv_ref, gmk_ref, k_ref, v_ref):
    hm = (_rms(mem_ref[...]) * gin_ref[...]).astype(BF16)
    kk = _dot(hm, wmk_ref[...])
    for h in range(MEM_HEADS):
        sl = slice(h * MEM_HEAD_DIM, (h + 1) * MEM_HEAD_DIM)
        k_ref[:, sl] = _rms(kk[:, sl]) * gmk_ref[...]
    v_ref[...] = _dot(hm, wmv_ref[...])


def _memkv(mem, w):
    m = mem.shape[0]
    tm = TOK_TILE
    row = lambda width: pl.BlockSpec((tm, width), lambda i: (i, 0))
    consts = [w['g_mem_in'], w['w_mk'], w['w_mv'], w['g_mk']]
    return pl.pallas_call(
        _memkv_kernel, grid=(m // tm,),
        in_specs=[row(D_MODEL)] + [_const_spec(c.shape) for c in consts],
        out_specs=[row(MEM_WIDTH), row(MEM_WIDTH)],
        out_shape=[jax.ShapeDtypeStruct((m, MEM_WIDTH), F32)] * 2,
        compiler_params=_params("arbitrary"),
        name="memkv_prompt",
    )(mem, *consts)


def _mem_query(x1, gmx_ref, wmq_ref, gmq_ref):
    qm = _dot((_rms(x1) * gmx_ref[...]).astype(BF16), wmq_ref[...])
    heads = []
    for h in range(MEM_HEADS):
        sl = slice(h * MEM_HEAD_DIM, (h + 1) * MEM_HEAD_DIM)
        heads.append((_rms(qm[:, sl]) * (gmq_ref[...] * MEM_SCALE)).astype(BF16))
    return heads


def _merge_prompt_kernel(x_ref, an_ref, bn_ref, wo_ref, gmx_ref, wmq_ref, gmq_ref, mk_ref, mv_ref, wmo_ref,
                         y_ref, o_s):
    x1 = x_ref[...] + _dot(an_ref[...], wo_ref[0:A_WIDTH, :]) + _dot(bn_ref[...], wo_ref[A_WIDTH:, :])
    heads = _mem_query(x1, gmx_ref, wmq_ref, gmq_ref)
    lanes = [slice(h * MEM_HEAD_DIM, (h + 1) * MEM_HEAD_DIM) for h in range(MEM_HEADS)]
    scores = [_dot_nt(heads[h], mk_ref[0, :, sl].astype(BF16)) for h, sl in enumerate(lanes)]
    probs = [jnp.exp(s - jnp.max(s, axis=-1, keepdims=True)) for s in scores]
    for p, sl in zip(probs, lanes):
        o = _dot(p.astype(BF16), mv_ref[0, :, sl].astype(BF16)) / jnp.sum(p, axis=-1, keepdims=True)
        o_s[:, sl] = o.astype(BF16)
    y_ref[...] = x1 + _dot(o_s[...], wmo_ref[...])


def _merge_prompt(x, a_n, b_n, mk, mv, w, seq):
    m = x.shape[0]
    tm = TOK_TILE
    per_seq = seq // tm
    mem_len = mk.shape[1]
    row = lambda width: pl.BlockSpec((tm, width), lambda i: (i, 0))
    mem = pl.BlockSpec((1, mem_len, MEM_WIDTH), lambda i: (i // per_seq, 0, 0))
    c = lambda name: _const_spec(w[name].shape)
    return pl.pallas_call(
        _merge_prompt_kernel, grid=(m // tm,),
        in_specs=[row(D_MODEL), row(A_WIDTH), row(B_WIDTH), c('w_o'), c('g_mem_x'), c('w_mq'),
                  c('g_mq'), mem, mem, c('w_mo')],
        out_specs=row(D_MODEL),
        out_shape=jax.ShapeDtypeStruct((m, D_MODEL), F32),
        scratch_shapes=[pltpu.VMEM((tm, MEM_WIDTH), BF16)],
        compiler_params=_params("arbitrary"),
        name="merge_prompt",
    )(x, a_n, b_n, w['w_o'], w['g_mem_x'], w['w_mq'], w['g_mq'], mk, mv, w['w_mo'])


def _merge_sample_kernel(x_ref, an_ref, ctx_ref, wuvb_ref, gob_ref, wo_ref, gmx_ref, wmq_ref, gmq_ref,
                         x1_ref, qm_ref):
    b = _dot(ctx_ref[...].astype(BF16), wuvb_ref[...])
    bn = (_rms(b) * gob_ref[...]).astype(BF16)
    x1 = x_ref[...] + _dot(an_ref[...], wo_ref[0:A_WIDTH, :]) + _dot(bn, wo_ref[A_WIDTH:, :])
    x1_ref[...] = x1
    heads = _mem_query(x1, gmx_ref, wmq_ref, gmq_ref)
    for h in range(MEM_HEADS):
        qm_ref[:, h * MEM_HEAD_DIM:(h + 1) * MEM_HEAD_DIM] = heads[h]


def _merge_sample(x, a_n, ctx, w):
    m = x.shape[0]
    tm = TOK_TILE
    row = lambda width: pl.BlockSpec((tm, width), lambda i: (i, 0))
    c = lambda name: _const_spec(w[name].shape)
    return pl.pallas_call(
        _merge_sample_kernel, grid=(m // tm,),
        in_specs=[row(D_MODEL), row(A_WIDTH), row(ctx.shape[1]), c('w_uv_heads'), c('g_out_b'), c('w_o'),
                  c('g_mem_x'), c('w_mq'), c('g_mq')],
        out_specs=[row(D_MODEL), row(MEM_WIDTH)],
        out_shape=[jax.ShapeDtypeStruct((m, D_MODEL), F32), jax.ShapeDtypeStruct((m, MEM_WIDTH), BF16)],
        compiler_params=_params("arbitrary"),
        name="merge_sample",
    )(x, a_n, ctx, w['w_uv_heads'], w['g_out_b'], w['w_o'], w['g_mem_x'], w['w_mq'], w['g_mq'])


def _memattn_sample_kernel(q_ref, mk_ref, mv_ref, o_ref):
    nrow, nkey = q_ref.shape[1], mk_ref.shape[1]
    s = jnp.einsum('nqd,nkd->nqk', q_ref[...], mk_ref[...].astype(BF16), preferred_element_type=F32)
    same_head = (lax.broadcasted_iota(jnp.int32, (1, nrow, nkey), 1) % MEM_HEADS
                 == lax.broadcasted_iota(jnp.int32, (1, nrow, nkey), 2) % MEM_HEADS)
    s = jnp.where(same_head, s, NEG_INF)
    p = jnp.exp(s - jnp.max(s, axis=-1, keepdims=True))
    o = jnp.einsum('nqk,nkd->nqd', p.astype(BF16), mv_ref[...].astype(BF16), preferred_element_type=F32)
    o_ref[...] = (o / jnp.sum(p, axis=-1, keepdims=True)).astype(BF16)


def _memattn_sample(qm, mk, mv, first_seq):
    n_seq, nrow, _ = qm.shape
    nkey = mk.shape[1]
    ns = MEM_SEQ_TILE
    first_blk = first_seq // ns
    return pl.pallas_call(
        _memattn_sample_kernel, grid=(n_seq // ns,),
        in_specs=[pl.BlockSpec((ns, nrow, MEM_HEAD_DIM), lambda i: (i, 0, 0)),
                  pl.BlockSpec((ns, nkey, MEM_HEAD_DIM), lambda i: (first_blk + i, 0, 0)),
                  pl.BlockSpec((ns, nkey, MEM_HEAD_DIM), lambda i: (first_blk + i, 0, 0))],
        out_specs=pl.BlockSpec((ns, nrow, MEM_HEAD_DIM), lambda i: (i, 0, 0)),
        out_shape=jax.ShapeDtypeStruct((n_seq, nrow, MEM_HEAD_DIM), BF16),
        compiler_params=_params("arbitrary"),
        name="memattn_sample",
    )(qm, mk, mv)


def _ffn_kernel(*refs, sample, tiles_per_seq):
    if sample:
        (x_ref, o_ref, wmo_ref, gffn_ref, wup_ref, wd_ref, wc_ref, bc_ref, p1_ref, p2_ref,
         y_ref, gate_ref, h_s, g_s, act_s) = refs
        x2 = x_ref[...] + _dot(o_ref[...], wmo_ref[...])
        g_s[0:8, :] = jnp.zeros((8, D_FF), F32)
    else:
        (x_ref, gffn_ref, wup_ref, wd_ref, wc_ref, bc_ref, y_ref, tail_ref, h_s, g_s, act_s, carry_s) = refs
        x2 = x_ref[...]

        @pl.when(pl.program_id(0) == 0)
        def _():
            carry_s[...] = jnp.zeros_like(carry_s)
        g_s[0:8, :] = jnp.where((pl.program_id(0) % tiles_per_seq) == 0, 0.0, carry_s[...])
    tm = x_ref.shape[0]
    y_ref[...] = x2
    h_s[...] = (_rms(x2) * gffn_ref[...]).astype(BF16)
    if sample:
        t_in_seq = lax.broadcasted_iota(jnp.int32, (tm, 1), 0) % tiles_per_seq
    for f in range(N_FF_TILES):
        sl = slice(f * FF_TILE, (f + 1) * FF_TILE)
        h = h_s[...]
        g = _dot(h, wup_ref[:, sl])
        val = _dot(h, wup_ref[:, D_FF + f * FF_TILE:D_FF + (f + 1) * FF_TILE])
        g_s[8:, sl] = g
        g1 = g_s[7:7 + tm, sl]
        g2 = g_s[6:6 + tm, sl]
        if sample:
            gate_ref[:, sl] = g
            g1 = jnp.where(t_in_seq >= 1, g1, p1_ref[:, sl])
            g2 = jnp.where(t_in_seq >= 2, g2, p2_ref[:, sl])
        conv = bc_ref[:, sl] + g2 * wc_ref[0:1, sl] + g1 * wc_ref[1:2, sl] + g * wc_ref[2:3, sl]
        act_s[:, sl] = (conv / (1.0 + jnp.exp(-conv)) * val).astype(BF16)
    if not sample:
        tail = g_s[tm:tm + 8, :]
        carry_s[...] = tail
        tail_ref[0] = tail
    y_ref[...] += _dot(act_s[...], wd_ref[...])


def _ffn_scratch(tm):
    return [pltpu.VMEM((tm, D_MODEL), BF16), pltpu.VMEM((tm + 8, D_FF), F32), pltpu.VMEM((tm, D_FF), BF16)]


def _ffn_prompt(x2, w, n_batch, seq):
    m = x2.shape[0]
    tm = TOK_TILE
    per_seq = seq // tm
    row = pl.BlockSpec((tm, D_MODEL), lambda i: (i, 0))
    c = lambda name: _const_spec(w[name].shape)
    return pl.pallas_call(
        functools.partial(_ffn_kernel, sample=False, tiles_per_seq=per_seq), grid=(m // tm,),
        in_specs=[row, c('g_ffn'), c('w_up'), c('w_down'), c('w_conv'), c('b_conv')],
        out_specs=[row, pl.BlockSpec((1, 8, D_FF), lambda i: (i // per_seq, 0, 0))],
        out_shape=[jax.ShapeDtypeStruct((m, D_MODEL), F32), jax.ShapeDtypeStruct((n_batch, 8, D_FF), F32)],
        scratch_shapes=_ffn_scratch(tm) + [pltpu.VMEM((8, D_FF), F32)],
        compiler_params=_params("arbitrary"),
        name="ffn_prompt",
    )(x2, w['g_ffn'], w['w_up'], w['w_down'], w['w_conv'], w['b_conv'])


def _ffn_sample(x1, o, prev1, prev2, w, t_q):
    m = x1.shape[0]
    tm = TOK_TILE
    row = lambda width: pl.BlockSpec((tm, width), lambda i: (i, 0))
    c = lambda name: _const_spec(w[name].shape)
    return pl.pallas_call(
        functools.partial(_ffn_kernel, sample=True, tiles_per_seq=t_q), grid=(m // tm,),
        in_specs=[row(D_MODEL), row(MEM_WIDTH), c('w_mo'), c('g_ffn'), c('w_up'), c('w_down'),
                  c('w_conv'), c('b_conv'), row(D_FF), row(D_FF)],
        out_specs=[row(D_MODEL), row(D_FF)],
        out_shape=[jax.ShapeDtypeStruct((m, D_MODEL), F32), jax.ShapeDtypeStruct((m, D_FF), F32)],
        scratch_shapes=_ffn_scratch(tm),
        compiler_params=_params("arbitrary"),
        name="ffn_sample",
    )(x1, o, w['w_mo'], w['g_ffn'], w['w_up'], w['w_down'], w['w_conv'], w['b_conv'], prev1, prev2)


def _head_pad(wm, width):
    k = wm.shape[0]
    wm = wm.reshape(k, MLA_HEADS, width)
    return jnp.pad(wm, ((0, 0), (0, 0), (0, HEAD_PAD - width))).reshape(k, MLA_HEADS * HEAD_PAD)


def _rot_half_cols(wm):
    half = QK_ROPE // 2
    return jnp.concatenate([-wm[..., half:], wm[..., :half]], axis=-1)


def _lane_vec(nope, rope):
    return jnp.concatenate([nope, rope, jnp.zeros((LANES - QK_HEAD,), F32)])[None, :]


def _rope_tables(pos):
    half = QK_ROPE // 2
    inv_freq = ROPE_THETA ** (-jnp.arange(half, dtype=F32) / half)
    ang = pos.astype(F32)[:, None] * inv_freq[None, :]
    return jnp.cos(ang), jnp.sin(ang)


def _head_tile_tables(pos):
    cos, sin = _rope_tables(pos)
    n = pos.shape[0]
    cos_t = jnp.concatenate([jnp.ones((n, QK_NOPE), F32), cos, cos, jnp.zeros((n, LANES - QK_HEAD), F32)], axis=1)
    sin_t = jnp.concatenate([jnp.zeros((n, QK_NOPE), F32), sin, sin, jnp.zeros((n, LANES - QK_HEAD), F32)], axis=1)
    return cos_t, sin_t


def _feature_table(pos, cols):
    cos, sin = _rope_tables(pos)
    t = jnp.concatenate([cos, cos, sin, sin], axis=1).T
    return jnp.pad(t, ((0, 0), (0, cols - pos.shape[0])))


def _query_feature_perm():
    half = QK_ROPE // 2
    p = np.zeros((LANES, LANES), np.float32)
    for j in range(QK_ROPE):
        p[QK_NOPE + j, j] = 1.0
    for j in range(half):
        p[QK_NOPE + half + j, QK_ROPE + j] = 1.0
        p[QK_NOPE + j, QK_ROPE + half + j] = -1.0
    return jnp.asarray(p, BF16)


def _prep_weights(l, t_q, g_mix, w_in, ln_v_g, ln_v_b, w_s, b_s, g_q_a, w_uq, g_kv_a, w_uk, w_uv, g_qk_q, g_qk_k,
                  g_out_a, g_out_b, w_o, g_mem_x, g_mem_in, w_mq, w_mk, w_mv, g_mq, g_mk, w_mo, g_ffn, w_up,
                  w_conv, b_conv, w_down):
    half = QK_ROPE // 2
    o_kpe = 2 * A_WIDTH + Q_LORA + KV_LORA
    vec = lambda a: a.reshape(1, -1).astype(F32)
    w = {}
    wkpe = w_in[l][:, o_kpe:]
    lane_tile = lambda blk: jnp.pad(blk, ((0, 0), (QK_NOPE, LANES - QK_HEAD)))
    w['w_in'] = jnp.concatenate([w_in[l][:, :o_kpe], lane_tile(wkpe), lane_tile(_rot_half_cols(wkpe))],
                                axis=1).astype(BF16)
    w['g_mix'] = vec(g_mix[l])
    w['ln_g'] = vec(ln_v_g[l])
    w['ln_b'] = vec(ln_v_b[l])
    w['g_q_a'] = vec(g_q_a[l])
    w['g_kv_a'] = vec(g_kv_a[l])
    w['g_out_a'] = vec(g_out_a[l])
    w['g_out_b'] = vec(g_out_b[l])
    w['g_out_b_col'] = g_out_b[l][:, None]
    wq = w_uq[l].reshape(Q_LORA, MLA_HEADS, QK_HEAD)
    wq_rot = jnp.concatenate([jnp.zeros((Q_LORA, MLA_HEADS, QK_NOPE), F32), _rot_half_cols(wq[..., QK_NOPE:])], axis=-1)
    w['wq'] = _head_pad(wq.reshape(Q_LORA, -1), QK_HEAD).astype(BF16)
    w['wq_rot'] = _head_pad(wq_rot.reshape(Q_LORA, -1), QK_HEAD).astype(BF16)
    w['wk'] = _head_pad(w_uk[l], QK_NOPE).astype(BF16)
    w['wvt'] = w_uv[l].T.astype(BF16)
    gq, gk = g_qk_q[l], g_qk_k[l]
    sw = lambda g: jnp.concatenate([g[QK_NOPE + half:], g[QK_NOPE:QK_NOPE + half]])
    zeros_n = jnp.zeros((QK_NOPE,), F32)
    w['gq'] = _lane_vec(gq[:QK_NOPE], gq[QK_NOPE:])
    w['gq_sw'] = _lane_vec(zeros_n, sw(gq))
    w['gk'] = _lane_vec(gk[:QK_NOPE], gk[QK_NOPE:])
    w['gk_sw'] = _lane_vec(zeros_n, sw(gk))
    w['gk_nope'] = _lane_vec(gk[:QK_NOPE], jnp.zeros((QK_ROPE,), F32))
    w['pq'] = _query_feature_perm()
    w['wukt'] = w_uk[l].T.astype(BF16)
    gr = gk[QK_NOPE:]
    w['gcol'] = jnp.concatenate([gr, gr])[:, None]
    wuv = w_uv[l].reshape(KV_LORA, MLA_HEADS, V_HEAD)
    eye = jnp.eye(MLA_HEADS, dtype=F32)
    w['w_uv_heads'] = (wuv[None, :, :, :] * eye[:, None, :, None]).reshape(MLA_HEADS * KV_LORA, B_WIDTH).astype(BF16)
    tril = jnp.tril(jnp.ones((CHUNK, CHUNK), F32))
    w['ws_prompt'] = (w_s[l] * tril).astype(BF16)
    w['bs_prompt'] = jnp.repeat(b_s[l].T, A_GROUP_DIM, axis=1)
    reps = CHUNK // t_q
    blk = w_s[l][:, :t_q, :t_q] * jnp.tril(jnp.ones((t_q, t_q), F32))
    w['ws_sample'] = jnp.einsum('ij,gts->gitjs', jnp.eye(reps, dtype=F32), blk).reshape(A_GROUPS, CHUNK, CHUNK).astype(BF16)
    w['bs_sample'] = jnp.repeat(jnp.tile(b_s[l][:, :t_q].T, (reps, 1)), A_GROUP_DIM, axis=1)
    w['w_o'] = w_o[l].astype(BF16)
    w['g_mem_x'] = vec(g_mem_x[l])
    w['g_mem_in'] = vec(g_mem_in[l])
    w['w_mq'] = w_mq[l].astype(BF16)
    w['w_mk'] = w_mk[l].astype(BF16)
    w['w_mv'] = w_mv[l].astype(BF16)
    w['g_mq'] = vec(g_mq[l])
    w['g_mk'] = vec(g_mk[l])
    w['w_mo'] = w_mo[l].astype(BF16)
    w['g_ffn'] = vec(g_ffn[l])
    w['w_up'] = w_up[l].astype(BF16)
    w['w_down'] = w_down[l].astype(BF16)
    w['w_conv'] = w_conv[l]
    w['b_conv'] = vec(b_conv[l])
    return w


def kernel(x_prompt, x_sample, cache_ckv, cache_kpe, cache_mem_k, cache_mem_v, state_ffn_conv, page_table, mem_prompt, g_mix, w_in, ln_v_g, ln_v_b, w_s, b_s, g_q_a, w_uq, g_kv_a, w_uk, w_uv, g_qk_q, g_qk_k, g_out_a, g_out_b, w_o, g_mem_x, g_mem_in, w_mq, w_mk, w_mv, g_mq, g_mk, w_mo, g_ffn, w_up, w_conv, b_conv, w_down):
    n_p, seq, _ = x_prompt.shape
    n_s, t_q, _ = x_sample.shape
    depth = g_mix.shape[0]
    mem_len = mem_prompt.shape[1]
    n_pages = page_table.shape[1]
    past = n_pages * PAGE_SIZE

    pos_p = jnp.arange(seq)
    pos_s = past + jnp.arange(t_q)
    cos_p, sin_p = _head_tile_tables(pos_p)
    cos_s, sin_s = _head_tile_tables(jnp.tile(pos_s, TOK_TILE // t_q))
    n_kv_tiles = past // KV_TILE
    t_past = _feature_table(jnp.arange(past), past).reshape(2 * QK_ROPE, n_kv_tiles, KV_TILE).transpose(1, 0, 2)
    t_new = _feature_table(pos_s, LANES)
    n_phys = cache_ckv.shape[1]
    ckv_pages = cache_ckv.reshape(depth * n_phys, PAGE_SIZE, KV_LORA)
    kpe_pages_t = jnp.swapaxes(cache_kpe, -1, -2).reshape(depth * n_phys, QK_ROPE, PAGE_SIZE)
    mem_rows = mem_len * MEM_HEADS
    mem_k_rows = cache_mem_k.reshape(depth * n_s, mem_rows, MEM_HEAD_DIM)
    mem_v_rows = cache_mem_v.reshape(depth * n_s, mem_rows, MEM_HEAD_DIM)

    xp = x_prompt.reshape(n_p * seq, D_MODEL)
    xs = x_sample.reshape(n_s * t_q, D_MODEL)
    outs = {k: [] for k in ('p_ckv', 'p_kpe', 'p_mk', 'p_mv', 'p_conv', 's_ckv', 's_kpe', 's_chunk_v', 's_conv')}
    for l in range(depth):
        w = _prep_weights(l, t_q, g_mix, w_in, ln_v_g, ln_v_b, w_s, b_s, g_q_a, w_uq, g_kv_a, w_uk, w_uv, g_qk_q,
                          g_qk_k, g_out_a, g_out_b, w_o, g_mem_x, g_mem_in, w_mq, w_mk, w_mv, g_mq, g_mk, w_mo,
                          g_ffn, w_up, w_conv, b_conv, w_down)
        w['t_past'], w['t_new'] = t_past, t_new

        wp = dict(w, ws=w['ws_prompt'], bs=w['bs_prompt'])
        q, k, vt, ckvn, kpe, a_n = _inproj(xp, cos_p, sin_p, wp, sample=False)
        b_n = _flash(q, k, vt, w['g_out_b_col'], n_p, seq)
        mk, mv = _memkv(mem_prompt.reshape(n_p * mem_len, D_MODEL), w)
        mk3 = mk.reshape(n_p, mem_len, MEM_WIDTH)
        mv3 = mv.reshape(n_p, mem_len, MEM_WIDTH)
        x2 = _merge_prompt(xp, a_n, b_n, mk3, mv3, w, seq)
        xp, tail = _ffn_prompt(x2, w, n_p, seq)
        outs['p_ckv'].append(ckvn.reshape(n_p, seq, KV_LORA))
        outs['p_kpe'].append(kpe.reshape(n_p, seq, QK_ROPE))
        outs['p_mk'].append(mk.reshape(n_p, mem_len, MEM_HEADS, MEM_HEAD_DIM))
        outs['p_mv'].append(mv.reshape(n_p, mem_len, MEM_HEADS, MEM_HEAD_DIM))
        outs['p_conv'].append(tail[:, 8 - (CONV_W - 1):, :])

        wsm = dict(w, ws=w['ws_sample'], bs=w['bs_sample'])
        qlat, qrf, ckvn_s, kpe_s, a_ns, vg = _inproj(xs, cos_s, sin_s, wsm, sample=True)
        nq = t_q * MLA_HEADS
        kpnew_t = jnp.pad(jnp.swapaxes(kpe_s.reshape(n_s, t_q, QK_ROPE), 1, 2), ((0, 0), (0, 0), (0, LANES - t_q)))
        ctx = _paged(page_table + l * n_phys, qlat.reshape(n_s, nq, KV_LORA), qrf.reshape(n_s, nq, LANES),
                     ckvn_s.reshape(n_s, t_q, KV_LORA), kpnew_t, ckv_pages, kpe_pages_t, w)
        x1, qm = _merge_sample(xs, a_ns, ctx.reshape(n_s * t_q, MLA_HEADS * KV_LORA), w)
        o = _memattn_sample(qm.reshape(n_s, t_q * MEM_HEADS, MEM_HEAD_DIM), mem_k_rows, mem_v_rows, l * n_s)
        st = state_ffn_conv[l]
        zero = jnp.zeros_like(st[:, :1])
        prev1 = jnp.concatenate([st[:, 1:2]] + [zero] * (t_q - 1), axis=1)
        prev2 = jnp.concatenate([st[:, 0:1], st[:, 1:2]] + [zero] * (t_q - 2), axis=1)
        flat = lambda a: a.reshape(n_s * t_q, D_FF)
        xs, gate = _ffn_sample(x1, o.reshape(n_s * t_q, MEM_WIDTH), flat(prev1), flat(prev2), w, t_q)
        gate = gate.reshape(n_s, t_q, D_FF)
        outs['s_ckv'].append(ckvn_s.reshape(n_s, t_q, KV_LORA))
        outs['s_kpe'].append(kpe_s.reshape(n_s, t_q, QK_ROPE))
        outs['s_chunk_v'].append(vg.reshape(n_s, t_q, A_GROUPS, A_GROUP_DIM))
        outs['s_conv'].append(gate[:, t_q - (CONV_W - 1):, :])

    return (xp.reshape(n_p, seq, D_MODEL), xs.reshape(n_s, t_q, D_MODEL),
            jnp.stack(outs['p_ckv']), jnp.stack(outs['p_kpe']), jnp.stack(outs['p_mk']), jnp.stack(outs['p_mv']),
            jnp.stack(outs['p_conv']), jnp.stack(outs['s_ckv']), jnp.stack(outs['s_kpe']),
            jnp.stack(outs['s_chunk_v']), jnp.stack(outs['s_conv']))
```

```python
import functools

import numpy as np
import jax
import jax.numpy as jnp
from jax import lax
from jax.experimental import pallas as pl
from jax.experimental.pallas import tpu as pltpu

D_MODEL = 1024
A_GROUPS = 8
A_GROUP_DIM = 64
A_WIDTH = A_GROUPS * A_GROUP_DIM
CHUNK = 128
MLA_HEADS = 8
Q_LORA = 256
KV_LORA = 256
QK_NOPE = 64
QK_ROPE = 32
QK_HEAD = QK_NOPE + QK_ROPE
V_HEAD = 64
B_WIDTH = MLA_HEADS * V_HEAD
ROPE_THETA = 10000.0
PAGE_SIZE = 128
MEM_HEADS = 4
MEM_HEAD_DIM = 128
MEM_WIDTH = MEM_HEADS * MEM_HEAD_DIM
D_FF = 2816
CONV_W = 3
EPS = 1e-6
NEG_INF = -1e30

LANES = 128
HEAD_PAD = LANES
QK_SCALE = QK_HEAD ** -0.5
LOG2E = 1.4426950408889634
V_AUG = V_HEAD + 16
MEM_SCALE = MEM_HEAD_DIM ** -0.5

TOK_TILE = 512
FF_TILE = 256
N_FF_TILES = D_FF // FF_TILE
ATT_TILE = 256
KV_TILE = 1024
MEM_SEQ_TILE = 8
VMEM_LIMIT = 56 * 1024 * 1024

F32 = jnp.float32
BF16 = jnp.bfloat16


def _dot(a, b):
    return jnp.dot(a, b, preferred_element_type=F32)


def _dot_nt(a, b):
    return lax.dot_general(a, b, (((1,), (1,)), ((), ())), preferred_element_type=F32)


def _rms(x):
    return x * lax.rsqrt(jnp.mean(x * x, axis=-1, keepdims=True) + EPS)


def _gelu(x):
    return jax.nn.gelu(x, approximate=True)


def _const_spec(shape):
    nd = len(shape)
    return pl.BlockSpec(shape, lambda *_: (0,) * nd, pipeline_mode=pl.Buffered(1))


def _params(*sem):
    return pltpu.CompilerParams(dimension_semantics=sem, vmem_limit_bytes=VMEM_LIMIT)


def _inproj_kernel(x_ref, gmix_ref, win_ref, lng_ref, lnb_ref, ws_ref, bs_ref, gqa_ref, wq_ref, wqr_ref,
                   gkva_ref, wk_ref, wv_ref, gq_ref, gqs_ref, gk_ref, gks_ref, cos_ref, sin_ref, goa_ref,
                   pq_ref, *rest, sample):
    if sample:
        qlat_ref, qrf_ref, ckvn_ref, kpe_ref, an_ref, vg_ref, zu_s, vgb_s, a_s = rest
    else:
        q_ref, k_ref, v_ref, ckvn_ref, kpe_ref, an_ref, zu_s, vgb_s, a_s = rest
    tm = x_ref.shape[0]
    xn = (_rms(x_ref[...]) * gmix_ref[...]).astype(BF16)

    o_cq = 2 * A_WIDTH
    o_ckv = o_cq + Q_LORA
    o_kpe = o_ckv + KV_LORA
    zv = _dot(xn, win_ref[:, A_WIDTH:2 * A_WIDTH])
    cq = _dot(xn, win_ref[:, o_cq:o_ckv])
    ckv = _dot(xn, win_ref[:, o_ckv:o_kpe])
    kpt = _dot(xn, win_ref[:, o_kpe:o_kpe + LANES])
    if not sample:
        kpr = _dot(xn, win_ref[:, o_kpe + LANES:o_kpe + 2 * LANES])
    zu_s[...] = _dot(xn, win_ref[:, 0:A_WIDTH])

    cqn = (_rms(cq) * gqa_ref[...]).astype(BF16)
    ckvn = _rms(ckv) * gkva_ref[...]
    ckvn_ref[...] = ckvn
    kpe_ref[...] = kpt[:, QK_NOPE:QK_HEAD]
    qall = _dot(cqn, wq_ref[...])
    qrall = _dot(cqn, wqr_ref[...])
    if not sample:
        cb = ckvn.astype(BF16)
        kall = _dot(cb, wk_ref[...])
        vt = _dot_nt(wv_ref[...], cb).astype(BF16)
        ones = jnp.ones((V_AUG - V_HEAD, ATT_TILE), BF16)
        for c in range(tm // ATT_TILE):
            cols = slice(c * ATT_TILE, (c + 1) * ATT_TILE)
            for h in range(MLA_HEADS):
                v_ref[c, h * V_AUG:h * V_AUG + V_HEAD, :] = vt[h * V_HEAD:(h + 1) * V_HEAD, cols]
                v_ref[c, h * V_AUG + V_HEAD:(h + 1) * V_AUG, :] = ones

    gv = _gelu(zv)
    low = lax.broadcasted_iota(jnp.int32, (1, LANES), 1) < A_GROUP_DIM
    inv_g = 1.0 / A_GROUP_DIM
    for p in range(A_WIDTH // LANES):
        sl = slice(p * LANES, (p + 1) * LANES)
        t = gv[:, sl]
        s_lo = jnp.sum(jnp.where(low, t, 0.0), axis=-1, keepdims=True)
        s_hi = jnp.sum(jnp.where(low, 0.0, t), axis=-1, keepdims=True)
        tc = t - jnp.where(low, s_lo, s_hi) * inv_g
        tc2 = tc * tc
        v_lo = jnp.sum(jnp.where(low, tc2, 0.0), axis=-1, keepdims=True)
        v_hi = jnp.sum(jnp.where(low, 0.0, tc2), axis=-1, keepdims=True)
        y = tc * lax.rsqrt(jnp.where(low, v_lo, v_hi) * inv_g + EPS) * lng_ref[:, sl] + lnb_ref[:, sl]
        if sample:
            vg_ref[:, sl] = y
        vgb_s[:, sl] = y.astype(BF16)
    for c in range(tm // CHUNK):
        rows = slice(c * CHUNK, (c + 1) * CHUNK)
        for p in range(A_WIDTH // LANES):
            sl = slice(p * LANES, (p + 1) * LANES)
            vp = vgb_s[rows, sl]
            mixed = jnp.where(low, _dot(ws_ref[2 * p], vp), _dot(ws_ref[2 * p + 1], vp)) + bs_ref[:, sl]
            a_s[rows, sl] = _gelu(zu_s[rows, sl]) * mixed
    an_ref[...] = (_rms(a_s[...]) * goa_ref[...]).astype(BF16)

    cosv = cos_ref[...]
    sinv = sin_ref[...]
    q_scale = QK_SCALE if sample else QK_SCALE * LOG2E
    qa = gq_ref[...] * cosv * q_scale
    qb = gqs_ref[...] * sinv * q_scale
    inv_d = 1.0 / QK_HEAD
    if not sample:
        ka = gk_ref[...] * cosv
        kb = gks_ref[...] * sinv
    for h in range(MLA_HEADS):
        sl = slice(h * HEAD_PAD, (h + 1) * HEAD_PAD)
        qh = qall[:, sl]
        r = lax.rsqrt(jnp.sum(qh * qh, axis=-1, keepdims=True) * inv_d + EPS)
        qf = r * (qh * qa + qrall[:, sl] * qb)
        if sample:
            qg = (qf * gk_ref[...]).astype(BF16)
            qlat_ref[:, h * KV_LORA:(h + 1) * KV_LORA] = _dot_nt(qg, wk_ref[:, sl]).astype(BF16)
            qrf_ref[:, sl] = _dot(qf.astype(BF16), pq_ref[...]).astype(BF16)
        else:
            q_ref[:, sl] = qf.astype(BF16)
            kh = kall[:, sl] + kpt
            rk = lax.rsqrt(jnp.sum(kh * kh, axis=-1, keepdims=True) * inv_d + EPS)
            k_ref[:, sl] = (rk * (kh * ka + kpr * kb)).astype(BF16)


def _inproj(x, cos, sin, w, *, sample):
    m = x.shape[0]
    tm = TOK_TILE
    n_tiles = m // tm
    n_pos_tiles = cos.shape[0] // tm
    row = lambda width: pl.BlockSpec((tm, width), lambda i: (i, 0))
    pos = pl.BlockSpec((tm, LANES), lambda i: (i % n_pos_tiles, 0))
    consts = [w['g_mix'], w['w_in'], w['ln_g'], w['ln_b'], w['ws'], w['bs'], w['g_q_a'], w['wq'], w['wq_rot'],
              w['g_kv_a'], w['wk'], w['wvt'], w['gq'], w['gq_sw'], w['gk_nope'] if sample else w['gk'], w['gk_sw']]
    in_specs = ([row(D_MODEL)] + [_const_spec(c.shape) for c in consts] + [pos, pos]
                + [_const_spec(w['g_out_a'].shape), _const_spec(w['pq'].shape)])
    args = [x] + consts + [cos, sin, w['g_out_a'], w['pq']]
    wide = MLA_HEADS * HEAD_PAD
    if sample:
        out_shape = [jax.ShapeDtypeStruct((m, MLA_HEADS * KV_LORA), BF16), jax.ShapeDtypeStruct((m, wide), BF16),
                     jax.ShapeDtypeStruct((m, KV_LORA), F32), jax.ShapeDtypeStruct((m, QK_ROPE), F32),
                     jax.ShapeDtypeStruct((m, A_WIDTH), BF16), jax.ShapeDtypeStruct((m, A_WIDTH), F32)]
        out_specs = [row(MLA_HEADS * KV_LORA), row(wide), row(KV_LORA), row(QK_ROPE), row(A_WIDTH), row(A_WIDTH)]
    else:
        att_tiles = tm // ATT_TILE
        out_shape = [jax.ShapeDtypeStruct((m, wide), BF16)] * 2 + [
            jax.ShapeDtypeStruct((m // ATT_TILE, MLA_HEADS * V_AUG, ATT_TILE), BF16),
            jax.ShapeDtypeStruct((m, KV_LORA), F32), jax.ShapeDtypeStruct((m, QK_ROPE), F32),
            jax.ShapeDtypeStruct((m, A_WIDTH), BF16)]
        out_specs = [row(wide)] * 2 + [pl.BlockSpec((att_tiles, MLA_HEADS * V_AUG, ATT_TILE), lambda i: (i, 0, 0)),
                                       row(KV_LORA), row(QK_ROPE), row(A_WIDTH)]
    return pl.pallas_call(
        functools.partial(_inproj_kernel, sample=sample),
        grid=(n_tiles,), in_specs=in_specs, out_specs=out_specs, out_shape=out_shape,
        scratch_shapes=[pltpu.VMEM((tm, A_WIDTH), F32), pltpu.VMEM((tm, A_WIDTH), BF16),
                        pltpu.VMEM((tm, A_WIDTH), F32)],
        compiler_params=_params("arbitrary"),
        name="inproj_sample" if sample else "inproj_prompt",
    )(*args)


def _flash_kernel(q_ref, k_ref, vt_ref, gob_ref, o_ref):
    i = pl.program_id(1)
    t = ATT_TILE
    causal = lax.broadcasted_iota(jnp.int32, (t, t), 0) <= lax.broadcasted_iota(jnp.int32, (t, t), 1)

    def step(j, carry, masked):
        start = pl.multiple_of(j * t, t)
        vt = vt_ref[j]
        heads = [slice(h * HEAD_PAD, (h + 1) * HEAD_PAD) for h in range(MLA_HEADS)]
        scores = [_dot_nt(k_ref[pl.ds(start, t), sl], q_ref[:, sl]) for sl in heads]
        stats = []
        for h in range(MLA_HEADS):
            m = carry[h][0]
            s = jnp.where(causal, scores[h], NEG_INF) if masked else scores[h]
            m_new = jnp.maximum(m, jnp.max(s, axis=0, keepdims=True))
            stats.append((m_new, jnp.exp2(m - m_new), jnp.exp2(s - m_new).astype(BF16)))
        new = []
        for h, (m_new, alpha, p) in enumerate(stats):
            pv = _dot(vt[h * V_AUG:(h + 1) * V_AUG, :], p)
            _, l, acc = carry[h]
            new.append((m_new, alpha * l + pv[V_HEAD:V_HEAD + 1], alpha * acc + pv[:V_HEAD]))
        return tuple(new)

    init = tuple((jnp.full((1, t), NEG_INF, F32), jnp.zeros((1, t), F32), jnp.zeros((V_HEAD, t), F32))
                 for _ in range(MLA_HEADS))
    carry = lax.fori_loop(0, i, functools.partial(step, masked=False), init)
    carry = step(i, carry, True)
    o_t = jnp.concatenate([acc / l for _, l, acc in carry], axis=0)
    bn_t = o_t * lax.rsqrt(jnp.mean(o_t * o_t, axis=0, keepdims=True) + EPS) * gob_ref[...]
    o_ref[...] = bn_t.T.astype(BF16)


def _flash(q, k, vt, g_out_b_col, n_batch, seq):
    m = q.shape[0]
    nq = seq // ATT_TILE
    wide = MLA_HEADS * HEAD_PAD
    return pl.pallas_call(
        _flash_kernel,
        grid=(n_batch, nq),
        in_specs=[pl.BlockSpec((ATT_TILE, wide), lambda b, i: (b * nq + i, 0)),
                  pl.BlockSpec((seq, wide), lambda b, i: (b, 0)),
                  pl.BlockSpec((nq, MLA_HEADS * V_AUG, ATT_TILE), lambda b, i: (b, 0, 0)),
                  pl.BlockSpec((B_WIDTH, 1), lambda b, i: (0, 0))],
        out_specs=pl.BlockSpec((ATT_TILE, B_WIDTH), lambda b, i: (b * nq + i, 0)),
        out_shape=jax.ShapeDtypeStruct((m, B_WIDTH), BF16),
        compiler_params=_params("arbitrary", "arbitrary"),
        name="flash_prompt",
    )(q, k, vt, g_out_b_col)


def _paged_kernel(pt_ref, qlat_ref, qrf_ref, qlat_next_ref, qrf_next_ref, cnew_ref, kpnew_ref, wukt_ref, gcol_ref,
                  tpast_ref, tnew_ref, ckv_hbm, kpe_hbm, ctx_ref, cbuf, pbuf, cn_s, tg_s, s0_s, sems, *, n_seq, n_pages):
    n = pl.program_id(0)
    slot = n % 2
    ppt = KV_TILE // PAGE_SIZE
    n_tiles = n_pages // ppt

    def start_fetch(seq, sl):
        def body(p, _):
            page = pt_ref[seq, p]
            pltpu.make_async_copy(ckv_hbm.at[page], cbuf.at[sl, p], sems.at[0, sl]).start()
            pltpu.make_async_copy(kpe_hbm.at[page], pbuf.at[sl, p], sems.at[1, sl]).start()
            return 0
        lax.fori_loop(0, n_pages, body, 0, unroll=8)

    def wait_fetch(sl):
        pltpu.make_async_copy(ckv_hbm.at[pl.ds(0, n_pages)], cbuf.at[sl], sems.at[0, sl]).wait()
        pltpu.make_async_copy(kpe_hbm.at[pl.ds(0, n_pages)], pbuf.at[sl], sems.at[1, sl]).wait()

    nq = qlat_ref.shape[1]
    t_q = nq // MLA_HEADS
    n_nope = MLA_HEADS * QK_NOPE
    n_feat = 3 * QK_ROPE
    ones_rows = jnp.where(lax.broadcasted_iota(jnp.int32, (16, n_feat), 1) >= 2 * QK_ROPE, 1.0, 0.0).astype(BF16)

    def query_mats(ql_ref, qr_ref):
        a_mat = jnp.concatenate([wukt_ref[...], ql_ref[0]], axis=0)
        l_mat = jnp.concatenate([qr_ref[0][:, :n_feat], ones_rows], axis=0)
        return a_mat, l_mat

    def scores(mats, cb, kpt, tg):
        a_mat, l_mat = mats
        tk = cb.shape[0]
        m1 = _dot_nt(a_mat, cb)
        knt = m1[:n_nope]
        ssq = jnp.sum((knt * knt).reshape(MLA_HEADS, QK_NOPE, tk), axis=1)
        feat = jnp.concatenate([kpt * tg[:QK_ROPE], kpt * tg[QK_ROPE:], kpt * kpt], axis=0)
        m2 = _dot(l_mat, feat.astype(BF16))
        r = lax.rsqrt((ssq + m2[nq:nq + MLA_HEADS]) * (1.0 / QK_HEAD) + EPS)
        s = (m1[n_nope:] + m2[:nq]).reshape(t_q, MLA_HEADS, tk) * r[None]
        return s.reshape(nq, tk)

    def update(carry, s, cb):
        m, l, acc = carry
        m_new = jnp.maximum(m, jnp.max(s, axis=-1, keepdims=True))
        alpha = jnp.exp(m - m_new)
        p = jnp.exp(s - m_new)
        l = alpha * l + jnp.sum(p, axis=-1, keepdims=True)
        acc = alpha * acc + _dot(p.astype(BF16), cb)
        return m_new, l, acc

    def tile_scores(mats, sl, t):
        cb = cbuf[sl, t * ppt:(t + 1) * ppt].reshape(KV_TILE, KV_LORA).astype(BF16)
        kpt = jnp.concatenate([pbuf[sl, t * ppt + i] for i in range(ppt)], axis=1)
        return scores(mats, cb, kpt, tg_s[t]), cb

    @pl.when(n == 0)
    def _():
        cn_s[...] = jnp.zeros_like(cn_s)
        for t in range(n_tiles):
            tg_s[t] = tpast_ref[t] * gcol_ref[...]
        start_fetch(0, 0)

    has_next = n + 1 < n_seq

    @pl.when(has_next)
    def _():
        start_fetch(n + 1, 1 - slot)

    @pl.when(n == 0)
    def _():
        wait_fetch(0)
        s0_s[...] = tile_scores(query_mats(qlat_ref, qrf_ref), 0, 0)[0]

    cur = query_mats(qlat_ref, qrf_ref)
    nslot = jnp.where(has_next, 1 - slot, slot)
    s_cur = s0_s[...]
    cb_cur = cbuf[slot, 0:ppt].reshape(KV_TILE, KV_LORA).astype(BF16)
    stats = (jnp.full((nq, 1), NEG_INF, F32), jnp.zeros((nq, 1), F32), jnp.zeros((nq, KV_LORA), F32))
    for t in range(n_tiles):
        if t + 1 < n_tiles:
            s_next, cb_next = tile_scores(cur, slot, t + 1)
        else:
            @pl.when(has_next)
            def _():
                wait_fetch(1 - slot)
            s_next, cb_next = tile_scores(query_mats(qlat_next_ref, qrf_next_ref), nslot, 0)
        stats = update(stats, s_cur, cb_cur)
        s_cur, cb_cur = s_next, cb_next
    s0_s[...] = s_cur

    cn_s[0:t_q, :] = cnew_ref[0]
    cb_new = cn_s[...].astype(BF16)
    s_new = scores(cur, cb_new, kpnew_ref[0], tnew_ref[...] * gcol_ref[...])
    key = lax.broadcasted_iota(jnp.int32, (nq, LANES), 1)
    qry = lax.broadcasted_iota(jnp.int32, (nq, LANES), 0) // MLA_HEADS
    m, l, acc = update(stats, jnp.where(key <= qry, s_new, NEG_INF), cb_new)
    ctx_ref[0] = acc / l


def _paged(page_table, qlat, qrf, cnew, kpnew_t, cache_ckv, cache_kpe_t, w):
    n_seq, n_pages = page_table.shape
    nq = qlat.shape[1]
    t_q = cnew.shape[1]
    seq_spec = lambda shape: pl.BlockSpec((1,) + shape, lambda n, pt: (n, 0, 0))
    next_spec = lambda shape: pl.BlockSpec((1,) + shape, lambda n, pt: (jnp.minimum(n + 1, n_seq - 1), 0, 0))
    const = lambda shape: pl.BlockSpec(shape, lambda n, pt: (0,) * len(shape), pipeline_mode=pl.Buffered(1))
    grid_spec = pltpu.PrefetchScalarGridSpec(
        num_scalar_prefetch=1, grid=(n_seq,),
        in_specs=[seq_spec((nq, KV_LORA)), seq_spec((nq, LANES)), next_spec((nq, KV_LORA)), next_spec((nq, LANES)),
                  seq_spec((t_q, KV_LORA)), seq_spec((QK_ROPE, LANES)),
                  const(w['wukt'].shape), const(w['gcol'].shape), const(w['t_past'].shape), const(w['t_new'].shape),
                  pl.BlockSpec(memory_space=pl.ANY), pl.BlockSpec(memory_space=pl.ANY)],
        out_specs=seq_spec((nq, KV_LORA)),
        scratch_shapes=[pltpu.VMEM((2, n_pages, PAGE_SIZE, KV_LORA), F32),
                        pltpu.VMEM((2, n_pages, QK_ROPE, PAGE_SIZE), F32),
                        pltpu.VMEM((LANES, KV_LORA), F32), pltpu.VMEM(w['t_past'].shape, F32),
                        pltpu.VMEM((nq, KV_TILE), F32), pltpu.SemaphoreType.DMA((2, 2))])
    return pl.pallas_call(
        functools.partial(_paged_kernel, n_seq=n_seq, n_pages=n_pages),
        grid_spec=grid_spec,
        out_shape=jax.ShapeDtypeStruct((n_seq, nq, KV_LORA), F32),
        compiler_params=_params("arbitrary"),
        name="paged_sample",
    )(page_table, qlat, qrf, qlat, qrf, cnew, kpnew_t, w['wukt'], w['gcol'], w['t_past'], w['t_new'],
      cache_ckv, cache_kpe_t)


def _memkv_kernel(mem_ref, gin_ref, wmk_ref, wmv_ref, gmk_ref, k_ref, v_ref):
    hm = (_rms(mem_ref[...]) * gin_ref[...]).astype(BF16)
    kk = _dot(hm, wmk_ref[...])
    for h in range(MEM_HEADS):
        sl = slice(h * MEM_HEAD_DIM, (h + 1) * MEM_HEAD_DIM)
        k_ref[:, sl] = _rms(kk[:, sl]) * gmk_ref[...]
    v_ref[...] = _dot(hm, wmv_ref[...])


def _memkv(mem, w):
    m = mem.shape[0]
    tm = TOK_TILE
    row = lambda width: pl.BlockSpec((tm, width), lambda i: (i, 0))
    consts = [w['g_mem_in'], w['w_mk'], w['w_mv'], w['g_mk']]
    return pl.pallas_call(
        _memkv_kernel, grid=(m // tm,),
        in_specs=[row(D_MODEL)] + [_const_spec(c.shape) for c in consts],
        out_specs=[row(MEM_WIDTH), row(MEM_WIDTH)],
        out_shape=[jax.ShapeDtypeStruct((m, MEM_WIDTH), F32)] * 2,
        compiler_params=_params("arbitrary"),
        name="memkv_prompt",
    )(mem, *consts)


def _mem_query(x1, gmx_ref, wmq_ref, gmq_ref):
    qm = _dot((_rms(x1) * gmx_ref[...]).astype(BF16), wmq_ref[...])
    heads = []
    for h in range(MEM_HEADS):
        sl = slice(h * MEM_HEAD_DIM, (h + 1) * MEM_HEAD_DIM)
        heads.append((_rms(qm[:, sl]) * (gmq_ref[...] * MEM_SCALE)).astype(BF16))
    return heads


def _merge_prompt_kernel(x_ref, an_ref, bn_ref, wo_ref, gmx_ref, wmq_ref, gmq_ref, mk_ref, mv_ref, wmo_ref,
                         y_ref, o_s):
    x1 = x_ref[...] + _dot(an_ref[...], wo_ref[0:A_WIDTH, :]) + _dot(bn_ref[...], wo_ref[A_WIDTH:, :])
    heads = _mem_query(x1, gmx_ref, wmq_ref, gmq_ref)
    lanes = [slice(h * MEM_HEAD_DIM, (h + 1) * MEM_HEAD_DIM) for h in range(MEM_HEADS)]
    scores = [_dot_nt(heads[h], mk_ref[0, :, sl].astype(BF16)) for h, sl in enumerate(lanes)]
    probs = [jnp.exp(s - jnp.max(s, axis=-1, keepdims=True)) for s in scores]
    for p, sl in zip(probs, lanes):
        o = _dot(p.astype(BF16), mv_ref[0, :, sl].astype(BF16)) / jnp.sum(p, axis=-1, keepdims=True)
        o_s[:, sl] = o.astype(BF16)
    y_ref[...] = x1 + _dot(o_s[...], wmo_ref[...])


def _merge_prompt(x, a_n, b_n, mk, mv, w, seq):
    m = x.shape[0]
    tm = TOK_TILE
    per_seq = seq // tm
    mem_len = mk.shape[1]
    row = lambda width: pl.BlockSpec((tm, width), lambda i: (i, 0))
    mem = pl.BlockSpec((1, mem_len, MEM_WIDTH), lambda i: (i // per_seq, 0, 0))
    c = lambda name: _const_spec(w[name].shape)
    return pl.pallas_call(
        _merge_prompt_kernel, grid=(m // tm,),
        in_specs=[row(D_MODEL), row(A_WIDTH), row(B_WIDTH), c('w_o'), c('g_mem_x'), c('w_mq'),
                  c('g_mq'), mem, mem, c('w_mo')],
        out_specs=row(D_MODEL),
        out_shape=jax.ShapeDtypeStruct((m, D_MODEL), F32),
        scratch_shapes=[pltpu.VMEM((tm, MEM_WIDTH), BF16)],
        compiler_params=_params("arbitrary"),
        name="merge_prompt",
    )(x, a_n, b_n, w['w_o'], w['g_mem_x'], w['w_mq'], w['g_mq'], mk, mv, w['w_mo'])


def _merge_sample_kernel(x_ref, an_ref, ctx_ref, wuvb_ref, gob_ref, wo_ref, gmx_ref, wmq_ref, gmq_ref,
                         x1_ref, qm_ref):
    b = _dot(ctx_ref[...].astype(BF16), wuvb_ref[...])
    bn = (_rms(b) * gob_ref[...]).astype(BF16)
    x1 = x_ref[...] + _dot(an_ref[...], wo_ref[0:A_WIDTH, :]) + _dot(bn, wo_ref[A_WIDTH:, :])
    x1_ref[...] = x1
    heads = _mem_query(x1, gmx_ref, wmq_ref, gmq_ref)
    for h in range(MEM_HEADS):
        qm_ref[:, h * MEM_HEAD_DIM:(h + 1) * MEM_HEAD_DIM] = heads[h]


def _merge_sample(x, a_n, ctx, w):
    m = x.shape[0]
    tm = TOK_TILE
    row = lambda width: pl.BlockSpec((tm, width), lambda i: (i, 0))
    c = lambda name: _const_spec(w[name].shape)
    return pl.pallas_call(
        _merge_sample_kernel, grid=(m // tm,),
        in_specs=[row(D_MODEL), row(A_WIDTH), row(ctx.shape[1]), c('w_uv_heads'), c('g_out_b'), c('w_o'),
                  c('g_mem_x'), c('w_mq'), c('g_mq')],
        out_specs=[row(D_MODEL), row(MEM_WIDTH)],
        out_shape=[jax.ShapeDtypeStruct((m, D_MODEL), F32), jax.ShapeDtypeStruct((m, MEM_WIDTH), BF16)],
        compiler_params=_params("arbitrary"),
        name="merge_sample",
    )(x, a_n, ctx, w['w_uv_heads'], w['g_out_b'], w['w_o'], w['g_mem_x'], w['w_mq'], w['g_mq'])


def _memattn_sample_kernel(q_ref, mk_ref, mv_ref, o_ref):
    nrow, nkey = q_ref.shape[1], mk_ref.shape[1]
    s = jnp.einsum('nqd,nkd->nqk', q_ref[...], mk_ref[...].astype(BF16), preferred_element_type=F32)
    same_head = (lax.broadcasted_iota(jnp.int32, (1, nrow, nkey), 1) % MEM_HEADS
                 == lax.broadcasted_iota(jnp.int32, (1, nrow, nkey), 2) % MEM_HEADS)
    s = jnp.where(same_head, s, NEG_INF)
    p = jnp.exp(s - jnp.max(s, axis=-1, keepdims=True))
    o = jnp.einsum('nqk,nkd->nqd', p.astype(BF16), mv_ref[...].astype(BF16), preferred_element_type=F32)
    o_ref[...] = (o / jnp.sum(p, axis=-1, keepdims=True)).astype(BF16)


def _memattn_sample(qm, mk, mv, first_seq):
    n_seq, nrow, _ = qm.shape
    nkey = mk.shape[1]
    ns = MEM_SEQ_TILE
    first_blk = first_seq // ns
    return pl.pallas_call(
        _memattn_sample_kernel, grid=(n_seq // ns,),
        in_specs=[pl.BlockSpec((ns, nrow, MEM_HEAD_DIM), lambda i: (i, 0, 0)),
                  pl.BlockSpec((ns, nkey, MEM_HEAD_DIM), lambda i: (first_blk + i, 0, 0)),
                  pl.BlockSpec((ns, nkey, MEM_HEAD_DIM), lambda i: (first_blk + i, 0, 0))],
        out_specs=pl.BlockSpec((ns, nrow, MEM_HEAD_DIM), lambda i: (i, 0, 0)),
        out_shape=jax.ShapeDtypeStruct((n_seq, nrow, MEM_HEAD_DIM), BF16),
        compiler_params=_params("arbitrary"),
        name="memattn_sample",
    )(qm, mk, mv)


def _ffn_kernel(*refs, sample, tiles_per_seq):
    if sample:
        (x_ref, o_ref, wmo_ref, gffn_ref, wup_ref, wd_ref, wc_ref, bc_ref, p1_ref, p2_ref,
         y_ref, gate_ref, h_s, g_s, act_s) = refs
        x2 = x_ref[...] + _dot(o_ref[...], wmo_ref[...])
        g_s[0:8, :] = jnp.zeros((8, D_FF), F32)
    else:
        (x_ref, gffn_ref, wup_ref, wd_ref, wc_ref, bc_ref, y_ref, tail_ref, h_s, g_s, act_s, carry_s) = refs
        x2 = x_ref[...]

        @pl.when(pl.program_id(0) == 0)
        def _():
            carry_s[...] = jnp.zeros_like(carry_s)
        g_s[0:8, :] = jnp.where((pl.program_id(0) % tiles_per_seq) == 0, 0.0, carry_s[...])
    tm = x_ref.shape[0]
    y_ref[...] = x2
    h_s[...] = (_rms(x2) * gffn_ref[...]).astype(BF16)
    if sample:
        t_in_seq = lax.broadcasted_iota(jnp.int32, (tm, 1), 0) % tiles_per_seq
    for f in range(N_FF_TILES):
        sl = slice(f * FF_TILE, (f + 1) * FF_TILE)
        h = h_s[...]
        g = _dot(h, wup_ref[:, sl])
        val = _dot(h, wup_ref[:, D_FF + f * FF_TILE:D_FF + (f + 1) * FF_TILE])
        g_s[8:, sl] = g
        g1 = g_s[7:7 + tm, sl]
        g2 = g_s[6:6 + tm, sl]
        if sample:
            gate_ref[:, sl] = g
            g1 = jnp.where(t_in_seq >= 1, g1, p1_ref[:, sl])
            g2 = jnp.where(t_in_seq >= 2, g2, p2_ref[:, sl])
        conv = bc_ref[:, sl] + g2 * wc_ref[0:1, sl] + g1 * wc_ref[1:2, sl] + g * wc_ref[2:3, sl]
        act_s[:, sl] = (conv / (1.0 + jnp.exp(-conv)) * val).astype(BF16)
    if not sample:
        tail = g_s[tm:tm + 8, :]
        carry_s[...] = tail
        tail_ref[0] = tail
    y_ref[...] += _dot(act_s[...], wd_ref[...])


def _ffn_scratch(tm):
    return [pltpu.VMEM((tm, D_MODEL), BF16), pltpu.VMEM((tm + 8, D_FF), F32), pltpu.VMEM((tm, D_FF), BF16)]


def _ffn_prompt(x2, w, n_batch, seq):
    m = x2.shape[0]
    tm = TOK_TILE
    per_seq = seq // tm
    row = pl.BlockSpec((tm, D_MODEL), lambda i: (i, 0))
    c = lambda name: _const_spec(w[name].shape)
    return pl.pallas_call(
        functools.partial(_ffn_kernel, sample=False, tiles_per_seq=per_seq), grid=(m // tm,),
        in_specs=[row, c('g_ffn'), c('w_up'), c('w_down'), c('w_conv'), c('b_conv')],
        out_specs=[row, pl.BlockSpec((1, 8, D_FF), lambda i: (i // per_seq, 0, 0))],
        out_shape=[jax.ShapeDtypeStruct((m, D_MODEL), F32), jax.ShapeDtypeStruct((n_batch, 8, D_FF), F32)],
        scratch_shapes=_ffn_scratch(tm) + [pltpu.VMEM((8, D_FF), F32)],
        compiler_params=_params("arbitrary"),
        name="ffn_prompt",
    )(x2, w['g_ffn'], w['w_up'], w['w_down'], w['w_conv'], w['b_conv'])


def _ffn_sample(x1, o, prev1, prev2, w, t_q):
    m = x1.shape[0]
    tm = TOK_TILE
    row = lambda width: pl.BlockSpec((tm, width), lambda i: (i, 0))
    c = lambda name: _const_spec(w[name].shape)
    return pl.pallas_call(
        functools.partial(_ffn_kernel, sample=True, tiles_per_seq=t_q), grid=(m // tm,),
        in_specs=[row(D_MODEL), row(MEM_WIDTH), c('w_mo'), c('g_ffn'), c('w_up'), c('w_down'),
                  c('w_conv'), c('b_conv'), row(D_FF), row(D_FF)],
        out_specs=[row(D_MODEL), row(D_FF)],
        out_shape=[jax.ShapeDtypeStruct((m, D_MODEL), F32), jax.ShapeDtypeStruct((m, D_FF), F32)],
        scratch_shapes=_ffn_scratch(tm),
        compiler_params=_params("arbitrary"),
        name="ffn_sample",
    )(x1, o, w['w_mo'], w['g_ffn'], w['w_up'], w['w_down'], w['w_conv'], w['b_conv'], prev1, prev2)


def _head_pad(wm, width):
    k = wm.shape[0]
    wm = wm.reshape(k, MLA_HEADS, width)
    return jnp.pad(wm, ((0, 0), (0, 0), (0, HEAD_PAD - width))).reshape(k, MLA_HEADS * HEAD_PAD)


def _rot_half_cols(wm):
    half = QK_ROPE // 2
    return jnp.concatenate([-wm[..., half:], wm[..., :half]], axis=-1)


def _lane_vec(nope, rope):
    return jnp.concatenate([nope, rope, jnp.zeros((LANES - QK_HEAD,), F32)])[None, :]


def _rope_tables(pos):
    half = QK_ROPE // 2
    inv_freq = ROPE_THETA ** (-jnp.arange(half, dtype=F32) / half)
    ang = pos.astype(F32)[:, None] * inv_freq[None, :]
    return jnp.cos(ang), jnp.sin(ang)


def _head_tile_tables(pos):
    cos, sin = _rope_tables(pos)
    n = pos.shape[0]
    cos_t = jnp.concatenate([jnp.ones((n, QK_NOPE), F32), cos, cos, jnp.zeros((n, LANES - QK_HEAD), F32)], axis=1)
    sin_t = jnp.concatenate([jnp.zeros((n, QK_NOPE), F32), sin, sin, jnp.zeros((n, LANES - QK_HEAD), F32)], axis=1)
    return cos_t, sin_t


def _feature_table(pos, cols):
    cos, sin = _rope_tables(pos)
    t = jnp.concatenate([cos, cos, sin, sin], axis=1).T
    return jnp.pad(t, ((0, 0), (0, cols - pos.shape[0])))


def _query_feature_perm():
    half = QK_ROPE // 2
    p = np.zeros((LANES, LANES), np.float32)
    for j in range(QK_ROPE):
        p[QK_NOPE + j, j] = 1.0
    for j in range(half):
        p[QK_NOPE + half + j, QK_ROPE + j] = 1.0
        p[QK_NOPE + j, QK_ROPE + half + j] = -1.0
    return jnp.asarray(p, BF16)


def _prep_weights(l, t_q, g_mix, w_in, ln_v_g, ln_v_b, w_s, b_s, g_q_a, w_uq, g_kv_a, w_uk, w_uv, g_qk_q, g_qk_k,
                  g_out_a, g_out_b, w_o, g_mem_x, g_mem_in, w_mq, w_mk, w_mv, g_mq, g_mk, w_mo, g_ffn, w_up,
                  w_conv, b_conv, w_down):
    half = QK_ROPE // 2
    o_kpe = 2 * A_WIDTH + Q_LORA + KV_LORA
    vec = lambda a: a.reshape(1, -1).astype(F32)
    w = {}
    wkpe = w_in[l][:, o_kpe:]
    lane_tile = lambda blk: jnp.pad(blk, ((0, 0), (QK_NOPE, LANES - QK_HEAD)))
    w['w_in'] = jnp.concatenate([w_in[l][:, :o_kpe], lane_tile(wkpe), lane_tile(_rot_half_cols(wkpe))],
                                axis=1).astype(BF16)
    w['g_mix'] = vec(g_mix[l])
    w['ln_g'] = vec(ln_v_g[l])
    w['ln_b'] = vec(ln_v_b[l])
    w['g_q_a'] = vec(g_q_a[l])
    w['g_kv_a'] = vec(g_kv_a[l])
    w['g_out_a'] = vec(g_out_a[l])
    w['g_out_b'] = vec(g_out_b[l])
    w['g_out_b_col'] = g_out_b[l][:, None]
    wq = w_uq[l].reshape(Q_LORA, MLA_HEADS, QK_HEAD)
    wq_rot = jnp.concatenate([jnp.zeros((Q_LORA, MLA_HEADS, QK_NOPE), F32), _rot_half_cols(wq[..., QK_NOPE:])], axis=-1)
    w['wq'] = _head_pad(wq.reshape(Q_LORA, -1), QK_HEAD).astype(BF16)
    w['wq_rot'] = _head_pad(wq_rot.reshape(Q_LORA, -1), QK_HEAD).astype(BF16)
    w['wk'] = _head_pad(w_uk[l], QK_NOPE).astype(BF16)
    w['wvt'] = w_uv[l].T.astype(BF16)
    gq, gk = g_qk_q[l], g_qk_k[l]
    sw = lambda g: jnp.concatenate([g[QK_NOPE + half:], g[QK_NOPE:QK_NOPE + half]])
    zeros_n = jnp.zeros((QK_NOPE,), F32)
    w['gq'] = _lane_vec(gq[:QK_NOPE], gq[QK_NOPE:])
    w['gq_sw'] = _lane_vec(zeros_n, sw(gq))
    w['gk'] = _lane_vec(gk[:QK_NOPE], gk[QK_NOPE:])
    w['gk_sw'] = _lane_vec(zeros_n, sw(gk))
    w['gk_nope'] = _lane_vec(gk[:QK_NOPE], jnp.zeros((QK_ROPE,), F32))
    w['pq'] = _query_feature_perm()
    w['wukt'] = w_uk[l].T.astype(BF16)
    gr = gk[QK_NOPE:]
    w['gcol'] = jnp.concatenate([gr, gr])[:, None]
    wuv = w_uv[l].reshape(KV_LORA, MLA_HEADS, V_HEAD)
    eye = jnp.eye(MLA_HEADS, dtype=F32)
    w['w_uv_heads'] = (wuv[None, :, :, :] * eye[:, None, :, None]).reshape(MLA_HEADS * KV_LORA, B_WIDTH).astype(BF16)
    tril = jnp.tril(jnp.ones((CHUNK, CHUNK), F32))
    w['ws_prompt'] = (w_s[l] * tril).astype(BF16)
    w['bs_prompt'] = jnp.repeat(b_s[l].T, A_GROUP_DIM, axis=1)
    reps = CHUNK // t_q
    blk = w_s[l][:, :t_q, :t_q] * jnp.tril(jnp.ones((t_q, t_q), F32))
    w['ws_sample'] = jnp.einsum('ij,gts->gitjs', jnp.eye(reps, dtype=F32), blk).reshape(A_GROUPS, CHUNK, CHUNK).astype(BF16)
    w['bs_sample'] = jnp.repeat(jnp.tile(b_s[l][:, :t_q].T, (reps, 1)), A_GROUP_DIM, axis=1)
    w['w_o'] = w_o[l].astype(BF16)
    w['g_mem_x'] = vec(g_mem_x[l])
    w['g_mem_in'] = vec(g_mem_in[l])
    w['w_mq'] = w_mq[l].astype(BF16)
    w['w_mk'] = w_mk[l].astype(BF16)
    w['w_mv'] = w_mv[l].astype(BF16)
    w['g_mq'] = vec(g_mq[l])
    w['g_mk'] = vec(g_mk[l])
    w['w_mo'] = w_mo[l].astype(BF16)
    w['g_ffn'] = vec(g_ffn[l])
    w['w_up'] = w_up[l].astype(BF16)
    w['w_down'] = w_down[l].astype(BF16)
    w['w_conv'] = w_conv[l]
    w['b_conv'] = vec(b_conv[l])
    return w


def kernel(x_prompt, x_sample, cache_ckv, cache_kpe, cache_mem_k, cache_mem_v, state_ffn_conv, page_table, mem_prompt, g_mix, w_in, ln_v_g, ln_v_b, w_s, b_s, g_q_a, w_uq, g_kv_a, w_uk, w_uv, g_qk_q, g_qk_k, g_out_a, g_out_b, w_o, g_mem_x, g_mem_in, w_mq, w_mk, w_mv, g_mq, g_mk, w_mo, g_ffn, w_up, w_conv, b_conv, w_down):
    n_p, seq, _ = x_prompt.shape
    n_s, t_q, _ = x_sample.shape
    depth = g_mix.shape[0]
    mem_len = mem_prompt.shape[1]
    n_pages = page_table.shape[1]
    past = n_pages * PAGE_SIZE

    pos_p = jnp.arange(seq)
    pos_s = past + jnp.arange(t_q)
    cos_p, sin_p = _head_tile_tables(pos_p)
    cos_s, sin_s = _head_tile_tables(jnp.tile(pos_s, TOK_TILE // t_q))
    n_kv_tiles = past // KV_TILE
    t_past = _feature_table(jnp.arange(past), past).reshape(2 * QK_ROPE, n_kv_tiles, KV_TILE).transpose(1, 0, 2)
    t_new = _feature_table(pos_s, LANES)
    n_phys = cache_ckv.shape[1]
    ckv_pages = cache_ckv.reshape(depth * n_phys, PAGE_SIZE, KV_LORA)
    kpe_pages_t = jnp.swapaxes(cache_kpe, -1, -2).reshape(depth * n_phys, QK_ROPE, PAGE_SIZE)
    mem_rows = mem_len * MEM_HEADS
    mem_k_rows = cache_mem_k.reshape(depth * n_s, mem_rows, MEM_HEAD_DIM)
    mem_v_rows = cache_mem_v.reshape(depth * n_s, mem_rows, MEM_HEAD_DIM)

    xp = x_prompt.reshape(n_p * seq, D_MODEL)
    xs = x_sample.reshape(n_s * t_q, D_MODEL)
    outs = {k: [] for k in ('p_ckv', 'p_kpe', 'p_mk', 'p_mv', 'p_conv', 's_ckv', 's_kpe', 's_chunk_v', 's_conv')}
    for l in range(depth):
        w = _prep_weights(l, t_q, g_mix, w_in, ln_v_g, ln_v_b, w_s, b_s, g_q_a, w_uq, g_kv_a, w_uk, w_uv, g_qk_q,
                          g_qk_k, g_out_a, g_out_b, w_o, g_mem_x, g_mem_in, w_mq, w_mk, w_mv, g_mq, g_mk, w_mo,
                          g_ffn, w_up, w_conv, b_conv, w_down)
        w['t_past'], w['t_new'] = t_past, t_new

        wp = dict(w, ws=w['ws_prompt'], bs=w['bs_prompt'])
        q, k, vt, ckvn, kpe, a_n = _inproj(xp, cos_p, sin_p, wp, sample=False)
        b_n = _flash(q, k, vt, w['g_out_b_col'], n_p, seq)
        mk, mv = _memkv(mem_prompt.reshape(n_p * mem_len, D_MODEL), w)
        mk3 = mk.reshape(n_p, mem_len, MEM_WIDTH)
        mv3 = mv.reshape(n_p, mem_len, MEM_WIDTH)
        x2 = _merge_prompt(xp, a_n, b_n, mk3, mv3, w, seq)
        xp, tail = _ffn_prompt(x2, w, n_p, seq)
        outs['p_ckv'].append(ckvn.reshape(n_p, seq, KV_LORA))
        outs['p_kpe'].append(kpe.reshape(n_p, seq, QK_ROPE))
        outs['p_mk'].append(mk.reshape(n_p, mem_len, MEM_HEADS, MEM_HEAD_DIM))
        outs['p_mv'].append(mv.reshape(n_p, mem_len, MEM_HEADS, MEM_HEAD_DIM))
        outs['p_conv'].append(tail[:, 8 - (CONV_W - 1):, :])

        wsm = dict(w, ws=w['ws_sample'], bs=w['bs_sample'])
        qlat, qrf, ckvn_s, kpe_s, a_ns, vg = _inproj(xs, cos_s, sin_s, wsm, sample=True)
        nq = t_q * MLA_HEADS
        kpnew_t = jnp.pad(jnp.swapaxes(kpe_s.reshape(n_s, t_q, QK_ROPE), 1, 2), ((0, 0), (0, 0), (0, LANES - t_q)))
        ctx = _paged(page_table + l * n_phys, qlat.reshape(n_s, nq, KV_LORA), qrf.reshape(n_s, nq, LANES),
                     ckvn_s.reshape(n_s, t_q, KV_LORA), kpnew_t, ckv_pages, kpe_pages_t, w)
        x1, qm = _merge_sample(xs, a_ns, ctx.reshape(n_s * t_q, MLA_HEADS * KV_LORA), w)
        o = _memattn_sample(qm.reshape(n_s, t_q * MEM_HEADS, MEM_HEAD_DIM), mem_k_rows, mem_v_rows, l * n_s)
        st = state_ffn_conv[l]
        zero = jnp.zeros_like(st[:, :1])
        prev1 = jnp.concatenate([st[:, 1:2]] + [zero] * (t_q - 1), axis=1)
        prev2 = jnp.concatenate([st[:, 0:1], st[:, 1:2]] + [zero] * (t_q - 2), axis=1)
        flat = lambda a: a.reshape(n_s * t_q, D_FF)
        xs, gate = _ffn_sample(x1, o.reshape(n_s * t_q, MEM_WIDTH), flat(prev1), flat(prev2), w, t_q)
        gate = gate.reshape(n_s, t_q, D_FF)
        outs['s_ckv'].append(ckvn_s.reshape(n_s, t_q, KV_LORA))
        outs['s_kpe'].append(kpe_s.reshape(n_s, t_q, QK_ROPE))
        outs['s_chunk_v'].append(vg.reshape(n_s, t_q, A_GROUPS, A_GROUP_DIM))
        outs['s_conv'].append(gate[:, t_q - (CONV_W - 1):, :])

    return (xp.reshape(n_p, seq, D_MODEL), xs.reshape(n_s, t_q, D_MODEL),
            jnp.stack(outs['p_ckv']), jnp.stack(outs['p_kpe']), jnp.stack(outs['p_mk']), jnp.stack(outs['p_mv']),
            jnp.stack(outs['p_conv']), jnp.stack(outs['s_ckv']), jnp.stack(outs['s_kpe']),
            jnp.stack(outs['s_chunk_v']), jnp.stack(outs['s_conv']))
```

```python
import functools

import numpy as np
import jax
import jax.numpy as jnp
from jax import lax
from jax.experimental import pallas as pl
from jax.experimental.pallas import tpu as pltpu

D_MODEL = 1024
A_GROUPS = 8
A_GROUP_DIM = 64
A_WIDTH = A_GROUPS * A_GROUP_DIM
CHUNK = 128
MLA_HEADS = 8
Q_LORA = 256
KV_LORA = 256
QK_NOPE = 64
QK_ROPE = 32
QK_HEAD = QK_NOPE + QK_ROPE
V_HEAD = 64
B_WIDTH = MLA_HEADS * V_HEAD
ROPE_THETA = 10000.0
PAGE_SIZE = 128
MEM_HEADS = 4
MEM_HEAD_DIM = 128
MEM_WIDTH = MEM_HEADS * MEM_HEAD_DIM
D_FF = 2816
CONV_W = 3
EPS = 1e-6
NEG_INF = -1e30

LANES = 128
HEAD_PAD = LANES
QK_SCALE = QK_HEAD ** -0.5
LOG2E = 1.4426950408889634
V_AUG = V_HEAD + 16
MEM_SCALE = MEM_HEAD_DIM ** -0.5

TOK_TILE = 512
FF_TILE = 256
N_FF_TILES = D_FF // FF_TILE
ATT_TILE = 256
KV_TILE = 1024
MEM_SEQ_TILE = 8
VMEM_LIMIT = 56 * 1024 * 1024

F32 = jnp.float32
BF16 = jnp.bfloat16


def _dot(a, b):
    return jnp.dot(a, b, preferred_element_type=F32)


def _dot_nt(a, b):
    return lax.dot_general(a, b, (((1,), (1,)), ((), ())), preferred_element_type=F32)


def _rms(x):
    return x * lax.rsqrt(jnp.mean(x * x, axis=-1, keepdims=True) + EPS)


def _gelu(x):
    return jax.nn.gelu(x, approximate=True)


def _const_spec(shape):
    nd = len(shape)
    return pl.BlockSpec(shape, lambda *_: (0,) * nd, pipeline_mode=pl.Buffered(1))


def _params(*sem):
    return pltpu.CompilerParams(dimension_semantics=sem, vmem_limit_bytes=VMEM_LIMIT)


def _inproj_kernel(x_ref, gmix_ref, win_ref, lng_ref, lnb_ref, ws_ref, bs_ref, gqa_ref, wq_ref, wqr_ref,
                   gkva_ref, wk_ref, wv_ref, gq_ref, gqs_ref, gk_ref, gks_ref, cos_ref, sin_ref, goa_ref,
                   pq_ref, *rest, sample):
    if sample:
        qlat_ref, qrf_ref, ckvn_ref, kpe_ref, an_ref, vg_ref, zu_s, vgb_s, a_s = rest
    else:
        q_ref, k_ref, v_ref, ckvn_ref, kpe_ref, an_ref, zu_s, vgb_s, a_s = rest
    tm = x_ref.shape[0]
    xn = (_rms(x_ref[...]) * gmix_ref[...]).astype(BF16)

    o_cq = 2 * A_WIDTH
    o_ckv = o_cq + Q_LORA
    o_kpe = o_ckv + KV_LORA
    zv = _dot(xn, win_ref[:, A_WIDTH:2 * A_WIDTH])
    cq = _dot(xn, win_ref[:, o_cq:o_ckv])
    ckv = _dot(xn, win_ref[:, o_ckv:o_kpe])
    kpt = _dot(xn, win_ref[:, o_kpe:o_kpe + LANES])
    if not sample:
        kpr = _dot(xn, win_ref[:, o_kpe + LANES:o_kpe + 2 * LANES])
    zu_s[...] = _dot(xn, win_ref[:, 0:A_WIDTH])

    cqn = (_rms(cq) * gqa_ref[...]).astype(BF16)
    ckvn = _rms(ckv) * gkva_ref[...]
    ckvn_ref[...] = ckvn
    kpe_ref[...] = kpt[:, QK_NOPE:QK_HEAD]
    qall = _dot(cqn, wq_ref[...])
    qrall = _dot(cqn, wqr_ref[...])
    if not sample:
        cb = ckvn.astype(BF16)
        kall = _dot(cb, wk_ref[...])
        vt = _dot_nt(wv_ref[...], cb).astype(BF16)
        ones = jnp.ones((V_AUG - V_HEAD, ATT_TILE), BF16)
        for c in range(tm // ATT_TILE):
            cols = slice(c * ATT_TILE, (c + 1) * ATT_TILE)
            for h in range(MLA_HEADS):
                v_ref[c, h * V_AUG:h * V_AUG + V_HEAD, :] = vt[h * V_HEAD:(h + 1) * V_HEAD, cols]
                v_ref[c, h * V_AUG + V_HEAD:(h + 1) * V_AUG, :] = ones

    gv = _gelu(zv)
    low = lax.broadcasted_iota(jnp.int32, (1, LANES), 1) < A_GROUP_DIM
    inv_g = 1.0 / A_GROUP_DIM
    for p in range(A_WIDTH // LANES):
        sl = slice(p * LANES, (p + 1) * LANES)
        t = gv[:, sl]
        s_lo = jnp.sum(jnp.where(low, t, 0.0), axis=-1, keepdims=True)
        s_hi = jnp.sum(jnp.where(low, 0.0, t), axis=-1, keepdims=True)
        tc = t - jnp.where(low, s_lo, s_hi) * inv_g
        tc2 = tc * tc
        v_lo = jnp.sum(jnp.where(low, tc2, 0.0), axis=-1, keepdims=True)
        v_hi = jnp.sum(jnp.where(low, 0.0, tc2), axis=-1, keepdims=True)
        y = tc * lax.rsqrt(jnp.where(low, v_lo, v_hi) * inv_g + EPS) * lng_ref[:, sl] + lnb_ref[:, sl]
        if sample:
            vg_ref[:, sl] = y
        vgb_s[:, sl] = y.astype(BF16)
    for c in range(tm // CHUNK):
        rows = slice(c * CHUNK, (c + 1) * CHUNK)
        for p in range(A_WIDTH // LANES):
            sl = slice(p * LANES, (p + 1) * LANES)
            vp = vgb_s[rows, sl]
            mixed = jnp.where(low, _dot(ws_ref[2 * p], vp), _dot(ws_ref[2 * p + 1], vp)) + bs_ref[:, sl]
            a_s[rows, sl] = _gelu(zu_s[rows, sl]) * mixed
    an_ref[...] = (_rms(a_s[...]) * goa_ref[...]).astype(BF16)

    cosv = cos_ref[...]
    sinv = sin_ref[...]
    q_scale = QK_SCALE if sample else QK_SCALE * LOG2E
    qa = gq_ref[...] * cosv * q_scale
    qb = gqs_ref[...] * sinv * q_scale
    inv_d = 1.0 / QK_HEAD
    if not sample:
        ka = gk_ref[...] * cosv
        kb = gks_ref[...] * sinv
    for h in range(MLA_HEADS):
        sl = slice(h * HEAD_PAD, (h + 1) * HEAD_PAD)
        qh = qall[:, sl]
        r = lax.rsqrt(jnp.sum(qh * qh, axis=-1, keepdims=True) * inv_d + EPS)
        qf = r * (qh * qa + qrall[:, sl] * qb)
        if sample:
            qg = (qf * gk_ref[...]).astype(BF16)
            qlat_ref[:, h * KV_LORA:(h + 1) * KV_LORA] = _dot_nt(qg, wk_ref[:, sl]).astype(BF16)
            qrf_ref[:, sl] = _dot(qf.astype(BF16), pq_ref[...]).astype(BF16)
        else:
            q_ref[:, sl] = qf.astype(BF16)
            kh = kall[:, sl] + kpt
            rk = lax.rsqrt(jnp.sum(kh * kh, axis=-1, keepdims=True) * inv_d + EPS)
            k_ref[:, sl] = (rk * (kh * ka + kpr * kb)).astype(BF16)


def _inproj(x, cos, sin, w, *, sample):
    m = x.shape[0]
    tm = TOK_TILE
    n_tiles = m // tm
    n_pos_tiles = cos.shape[0] // tm
    row = lambda width: pl.BlockSpec((tm, width), lambda i: (i, 0))
    pos = pl.BlockSpec((tm, LANES), lambda i: (i % n_pos_tiles, 0))
    consts = [w['g_mix'], w['w_in'], w['ln_g'], w['ln_b'], w['ws'], w['bs'], w['g_q_a'], w['wq'], w['wq_rot'],
              w['g_kv_a'], w['wk'], w['wvt'], w['gq'], w['gq_sw'], w['gk_nope'] if sample else w['gk'], w['gk_sw']]
    in_specs = ([row(D_MODEL)] + [_const_spec(c.shape) for c in consts] + [pos, pos]
                + [_const_spec(w['g_out_a'].shape), _const_spec(w['pq'].shape)])
    args = [x] + consts + [cos, sin, w['g_out_a'], w['pq']]
    wide = MLA_HEADS * HEAD_PAD
    if sample:
        out_shape = [jax.ShapeDtypeStruct((m, MLA_HEADS * KV_LORA), BF16), jax.ShapeDtypeStruct((m, wide), BF16),
                     jax.ShapeDtypeStruct((m, KV_LORA), F32), jax.ShapeDtypeStruct((m, QK_ROPE), F32),
                     jax.ShapeDtypeStruct((m, A_WIDTH), BF16), jax.ShapeDtypeStruct((m, A_WIDTH), F32)]
        out_specs = [row(MLA_HEADS * KV_LORA), row(wide), row(KV_LORA), row(QK_ROPE), row(A_WIDTH), row(A_WIDTH)]
    else:
        att_tiles = tm // ATT_TILE
        out_shape = [jax.ShapeDtypeStruct((m, wide), BF16)] * 2 + [
            jax.ShapeDtypeStruct((m // ATT_TILE, MLA_HEADS * V_AUG, ATT_TILE), BF16),
            jax.ShapeDtypeStruct((m, KV_LORA), F32), jax.ShapeDtypeStruct((m, QK_ROPE), F32),
            jax.ShapeDtypeStruct((m, A_WIDTH), BF16)]
        out_specs = [row(wide)] * 2 + [pl.BlockSpec((att_tiles, MLA_HEADS * V_AUG, ATT_TILE), lambda i: (i, 0, 0)),
                                       row(KV_LORA), row(QK_ROPE), row(A_WIDTH)]
    return pl.pallas_call(
        functools.partial(_inproj_kernel, sample=sample),
        grid=(n_tiles,), in_specs=in_specs, out_specs=out_specs, out_shape=out_shape,
        scratch_shapes=[pltpu.VMEM((tm, A_WIDTH), F32), pltpu.VMEM((tm, A_WIDTH), BF16),
                        pltpu.VMEM((tm, A_WIDTH), F32)],
        compiler_params=_params("arbitrary"),
        name="inproj_sample" if sample else "inproj_prompt",
    )(*args)


def _flash_kernel(q_ref, k_ref, vt_ref, gob_ref, o_ref):
    i = pl.program_id(1)
    t = ATT_TILE
    causal = lax.broadcasted_iota(jnp.int32, (t, t), 0) <= lax.broadcasted_iota(jnp.int32, (t, t), 1)

    def step(j, carry, masked):
        start = pl.multiple_of(j * t, t)
        vt = vt_ref[j]
        heads = [slice(h * HEAD_PAD, (h + 1) * HEAD_PAD) for h in range(MLA_HEADS)]
        scores = [_dot_nt(k_ref[pl.ds(start, t), sl], q_ref[:, sl]) for sl in heads]
        stats = []
        for h in range(MLA_HEADS):
            m = carry[h][0]
            s = jnp.where(causal, scores[h], NEG_INF) if masked else scores[h]
            m_new = jnp.maximum(m, jnp.max(s, axis=0, keepdims=True))
            stats.append((m_new, jnp.exp2(m - m_new), jnp.exp2(s - m_new).astype(BF16)))
        new = []
        for h, (m_new, alpha, p) in enumerate(stats):
            pv = _dot(vt[h * V_AUG:(h + 1) * V_AUG, :], p)
            _, l, acc = carry[h]
            new.append((m_new, alpha * l + pv[V_HEAD:V_HEAD + 1], alpha * acc + pv[:V_HEAD]))
        return tuple(new)

    init = tuple((jnp.full((1, t), NEG_INF, F32), jnp.zeros((1, t), F32), jnp.zeros((V_HEAD, t), F32))
                 for _ in range(MLA_HEADS))
    carry = lax.fori_loop(0, i, functools.partial(step, masked=False), init)
    carry = step(i, carry, True)
    o_t = jnp.concatenate([acc / l for _, l, acc in carry], axis=0)
    bn_t = o_t * lax.rsqrt(jnp.mean(o_t * o_t, axis=0, keepdims=True) + EPS) * gob_ref[...]
    o_ref[...] = bn_t.T.astype(BF16)


def _flash(q, k, vt, g_out_b_col, n_batch, seq):
    m = q.shape[0]
    nq = seq // ATT_TILE
    wide = MLA_HEADS * HEAD_PAD
    return pl.pallas_call(
        _flash_kernel,
        grid=(n_batch, nq),
        in_specs=[pl.BlockSpec((ATT_TILE, wide), lambda b, i: (b * nq + i, 0)),
                  pl.BlockSpec((seq, wide), lambda b, i: (b, 0)),
                  pl.BlockSpec((nq, MLA_HEADS * V_AUG, ATT_TILE), lambda b, i: (b, 0, 0)),
                  pl.BlockSpec((B_WIDTH, 1), lambda b, i: (0, 0))],
        out_specs=pl.BlockSpec((ATT_TILE, B_WIDTH), lambda b, i: (b * nq + i, 0)),
        out_shape=jax.ShapeDtypeStruct((m, B_WIDTH), BF16),
        compiler_params=_params("arbitrary", "arbitrary"),
        name="flash_prompt",
    )(q, k, vt, g_out_b_col)


def _paged_kernel(pt_ref, qlat_ref, qrf_ref, qlat_next_ref, qrf_next_ref, cnew_ref, kpnew_ref, wukt_ref, gcol_ref,
                  tpast_ref, tnew_ref, ckv_hbm, kpe_hbm, ctx_ref, cbuf, pbuf, cn_s, tg_s, s0_s, sems, *, n_seq, n_pages):
    n = pl.program_id(0)
    slot = n % 2
    ppt = KV_TILE // PAGE_SIZE
    n_tiles = n_pages // ppt

    def start_fetch(seq, sl, first=0, last=n_pages):
        for p in range(first, last):
            page = pt_ref[seq, p]
            pltpu.make_async_copy(ckv_hbm.at[page], cbuf.at[sl, p], sems.at[0, sl]).start()
            pltpu.make_async_copy(kpe_hbm.at[page], pbuf.at[sl, p], sems.at[1, sl]).start()

    def wait_fetch(sl):
        pltpu.make_async_copy(ckv_hbm.at[pl.ds(0, n_pages)], cbuf.at[sl], sems.at[0, sl]).wait()
        pltpu.make_async_copy(kpe_hbm.at[pl.ds(0, n_pages)], pbuf.at[sl], sems.at[1, sl]).wait()

    nq = qlat_ref.shape[1]
    t_q = nq // MLA_HEADS
    n_nope = MLA_HEADS * QK_NOPE
    n_feat = 3 * QK_ROPE
    ones_rows = jnp.where(lax.broadcasted_iota(jnp.int32, (16, n_feat), 1) >= 2 * QK_ROPE, 1.0, 0.0).astype(BF16)

    def query_mats(ql_ref, qr_ref):
        a_mat = jnp.concatenate([wukt_ref[...], ql_ref[0]], axis=0)
        l_mat = jnp.concatenate([qr_ref[0][:, :n_feat], ones_rows], axis=0)
        return a_mat, l_mat

    def scores(mats, cb, kpt, tg):
        a_mat, l_mat = mats
        tk = cb.shape[0]
        m1 = _dot_nt(a_mat, cb)
        knt = m1[:n_nope]
        ssq = jnp.sum((knt * knt).reshape(MLA_HEADS, QK_NOPE, tk), axis=1)
        feat = jnp.concatenate([kpt * tg[:QK_ROPE], kpt * tg[QK_ROPE:], kpt * kpt], axis=0)
        m2 = _dot(l_mat, feat.astype(BF16))
        r = lax.rsqrt((ssq + m2[nq:nq + MLA_HEADS]) * (1.0 / QK_HEAD) + EPS)
        s = (m1[n_nope:] + m2[:nq]).reshape(t_q, MLA_HEADS, tk) * r[None]
        return s.reshape(nq, tk)

    def update(carry, s, cb):
        m, l, acc = carry
        m_new = jnp.maximum(m, jnp.max(s, axis=-1, keepdims=True))
        alpha = jnp.exp(m - m_new)
        p = jnp.exp(s - m_new)
        l = alpha * l + jnp.sum(p, axis=-1, keepdims=True)
        acc = alpha * acc + _dot(p.astype(BF16), cb)
        return m_new, l, acc

    def tile_scores(mats, sl, t):
        cb = cbuf[sl, t * ppt:(t + 1) * ppt].reshape(KV_TILE, KV_LORA).astype(BF16)
        kpt = jnp.concatenate([pbuf[sl, t * ppt + i] for i in range(ppt)], axis=1)
        return scores(mats, cb, kpt, tg_s[t]), cb

    @pl.when(n == 0)
    def _():
        cn_s[...] = jnp.zeros_like(cn_s)
        for t in range(n_tiles):
            tg_s[t] = tpast_ref[t] * gcol_ref[...]
        start_fetch(0, 0)
        wait_fetch(0)
        s0_s[...] = tile_scores(query_mats(qlat_ref, qrf_ref), 0, 0)[0]

    cur = query_mats(qlat_ref, qrf_ref)
    nslot = 1 - slot
    nseq = jnp.minimum(n + 1, n_seq - 1)
    per_tile = -(-n_pages // (n_tiles - 2))
    cn_s[0:t_q, :] = cnew_ref[0]
    cb_new = cn_s[...].astype(BF16)
    s_new = scores(cur, cb_new, kpnew_ref[0], tnew_ref[...] * gcol_ref[...])
    key = lax.broadcasted_iota(jnp.int32, (nq, LANES), 1)
    qry = lax.broadcasted_iota(jnp.int32, (nq, LANES), 0) // MLA_HEADS
    s_new = jnp.where(key <= qry, s_new, NEG_INF)
    s_cur = s0_s[...]
    cb_cur = cbuf[slot, 0:ppt].reshape(KV_TILE, KV_LORA).astype(BF16)
    stats = (jnp.full((nq, 1), NEG_INF, F32), jnp.zeros((nq, 1), F32), jnp.zeros((nq, KV_LORA), F32))
    for t in range(n_tiles):
        start_fetch(nseq, nslot, min(t * per_tile, n_pages), min((t + 1) * per_tile, n_pages))
        if t + 1 < n_tiles:
            s_next, cb_next = tile_scores(cur, slot, t + 1)
        else:
            wait_fetch(nslot)
            s_next, cb_next = tile_scores(query_mats(qlat_next_ref, qrf_next_ref), nslot, 0)
        stats = update(stats, s_cur, cb_cur)
        s_cur, cb_cur = s_next, cb_next
    s0_s[...] = s_cur
    m, l, acc = update(stats, s_new, cb_new)
    ctx_ref[0] = acc / l


def _paged(page_table, qlat, qrf, cnew, kpnew_t, cache_ckv, cache_kpe_t, w):
    n_seq, n_pages = page_table.shape
    nq = qlat.shape[1]
    t_q = cnew.shape[1]
    seq_spec = lambda shape: pl.BlockSpec((1,) + shape, lambda n, pt: (n, 0, 0))
    next_spec = lambda shape: pl.BlockSpec((1,) + shape, lambda n, pt: (jnp.minimum(n + 1, n_seq - 1), 0, 0))
    const = lambda shape: pl.BlockSpec(shape, lambda n, pt: (0,) * len(shape), pipeline_mode=pl.Buffered(1))
    grid_spec = pltpu.PrefetchScalarGridSpec(
        num_scalar_prefetch=1, grid=(n_seq,),
        in_specs=[seq_spec((nq, KV_LORA)), seq_spec((nq, LANES)), next_spec((nq, KV_LORA)), next_spec((nq, LANES)),
                  seq_spec((t_q, KV_LORA)), seq_spec((QK_ROPE, LANES)),
                  const(w['wukt'].shape), const(w['gcol'].shape), const(w['t_past'].shape), const(w['t_new'].shape),
                  pl.BlockSpec(memory_space=pl.ANY), pl.BlockSpec(memory_space=pl.ANY)],
        out_specs=seq_spec((nq, KV_LORA)),
        scratch_shapes=[pltpu.VMEM((2, n_pages, PAGE_SIZE, KV_LORA), F32),
                        pltpu.VMEM((2, n_pages, QK_ROPE, PAGE_SIZE), F32),
                        pltpu.VMEM((LANES, KV_LORA), F32), pltpu.VMEM(w['t_past'].shape, F32),
                        pltpu.VMEM((nq, KV_TILE), F32), pltpu.SemaphoreType.DMA((2, 2))])
    return pl.pallas_call(
        functools.partial(_paged_kernel, n_seq=n_seq, n_pages=n_pages),
        grid_spec=grid_spec,
        out_shape=jax.ShapeDtypeStruct((n_seq, nq, KV_LORA), F32),
        compiler_params=_params("arbitrary"),
        name="paged_sample",
    )(page_table, qlat, qrf, qlat, qrf, cnew, kpnew_t, w['wukt'], w['gcol'], w['t_past'], w['t_new'],
      cache_ckv, cache_kpe_t)


def _memkv_kernel(mem_ref, gin_ref, wmk_ref, wmv_ref, gmk_ref, k_ref, v_ref):
    hm = (_rms(mem_ref[...]) * gin_ref[...]).astype(BF16)
    kk = _dot(hm, wmk_ref[...])
    for h in range(MEM_HEADS):
        sl = slice(h * MEM_HEAD_DIM, (h + 1) * MEM_HEAD_DIM)
        k_ref[:, sl] = _rms(kk[:, sl]) * gmk_ref[...]
    v_ref[...] = _dot(hm, wmv_ref[...])


def _memkv(mem, w):
    m = mem.shape[0]
    tm = TOK_TILE
    row = lambda width: pl.BlockSpec((tm, width), lambda i: (i, 0))
    consts = [w['g_mem_in'], w['w_mk'], w['w_mv'], w['g_mk']]
    return pl.pallas_call(
        _memkv_kernel, grid=(m // tm,),
        in_specs=[row(D_MODEL)] + [_const_spec(c.shape) for c in consts],
        out_specs=[row(MEM_WIDTH), row(MEM_WIDTH)],
        out_shape=[jax.ShapeDtypeStruct((m, MEM_WIDTH), F32)] * 2,
        compiler_params=_params("arbitrary"),
        name="memkv_prompt",
    )(mem, *consts)


def _mem_query(x1, gmx_ref, wmq_ref, gmq_ref):
    qm = _dot((_rms(x1) * gmx_ref[...]).astype(BF16), wmq_ref[...])
    heads = []
    for h in range(MEM_HEADS):
        sl = slice(h * MEM_HEAD_DIM, (h + 1) * MEM_HEAD_DIM)
        heads.append((_rms(qm[:, sl]) * (gmq_ref[...] * MEM_SCALE)).astype(BF16))
    return heads


def _merge_prompt_kernel(x_ref, an_ref, bn_ref, wo_ref, gmx_ref, wmq_ref, gmq_ref, mk_ref, mv_ref, wmo_ref,
                         y_ref, o_s):
    x1 = x_ref[...] + _dot(an_ref[...], wo_ref[0:A_WIDTH, :]) + _dot(bn_ref[...], wo_ref[A_WIDTH:, :])
    heads = _mem_query(x1, gmx_ref, wmq_ref, gmq_ref)
    lanes = [slice(h * MEM_HEAD_DIM, (h + 1) * MEM_HEAD_DIM) for h in range(MEM_HEADS)]
    scores = [_dot_nt(heads[h], mk_ref[0, :, sl].astype(BF16)) for h, sl in enumerate(lanes)]
    probs = [jnp.exp(s - jnp.max(s, axis=-1, keepdims=True)) for s in scores]
    for p, sl in zip(probs, lanes):
        o = _dot(p.astype(BF16), mv_ref[0, :, sl].astype(BF16)) / jnp.sum(p, axis=-1, keepdims=True)
        o_s[:, sl] = o.astype(BF16)
    y_ref[...] = x1 + _dot(o_s[...], wmo_ref[...])


def _merge_prompt(x, a_n, b_n, mk, mv, w, seq):
    m = x.shape[0]
    tm = TOK_TILE
    per_seq = seq // tm
    mem_len = mk.shape[1]
    row = lambda width: pl.BlockSpec((tm, width), lambda i: (i, 0))
    mem = pl.BlockSpec((1, mem_len, MEM_WIDTH), lambda i: (i // per_seq, 0, 0))
    c = lambda name: _const_spec(w[name].shape)
    return pl.pallas_call(
        _merge_prompt_kernel, grid=(m // tm,),
        in_specs=[row(D_MODEL), row(A_WIDTH), row(B_WIDTH), c('w_o'), c('g_mem_x'), c('w_mq'),
                  c('g_mq'), mem, mem, c('w_mo')],
        out_specs=row(D_MODEL),
        out_shape=jax.ShapeDtypeStruct((m, D_MODEL), F32),
        scratch_shapes=[pltpu.VMEM((tm, MEM_WIDTH), BF16)],
        compiler_params=_params("arbitrary"),
        name="merge_prompt",
    )(x, a_n, b_n, w['w_o'], w['g_mem_x'], w['w_mq'], w['g_mq'], mk, mv, w['w_mo'])


def _merge_sample_kernel(x_ref, an_ref, ctx_ref, wuvb_ref, gob_ref, wo_ref, gmx_ref, wmq_ref, gmq_ref,
                         x1_ref, qm_ref):
    b = _dot(ctx_ref[...].astype(BF16), wuvb_ref[...])
    bn = (_rms(b) * gob_ref[...]).astype(BF16)
    x1 = x_ref[...] + _dot(an_ref[...], wo_ref[0:A_WIDTH, :]) + _dot(bn, wo_ref[A_WIDTH:, :])
    x1_ref[...] = x1
    heads = _mem_query(x1, gmx_ref, wmq_ref, gmq_ref)
    for h in range(MEM_HEADS):
        qm_ref[:, h * MEM_HEAD_DIM:(h + 1) * MEM_HEAD_DIM] = heads[h]


def _merge_sample(x, a_n, ctx, w):
    m = x.shape[0]
    tm = TOK_TILE
    row = lambda width: pl.BlockSpec((tm, width), lambda i: (i, 0))
    c = lambda name: _const_spec(w[name].shape)
    return pl.pallas_call(
        _merge_sample_kernel, grid=(m // tm,),
        in_specs=[row(D_MODEL), row(A_WIDTH), row(ctx.shape[1]), c('w_uv_heads'), c('g_out_b'), c('w_o'),
                  c('g_mem_x'), c('w_mq'), c('g_mq')],
        out_specs=[row(D_MODEL), row(MEM_WIDTH)],
        out_shape=[jax.ShapeDtypeStruct((m, D_MODEL), F32), jax.ShapeDtypeStruct((m, MEM_WIDTH), BF16)],
        compiler_params=_params("arbitrary"),
        name="merge_sample",
    )(x, a_n, ctx, w['w_uv_heads'], w['g_out_b'], w['w_o'], w['g_mem_x'], w['w_mq'], w['g_mq'])


def _memattn_sample_kernel(q_ref, mk_ref, mv_ref, o_ref):
    nrow, nkey = q_ref.shape[1], mk_ref.shape[1]
    s = jnp.einsum('nqd,nkd->nqk', q_ref[...], mk_ref[...].astype(BF16), preferred_element_type=F32)
    same_head = (lax.broadcasted_iota(jnp.int32, (1, nrow, nkey), 1) % MEM_HEADS
                 == lax.broadcasted_iota(jnp.int32, (1, nrow, nkey), 2) % MEM_HEADS)
    s = jnp.where(same_head, s, NEG_INF)
    p = jnp.exp(s - jnp.max(s, axis=-1, keepdims=True))
    o = jnp.einsum('nqk,nkd->nqd', p.astype(BF16), mv_ref[...].astype(BF16), preferred_element_type=F32)
    o_ref[...] = (o / jnp.sum(p, axis=-1, keepdims=True)).astype(BF16)


def _memattn_sample(qm, mk, mv, first_seq):
    n_seq, nrow, _ = qm.shape
    nkey = mk.shape[1]
    ns = MEM_SEQ_TILE
    first_blk = first_seq // ns
    return pl.pallas_call(
        _memattn_sample_kernel, grid=(n_seq // ns,),
        in_specs=[pl.BlockSpec((ns, nrow, MEM_HEAD_DIM), lambda i: (i, 0, 0)),
                  pl.BlockSpec((ns, nkey, MEM_HEAD_DIM), lambda i: (first_blk + i, 0, 0)),
                  pl.BlockSpec((ns, nkey, MEM_HEAD_DIM), lambda i: (first_blk + i, 0, 0))],
        out_specs=pl.BlockSpec((ns, nrow, MEM_HEAD_DIM), lambda i: (i, 0, 0)),
        out_shape=jax.ShapeDtypeStruct((n_seq, nrow, MEM_HEAD_DIM), BF16),
        compiler_params=_params("arbitrary"),
        name="memattn_sample",
    )(qm, mk, mv)


def _ffn_kernel(*refs, sample, tiles_per_seq):
    if sample:
        (x_ref, o_ref, wmo_ref, gffn_ref, wup_ref, wd_ref, wc_ref, bc_ref, p1_ref, p2_ref,
         y_ref, gate_ref, h_s, g_s, act_s) = refs
        x2 = x_ref[...] + _dot(o_ref[...], wmo_ref[...])
        g_s[0:8, :] = jnp.zeros((8, D_FF), F32)
    else:
        (x_ref, gffn_ref, wup_ref, wd_ref, wc_ref, bc_ref, y_ref, tail_ref, h_s, g_s, act_s, carry_s) = refs
        x2 = x_ref[...]

        @pl.when(pl.program_id(0) == 0)
        def _():
            carry_s[...] = jnp.zeros_like(carry_s)
        g_s[0:8, :] = jnp.where((pl.program_id(0) % tiles_per_seq) == 0, 0.0, carry_s[...])
    tm = x_ref.shape[0]
    y_ref[...] = x2
    h_s[...] = (_rms(x2) * gffn_ref[...]).astype(BF16)
    if sample:
        t_in_seq = lax.broadcasted_iota(jnp.int32, (tm, 1), 0) % tiles_per_seq
    for f in range(N_FF_TILES):
        sl = slice(f * FF_TILE, (f + 1) * FF_TILE)
        h = h_s[...]
        g = _dot(h, wup_ref[:, sl])
        val = _dot(h, wup_ref[:, D_FF + f * FF_TILE:D_FF + (f + 1) * FF_TILE])
        g_s[8:, sl] = g
        g1 = g_s[7:7 + tm, sl]
        g2 = g_s[6:6 + tm, sl]
        if sample:
            gate_ref[:, sl] = g
            g1 = jnp.where(t_in_seq >= 1, g1, p1_ref[:, sl])
            g2 = jnp.where(t_in_seq >= 2, g2, p2_ref[:, sl])
        conv = bc_ref[:, sl] + g2 * wc_ref[0:1, sl] + g1 * wc_ref[1:2, sl] + g * wc_ref[2:3, sl]
        act_s[:, sl] = (conv / (1.0 + jnp.exp(-conv)) * val).astype(BF16)
    if not sample:
        tail = g_s[tm:tm + 8, :]
        carry_s[...] = tail
        tail_ref[0] = tail
    y_ref[...] += _dot(act_s[...], wd_ref[...])


def _ffn_scratch(tm):
    return [pltpu.VMEM((tm, D_MODEL), BF16), pltpu.VMEM((tm + 8, D_FF), F32), pltpu.VMEM((tm, D_FF), BF16)]


def _ffn_prompt(x2, w, n_batch, seq):
    m = x2.shape[0]
    tm = TOK_TILE
    per_seq = seq // tm
    row = pl.BlockSpec((tm, D_MODEL), lambda i: (i, 0))
    c = lambda name: _const_spec(w[name].shape)
    return pl.pallas_call(
        functools.partial(_ffn_kernel, sample=False, tiles_per_seq=per_seq), grid=(m // tm,),
        in_specs=[row, c('g_ffn'), c('w_up'), c('w_down'), c('w_conv'), c('b_conv')],
        out_specs=[row, pl.BlockSpec((1, 8, D_FF), lambda i: (i // per_seq, 0, 0))],
        out_shape=[jax.ShapeDtypeStruct((m, D_MODEL), F32), jax.ShapeDtypeStruct((n_batch, 8, D_FF), F32)],
        scratch_shapes=_ffn_scratch(tm) + [pltpu.VMEM((8, D_FF), F32)],
        compiler_params=_params("arbitrary"),
        name="ffn_prompt",
    )(x2, w['g_ffn'], w['w_up'], w['w_down'], w['w_conv'], w['b_conv'])


def _ffn_sample(x1, o, prev1, prev2, w, t_q):
    m = x1.shape[0]
    tm = TOK_TILE
    row = lambda width: pl.BlockSpec((tm, width), lambda i: (i, 0))
    c = lambda name: _const_spec(w[name].shape)
    return pl.pallas_call(
        functools.partial(_ffn_kernel, sample=True, tiles_per_seq=t_q), grid=(m // tm,),
        in_specs=[row(D_MODEL), row(MEM_WIDTH), c('w_mo'), c('g_ffn'), c('w_up'), c('w_down'),
                  c('w_conv'), c('b_conv'), row(D_FF), row(D_FF)],
        out_specs=[row(D_MODEL), row(D_FF)],
        out_shape=[jax.ShapeDtypeStruct((m, D_MODEL), F32), jax.ShapeDtypeStruct((m, D_FF), F32)],
        scratch_shapes=_ffn_scratch(tm),
        compiler_params=_params("arbitrary"),
        name="ffn_sample",
    )(x1, o, w['w_mo'], w['g_ffn'], w['w_up'], w['w_down'], w['w_conv'], w['b_conv'], prev1, prev2)


def _head_pad(wm, width):
    k = wm.shape[0]
    wm = wm.reshape(k, MLA_HEADS, width)
    return jnp.pad(wm, ((0, 0), (0, 0), (0, HEAD_PAD - width))).reshape(k, MLA_HEADS * HEAD_PAD)


def _rot_half_cols(wm):
    half = QK_ROPE // 2
    return jnp.concatenate([-wm[..., half:], wm[..., :half]], axis=-1)


def _lane_vec(nope, rope):
    return jnp.concatenate([nope, rope, jnp.zeros((LANES - QK_HEAD,), F32)])[None, :]


def _rope_tables(pos):
    half = QK_ROPE // 2
    inv_freq = ROPE_THETA ** (-jnp.arange(half, dtype=F32) / half)
    ang = pos.astype(F32)[:, None] * inv_freq[None, :]
    return jnp.cos(ang), jnp.sin(ang)


def _head_tile_tables(pos):
    cos, sin = _rope_tables(pos)
    n = pos.shape[0]
    cos_t = jnp.concatenate([jnp.ones((n, QK_NOPE), F32), cos, cos, jnp.zeros((n, LANES - QK_HEAD), F32)], axis=1)
    sin_t = jnp.concatenate([jnp.zeros((n, QK_NOPE), F32), sin, sin, jnp.zeros((n, LANES - QK_HEAD), F32)], axis=1)
    return cos_t, sin_t


def _feature_table(pos, cols):
    cos, sin = _rope_tables(pos)
    t = jnp.concatenate([cos, cos, sin, sin], axis=1).T
    return jnp.pad(t, ((0, 0), (0, cols - pos.shape[0])))


def _query_feature_perm():
    half = QK_ROPE // 2
    p = np.zeros((LANES, LANES), np.float32)
    for j in range(QK_ROPE):
        p[QK_NOPE + j, j] = 1.0
    for j in range(half):
        p[QK_NOPE + half + j, QK_ROPE + j] = 1.0
        p[QK_NOPE + j, QK_ROPE + half + j] = -1.0
    return jnp.asarray(p, BF16)


def _prep_weights(l, t_q, g_mix, w_in, ln_v_g, ln_v_b, w_s, b_s, g_q_a, w_uq, g_kv_a, w_uk, w_uv, g_qk_q, g_qk_k,
                  g_out_a, g_out_b, w_o, g_mem_x, g_mem_in, w_mq, w_mk, w_mv, g_mq, g_mk, w_mo, g_ffn, w_up,
                  w_conv, b_conv, w_down):
    half = QK_ROPE // 2
    o_kpe = 2 * A_WIDTH + Q_LORA + KV_LORA
    vec = lambda a: a.reshape(1, -1).astype(F32)
    w = {}
    wkpe = w_in[l][:, o_kpe:]
    lane_tile = lambda blk: jnp.pad(blk, ((0, 0), (QK_NOPE, LANES - QK_HEAD)))
    w['w_in'] = jnp.concatenate([w_in[l][:, :o_kpe], lane_tile(wkpe), lane_tile(_rot_half_cols(wkpe))],
                                axis=1).astype(BF16)
    w['g_mix'] = vec(g_mix[l])
    w['ln_g'] = vec(ln_v_g[l])
    w['ln_b'] = vec(ln_v_b[l])
    w['g_q_a'] = vec(g_q_a[l])
    w['g_kv_a'] = vec(g_kv_a[l])
    w['g_out_a'] = vec(g_out_a[l])
    w['g_out_b'] = vec(g_out_b[l])
    w['g_out_b_col'] = g_out_b[l][:, None]
    wq = w_uq[l].reshape(Q_LORA, MLA_HEADS, QK_HEAD)
    wq_rot = jnp.concatenate([jnp.zeros((Q_LORA, MLA_HEADS, QK_NOPE), F32), _rot_half_cols(wq[..., QK_NOPE:])], axis=-1)
    w['wq'] = _head_pad(wq.reshape(Q_LORA, -1), QK_HEAD).astype(BF16)
    w['wq_rot'] = _head_pad(wq_rot.reshape(Q_LORA, -1), QK_HEAD).astype(BF16)
    w['wk'] = _head_pad(w_uk[l], QK_NOPE).astype(BF16)
    w['wvt'] = w_uv[l].T.astype(BF16)
    gq, gk = g_qk_q[l], g_qk_k[l]
    sw = lambda g: jnp.concatenate([g[QK_NOPE + half:], g[QK_NOPE:QK_NOPE + half]])
    zeros_n = jnp.zeros((QK_NOPE,), F32)
    w['gq'] = _lane_vec(gq[:QK_NOPE], gq[QK_NOPE:])
    w['gq_sw'] = _lane_vec(zeros_n, sw(gq))
    w['gk'] = _lane_vec(gk[:QK_NOPE], gk[QK_NOPE:])
    w['gk_sw'] = _lane_vec(zeros_n, sw(gk))
    w['gk_nope'] = _lane_vec(gk[:QK_NOPE], jnp.zeros((QK_ROPE,), F32))
    w['pq'] = _query_feature_perm()
    w['wukt'] = w_uk[l].T.astype(BF16)
    gr = gk[QK_NOPE:]
    w['gcol'] = jnp.concatenate([gr, gr])[:, None]
    wuv = w_uv[l].reshape(KV_LORA, MLA_HEADS, V_HEAD)
    eye = jnp.eye(MLA_HEADS, dtype=F32)
    w['w_uv_heads'] = (wuv[None, :, :, :] * eye[:, None, :, None]).reshape(MLA_HEADS * KV_LORA, B_WIDTH).astype(BF16)
    tril = jnp.tril(jnp.ones((CHUNK, CHUNK), F32))
    w['ws_prompt'] = (w_s[l] * tril).astype(BF16)
    w['bs_prompt'] = jnp.repeat(b_s[l].T, A_GROUP_DIM, axis=1)
    reps = CHUNK // t_q
    blk = w_s[l][:, :t_q, :t_q] * jnp.tril(jnp.ones((t_q, t_q), F32))
    w['ws_sample'] = jnp.einsum('ij,gts->gitjs', jnp.eye(reps, dtype=F32), blk).reshape(A_GROUPS, CHUNK, CHUNK).astype(BF16)
    w['bs_sample'] = jnp.repeat(jnp.tile(b_s[l][:, :t_q].T, (reps, 1)), A_GROUP_DIM, axis=1)
    w['w_o'] = w_o[l].astype(BF16)
    w['g_mem_x'] = vec(g_mem_x[l])
    w['g_mem_in'] = vec(g_mem_in[l])
    w['w_mq'] = w_mq[l].astype(BF16)
    w['w_mk'] = w_mk[l].astype(BF16)
    w['w_mv'] = w_mv[l].astype(BF16)
    w['g_mq'] = vec(g_mq[l])
    w['g_mk'] = vec(g_mk[l])
    w['w_mo'] = w_mo[l].astype(BF16)
    w['g_ffn'] = vec(g_ffn[l])
    w['w_up'] = w_up[l].astype(BF16)
    w['w_down'] = w_down[l].astype(BF16)
    w['w_conv'] = w_conv[l]
    w['b_conv'] = vec(b_conv[l])
    return w


def kernel(x_prompt, x_sample, cache_ckv, cache_kpe, cache_mem_k, cache_mem_v, state_ffn_conv, page_table, mem_prompt, g_mix, w_in, ln_v_g, ln_v_b, w_s, b_s, g_q_a, w_uq, g_kv_a, w_uk, w_uv, g_qk_q, g_qk_k, g_out_a, g_out_b, w_o, g_mem_x, g_mem_in, w_mq, w_mk, w_mv, g_mq, g_mk, w_mo, g_ffn, w_up, w_conv, b_conv, w_down):
    n_p, seq, _ = x_prompt.shape
    n_s, t_q, _ = x_sample.shape
    depth = g_mix.shape[0]
    mem_len = mem_prompt.shape[1]
    n_pages = page_table.shape[1]
    past = n_pages * PAGE_SIZE

    pos_p = jnp.arange(seq)
    pos_s = past + jnp.arange(t_q)
    cos_p, sin_p = _head_tile_tables(pos_p)
    cos_s, sin_s = _head_tile_tables(jnp.tile(pos_s, TOK_TILE // t_q))
    n_kv_tiles = past // KV_TILE
    t_past = _feature_table(jnp.arange(past), past).reshape(2 * QK_ROPE, n_kv_tiles, KV_TILE).transpose(1, 0, 2)
    t_new = _feature_table(pos_s, LANES)
    n_phys = cache_ckv.shape[1]
    ckv_pages = cache_ckv.reshape(depth * n_phys, PAGE_SIZE, KV_LORA)
    kpe_pages_t = jnp.swapaxes(cache_kpe, -1, -2).reshape(depth * n_phys, QK_ROPE, PAGE_SIZE)
    mem_rows = mem_len * MEM_HEADS
    mem_k_rows = cache_mem_k.reshape(depth * n_s, mem_rows, MEM_HEAD_DIM)
    mem_v_rows = cache_mem_v.reshape(depth * n_s, mem_rows, MEM_HEAD_DIM)

    xp = x_prompt.reshape(n_p * seq, D_MODEL)
    xs = x_sample.reshape(n_s * t_q, D_MODEL)
    outs = {k: [] for k in ('p_ckv', 'p_kpe', 'p_mk', 'p_mv', 'p_conv', 's_ckv', 's_kpe', 's_chunk_v', 's_conv')}
    for l in range(depth):
        w = _prep_weights(l, t_q, g_mix, w_in, ln_v_g, ln_v_b, w_s, b_s, g_q_a, w_uq, g_kv_a, w_uk, w_uv, g_qk_q,
                          g_qk_k, g_out_a, g_out_b, w_o, g_mem_x, g_mem_in, w_mq, w_mk, w_mv, g_mq, g_mk, w_mo,
                          g_ffn, w_up, w_conv, b_conv, w_down)
        w['t_past'], w['t_new'] = t_past, t_new

        wp = dict(w, ws=w['ws_prompt'], bs=w['bs_prompt'])
        q, k, vt, ckvn, kpe, a_n = _inproj(xp, cos_p, sin_p, wp, sample=False)
        b_n = _flash(q, k, vt, w['g_out_b_col'], n_p, seq)
        mk, mv = _memkv(mem_prompt.reshape(n_p * mem_len, D_MODEL), w)
        mk3 = mk.reshape(n_p, mem_len, MEM_WIDTH)
        mv3 = mv.reshape(n_p, mem_len, MEM_WIDTH)
        x2 = _merge_prompt(xp, a_n, b_n, mk3, mv3, w, seq)
        xp, tail = _ffn_prompt(x2, w, n_p, seq)
        outs['p_ckv'].append(ckvn.reshape(n_p, seq, KV_LORA))
        outs['p_kpe'].append(kpe.reshape(n_p, seq, QK_ROPE))
        outs['p_mk'].append(mk.reshape(n_p, mem_len, MEM_HEADS, MEM_HEAD_DIM))
        outs['p_mv'].append(mv.reshape(n_p, mem_len, MEM_HEADS, MEM_HEAD_DIM))
        outs['p_conv'].append(tail[:, 8 - (CONV_W - 1):, :])

        wsm = dict(w, ws=w['ws_sample'], bs=w['bs_sample'])
        qlat, qrf, ckvn_s, kpe_s, a_ns, vg = _inproj(xs, cos_s, sin_s, wsm, sample=True)
        nq = t_q * MLA_HEADS
        kpnew_t = jnp.pad(jnp.swapaxes(kpe_s.reshape(n_s, t_q, QK_ROPE), 1, 2), ((0, 0), (0, 0), (0, LANES - t_q)))
        ctx = _paged(page_table + l * n_phys, qlat.reshape(n_s, nq, KV_LORA), qrf.reshape(n_s, nq, LANES),
                     ckvn_s.reshape(n_s, t_q, KV_LORA), kpnew_t, ckv_pages, kpe_pages_t, w)
        x1, qm = _merge_sample(xs, a_ns, ctx.reshape(n_s * t_q, MLA_HEADS * KV_LORA), w)
        o = _memattn_sample(qm.reshape(n_s, t_q * MEM_HEADS, MEM_HEAD_DIM), mem_k_rows, mem_v_rows, l * n_s)
        st = state_ffn_conv[l]
        zero = jnp.zeros_like(st[:, :1])
        prev1 = jnp.concatenate([st[:, 1:2]] + [zero] * (t_q - 1), axis=1)
        prev2 = jnp.concatenate([st[:, 0:1], st[:, 1:2]] + [zero] * (t_q - 2), axis=1)
        flat = lambda a: a.reshape(n_s * t_q, D_FF)
        xs, gate = _ffn_sample(x1, o.reshape(n_s * t_q, MEM_WIDTH), flat(prev1), flat(prev2), w, t_q)
        gate = gate.reshape(n_s, t_q, D_FF)
        outs['s_ckv'].append(ckvn_s.reshape(n_s, t_q, KV_LORA))
        outs['s_kpe'].append(kpe_s.reshape(n_s, t_q, QK_ROPE))
        outs['s_chunk_v'].append(vg.reshape(n_s, t_q, A_GROUPS, A_GROUP_DIM))
        outs['s_conv'].append(gate[:, t_q - (CONV_W - 1):, :])

    return (xp.reshape(n_p, seq, D_MODEL), xs.reshape(n_s, t_q, D_MODEL),
            jnp.stack(outs['p_ckv']), jnp.stack(outs['p_kpe']), jnp.stack(outs['p_mk']), jnp.stack(outs['p_mv']),
            jnp.stack(outs['p_conv']), jnp.stack(outs['s_ckv']), jnp.stack(outs['s_kpe']),
            jnp.stack(outs['s_chunk_v']), jnp.stack(outs['s_conv']))
```

```python
import functools

import numpy as np
import jax
import jax.numpy as jnp
from jax import lax
from jax.experimental import pallas as pl
from jax.experimental.pallas import tpu as pltpu

D_MODEL = 1024
A_GROUPS = 8
A_GROUP_DIM = 64
A_WIDTH = A_GROUPS * A_GROUP_DIM
CHUNK = 128
MLA_HEADS = 8
Q_LORA = 256
KV_LORA = 256
QK_NOPE = 64
QK_ROPE = 32
QK_HEAD = QK_NOPE + QK_ROPE
V_HEAD = 64
B_WIDTH = MLA_HEADS * V_HEAD
ROPE_THETA = 10000.0
PAGE_SIZE = 128
MEM_HEADS = 4
MEM_HEAD_DIM = 128
MEM_WIDTH = MEM_HEADS * MEM_HEAD_DIM
D_FF = 2816
CONV_W = 3
EPS = 1e-6
NEG_INF = -1e30

LANES = 128
HEAD_PAD = LANES
QK_SCALE = QK_HEAD ** -0.5
LOG2E = 1.4426950408889634
V_AUG = V_HEAD + 16
MEM_SCALE = MEM_HEAD_DIM ** -0.5

TOK_TILE = 512
FF_TILE = 256
N_FF_TILES = D_FF // FF_TILE
ATT_TILE = 256
KV_TILE = 1024
PAGED_BUFFERS = 3
MEM_SEQ_TILE = 8
VMEM_LIMIT = 56 * 1024 * 1024

F32 = jnp.float32
BF16 = jnp.bfloat16


def _dot(a, b):
    return jnp.dot(a, b, preferred_element_type=F32)


def _dot_nt(a, b):
    return lax.dot_general(a, b, (((1,), (1,)), ((), ())), preferred_element_type=F32)


def _rms(x):
    return x * lax.rsqrt(jnp.mean(x * x, axis=-1, keepdims=True) + EPS)


def _gelu(x):
    return jax.nn.gelu(x, approximate=True)


def _const_spec(shape):
    nd = len(shape)
    return pl.BlockSpec(shape, lambda *_: (0,) * nd, pipeline_mode=pl.Buffered(1))


def _params(*sem):
    return pltpu.CompilerParams(dimension_semantics=sem, vmem_limit_bytes=VMEM_LIMIT)


def _inproj_kernel(x_ref, gmix_ref, win_ref, lng_ref, lnb_ref, ws_ref, bs_ref, gqa_ref, wq_ref, wqr_ref,
                   gkva_ref, wk_ref, wv_ref, gq_ref, gqs_ref, gk_ref, gks_ref, cos_ref, sin_ref, goa_ref,
                   pq_ref, *rest, sample):
    if sample:
        qlat_ref, qrf_ref, ckvn_ref, kpe_ref, an_ref, vg_ref, zu_s, vgb_s, a_s = rest
    else:
        q_ref, k_ref, v_ref, ckvn_ref, kpe_ref, an_ref, zu_s, vgb_s, a_s = rest
    tm = x_ref.shape[0]
    xn = (_rms(x_ref[...]) * gmix_ref[...]).astype(BF16)

    o_cq = 2 * A_WIDTH
    o_ckv = o_cq + Q_LORA
    o_kpe = o_ckv + KV_LORA
    zv = _dot(xn, win_ref[:, A_WIDTH:2 * A_WIDTH])
    cq = _dot(xn, win_ref[:, o_cq:o_ckv])
    ckv = _dot(xn, win_ref[:, o_ckv:o_kpe])
    kpt = _dot(xn, win_ref[:, o_kpe:o_kpe + LANES])
    if not sample:
        kpr = _dot(xn, win_ref[:, o_kpe + LANES:o_kpe + 2 * LANES])
    zu_s[...] = _dot(xn, win_ref[:, 0:A_WIDTH])

    cqn = (_rms(cq) * gqa_ref[...]).astype(BF16)
    ckvn = _rms(ckv) * gkva_ref[...]
    ckvn_ref[...] = ckvn
    kpe_ref[...] = kpt[:, QK_NOPE:QK_HEAD]
    qall = _dot(cqn, wq_ref[...])
    qrall = _dot(cqn, wqr_ref[...])
    if not sample:
        cb = ckvn.astype(BF16)
        kall = _dot(cb, wk_ref[...])
        vt = _dot_nt(wv_ref[...], cb).astype(BF16)
        ones = jnp.ones((V_AUG - V_HEAD, ATT_TILE), BF16)
        for c in range(tm // ATT_TILE):
            cols = slice(c * ATT_TILE, (c + 1) * ATT_TILE)
            for h in range(MLA_HEADS):
                v_ref[c, h * V_AUG:h * V_AUG + V_HEAD, :] = vt[h * V_HEAD:(h + 1) * V_HEAD, cols]
                v_ref[c, h * V_AUG + V_HEAD:(h + 1) * V_AUG, :] = ones

    gv = _gelu(zv)
    low = lax.broadcasted_iota(jnp.int32, (1, LANES), 1) < A_GROUP_DIM
    inv_g = 1.0 / A_GROUP_DIM
    for p in range(A_WIDTH // LANES):
        sl = slice(p * LANES, (p + 1) * LANES)
        t = gv[:, sl]
        s_lo = jnp.sum(jnp.where(low, t, 0.0), axis=-1, keepdims=True)
        s_hi = jnp.sum(jnp.where(low, 0.0, t), axis=-1, keepdims=True)
        tc = t - jnp.where(low, s_lo, s_hi) * inv_g
        tc2 = tc * tc
        v_lo = jnp.sum(jnp.where(low, tc2, 0.0), axis=-1, keepdims=True)
        v_hi = jnp.sum(jnp.where(low, 0.0, tc2), axis=-1, keepdims=True)
        y = tc * lax.rsqrt(jnp.where(low, v_lo, v_hi) * inv_g + EPS) * lng_ref[:, sl] + lnb_ref[:, sl]
        if sample:
            vg_ref[:, sl] = y
        vgb_s[:, sl] = y.astype(BF16)
    for c in range(tm // CHUNK):
        rows = slice(c * CHUNK, (c + 1) * CHUNK)
        for p in range(A_WIDTH // LANES):
            sl = slice(p * LANES, (p + 1) * LANES)
            vp = vgb_s[rows, sl]
            mixed = jnp.where(low, _dot(ws_ref[2 * p], vp), _dot(ws_ref[2 * p + 1], vp)) + bs_ref[:, sl]
            a_s[rows, sl] = _gelu(zu_s[rows, sl]) * mixed
    an_ref[...] = (_rms(a_s[...]) * goa_ref[...]).astype(BF16)

    cosv = cos_ref[...]
    sinv = sin_ref[...]
    q_scale = QK_SCALE if sample else QK_SCALE * LOG2E
    qa = gq_ref[...] * cosv * q_scale
    qb = gqs_ref[...] * sinv * q_scale
    inv_d = 1.0 / QK_HEAD
    if not sample:
        ka = gk_ref[...] * cosv
        kb = gks_ref[...] * sinv
    for h in range(MLA_HEADS):
        sl = slice(h * HEAD_PAD, (h + 1) * HEAD_PAD)
        qh = qall[:, sl]
        r = lax.rsqrt(jnp.sum(qh * qh, axis=-1, keepdims=True) * inv_d + EPS)
        qf = r * (qh * qa + qrall[:, sl] * qb)
        if sample:
            qg = (qf * gk_ref[...]).astype(BF16)
            qlat_ref[:, h * KV_LORA:(h + 1) * KV_LORA] = _dot_nt(qg, wk_ref[:, sl]).astype(BF16)
            qrf_ref[:, sl] = _dot(qf.astype(BF16), pq_ref[...]).astype(BF16)
        else:
            q_ref[:, sl] = qf.astype(BF16)
            kh = kall[:, sl] + kpt
            rk = lax.rsqrt(jnp.sum(kh * kh, axis=-1, keepdims=True) * inv_d + EPS)
            k_ref[:, sl] = (rk * (kh * ka + kpr * kb)).astype(BF16)


def _inproj(x, cos, sin, w, *, sample):
    m = x.shape[0]
    tm = TOK_TILE
    n_tiles = m // tm
    n_pos_tiles = cos.shape[0] // tm
    row = lambda width: pl.BlockSpec((tm, width), lambda i: (i, 0))
    pos = pl.BlockSpec((tm, LANES), lambda i: (i % n_pos_tiles, 0))
    consts = [w['g_mix'], w['w_in'], w['ln_g'], w['ln_b'], w['ws'], w['bs'], w['g_q_a'], w['wq'], w['wq_rot'],
              w['g_kv_a'], w['wk'], w['wvt'], w['gq'], w['gq_sw'], w['gk_nope'] if sample else w['gk'], w['gk_sw']]
    in_specs = ([row(D_MODEL)] + [_const_spec(c.shape) for c in consts] + [pos, pos]
                + [_const_spec(w['g_out_a'].shape), _const_spec(w['pq'].shape)])
    args = [x] + consts + [cos, sin, w['g_out_a'], w['pq']]
    wide = MLA_HEADS * HEAD_PAD
    if sample:
        out_shape = [jax.ShapeDtypeStruct((m, MLA_HEADS * KV_LORA), BF16), jax.ShapeDtypeStruct((m, wide), BF16),
                     jax.ShapeDtypeStruct((m, KV_LORA), F32), jax.ShapeDtypeStruct((m, QK_ROPE), F32),
                     jax.ShapeDtypeStruct((m, A_WIDTH), BF16), jax.ShapeDtypeStruct((m, A_WIDTH), F32)]
        out_specs = [row(MLA_HEADS * KV_LORA), row(wide), row(KV_LORA), row(QK_ROPE), row(A_WIDTH), row(A_WIDTH)]
    else:
        att_tiles = tm // ATT_TILE
        out_shape = [jax.ShapeDtypeStruct((m, wide), BF16)] * 2 + [
            jax.ShapeDtypeStruct((m // ATT_TILE, MLA_HEADS * V_AUG, ATT_TILE), BF16),
            jax.ShapeDtypeStruct((m, KV_LORA), F32), jax.ShapeDtypeStruct((m, QK_ROPE), F32),
            jax.ShapeDtypeStruct((m, A_WIDTH), BF16)]
        out_specs = [row(wide)] * 2 + [pl.BlockSpec((att_tiles, MLA_HEADS * V_AUG, ATT_TILE), lambda i: (i, 0, 0)),
                                       row(KV_LORA), row(QK_ROPE), row(A_WIDTH)]
    return pl.pallas_call(
        functools.partial(_inproj_kernel, sample=sample),
        grid=(n_tiles,), in_specs=in_specs, out_specs=out_specs, out_shape=out_shape,
        scratch_shapes=[pltpu.VMEM((tm, A_WIDTH), F32), pltpu.VMEM((tm, A_WIDTH), BF16),
                        pltpu.VMEM((tm, A_WIDTH), F32)],
        compiler_params=_params("arbitrary"),
        name="inproj_sample" if sample else "inproj_prompt",
    )(*args)


def _flash_kernel(q_ref, k_ref, vt_ref, gob_ref, o_ref):
    i = pl.program_id(1)
    t = ATT_TILE
    causal = lax.broadcasted_iota(jnp.int32, (t, t), 0) <= lax.broadcasted_iota(jnp.int32, (t, t), 1)

    def step(j, carry, masked):
        start = pl.multiple_of(j * t, t)
        vt = vt_ref[j]
        heads = [slice(h * HEAD_PAD, (h + 1) * HEAD_PAD) for h in range(MLA_HEADS)]
        scores = [_dot_nt(k_ref[pl.ds(start, t), sl], q_ref[:, sl]) for sl in heads]
        stats = []
        for h in range(MLA_HEADS):
            m = carry[h][0]
            s = jnp.where(causal, scores[h], NEG_INF) if masked else scores[h]
            m_new = jnp.maximum(m, jnp.max(s, axis=0, keepdims=True))
            stats.append((m_new, jnp.exp2(m - m_new), jnp.exp2(s - m_new).astype(BF16)))
        new = []
        for h, (m_new, alpha, p) in enumerate(stats):
            pv = _dot(vt[h * V_AUG:(h + 1) * V_AUG, :], p)
            _, l, acc = carry[h]
            new.append((m_new, alpha * l + pv[V_HEAD:V_HEAD + 1], alpha * acc + pv[:V_HEAD]))
        return tuple(new)

    init = tuple((jnp.full((1, t), NEG_INF, F32), jnp.zeros((1, t), F32), jnp.zeros((V_HEAD, t), F32))
                 for _ in range(MLA_HEADS))
    carry = lax.fori_loop(0, i, functools.partial(step, masked=False), init)
    carry = step(i, carry, True)
    o_t = jnp.concatenate([acc / l for _, l, acc in carry], axis=0)
    bn_t = o_t * lax.rsqrt(jnp.mean(o_t * o_t, axis=0, keepdims=True) + EPS) * gob_ref[...]
    o_ref[...] = bn_t.T.astype(BF16)


def _flash(q, k, vt, g_out_b_col, n_batch, seq):
    m = q.shape[0]
    nq = seq // ATT_TILE
    wide = MLA_HEADS * HEAD_PAD
    return pl.pallas_call(
        _flash_kernel,
        grid=(n_batch, nq),
        in_specs=[pl.BlockSpec((ATT_TILE, wide), lambda b, i: (b * nq + i, 0)),
                  pl.BlockSpec((seq, wide), lambda b, i: (b, 0)),
                  pl.BlockSpec((nq, MLA_HEADS * V_AUG, ATT_TILE), lambda b, i: (b, 0, 0)),
                  pl.BlockSpec((B_WIDTH, 1), lambda b, i: (0, 0))],
        out_specs=pl.BlockSpec((ATT_TILE, B_WIDTH), lambda b, i: (b * nq + i, 0)),
        out_shape=jax.ShapeDtypeStruct((m, B_WIDTH), BF16),
        compiler_params=_params("arbitrary", "arbitrary"),
        name="flash_prompt",
    )(q, k, vt, g_out_b_col)


def _paged_kernel(pt_ref, qlat_ref, qrf_ref, qlat_next_ref, qrf_next_ref, cnew_ref, kpnew_ref, wukt_ref, gcol_ref,
                  tpast_ref, tnew_ref, ckv_hbm, kpe_hbm, ctx_ref, cbuf, pbuf, cn_s, tg_s, s0_s, sems, *, n_seq, n_pages):
    n = pl.program_id(0)
    slot = n % PAGED_BUFFERS
    ppt = KV_TILE // PAGE_SIZE
    n_tiles = n_pages // ppt

    def start_fetch(seq, sl, first=0, last=n_pages):
        for p in range(first, last):
            page = pt_ref[seq, p]
            pltpu.make_async_copy(ckv_hbm.at[page], cbuf.at[sl, p], sems.at[0, sl]).start()
            pltpu.make_async_copy(kpe_hbm.at[page], pbuf.at[sl, p], sems.at[1, sl]).start()

    def wait_fetch(sl):
        pltpu.make_async_copy(ckv_hbm.at[pl.ds(0, n_pages)], cbuf.at[sl], sems.at[0, sl]).wait()
        pltpu.make_async_copy(kpe_hbm.at[pl.ds(0, n_pages)], pbuf.at[sl], sems.at[1, sl]).wait()

    nq = qlat_ref.shape[1]
    t_q = nq // MLA_HEADS
    n_nope = MLA_HEADS * QK_NOPE
    n_feat = 3 * QK_ROPE
    ones_rows = jnp.where(lax.broadcasted_iota(jnp.int32, (16, n_feat), 1) >= 2 * QK_ROPE, 1.0, 0.0).astype(BF16)

    def query_mats(ql_ref, qr_ref):
        a_mat = jnp.concatenate([wukt_ref[...], ql_ref[0]], axis=0)
        l_mat = jnp.concatenate([qr_ref[0][:, :n_feat], ones_rows], axis=0)
        return a_mat, l_mat

    def scores(mats, cb, kpt, tg):
        a_mat, l_mat = mats
        tk = cb.shape[0]
        m1 = _dot_nt(a_mat, cb)
        knt = m1[:n_nope]
        ssq = jnp.sum((knt * knt).reshape(MLA_HEADS, QK_NOPE, tk), axis=1)
        feat = jnp.concatenate([kpt * tg[:QK_ROPE], kpt * tg[QK_ROPE:], kpt * kpt], axis=0)
        m2 = _dot(l_mat, feat.astype(BF16))
        r = lax.rsqrt((ssq + m2[nq:nq + MLA_HEADS]) * (1.0 / QK_HEAD) + EPS)
        s = (m1[n_nope:] + m2[:nq]).reshape(t_q, MLA_HEADS, tk) * r[None]
        return s.reshape(nq, tk)

    def update(carry, s, cb):
        m, l, acc = carry
        m_new = jnp.maximum(m, jnp.max(s, axis=-1, keepdims=True))
        alpha = jnp.exp(m - m_new)
        p = jnp.exp(s - m_new)
        l = alpha * l + jnp.sum(p, axis=-1, keepdims=True)
        acc = alpha * acc + _dot(p.astype(BF16), cb)
        return m_new, l, acc

    def tile_scores(mats, sl, t):
        cb = cbuf[sl, t * ppt:(t + 1) * ppt].reshape(KV_TILE, KV_LORA).astype(BF16)
        kpt = jnp.concatenate([pbuf[sl, t * ppt + i] for i in range(ppt)], axis=1)
        return scores(mats, cb, kpt, tg_s[t]), cb

    @pl.when(n == 0)
    def _():
        cn_s[...] = jnp.zeros_like(cn_s)
        for t in range(n_tiles):
            tg_s[t] = tpast_ref[t] * gcol_ref[...]
        start_fetch(0, 0)
        start_fetch(min(1, n_seq - 1), 1)
        wait_fetch(0)
        s0_s[...] = tile_scores(query_mats(qlat_ref, qrf_ref), 0, 0)[0]

    nslot = (n + 1) % PAGED_BUFFERS
    fslot = (n + 2) % PAGED_BUFFERS
    wait_fetch(nslot)
    cur = query_mats(qlat_ref, qrf_ref)
    fseq = jnp.minimum(n + 2, n_seq - 1)
    per_tile = -(-n_pages // n_tiles)
    cn_s[0:t_q, :] = cnew_ref[0]
    cb_new = cn_s[...].astype(BF16)
    s_new = scores(cur, cb_new, kpnew_ref[0], tnew_ref[...] * gcol_ref[...])
    key = lax.broadcasted_iota(jnp.int32, (nq, LANES), 1)
    qry = lax.broadcasted_iota(jnp.int32, (nq, LANES), 0) // MLA_HEADS
    s_new = jnp.where(key <= qry, s_new, NEG_INF)
    s_cur = s0_s[...]
    cb_cur = cbuf[slot, 0:ppt].reshape(KV_TILE, KV_LORA).astype(BF16)
    stats = (jnp.full((nq, 1), NEG_INF, F32), jnp.zeros((nq, 1), F32), jnp.zeros((nq, KV_LORA), F32))
    for t in range(n_tiles):
        start_fetch(fseq, fslot, min(t * per_tile, n_pages), min((t + 1) * per_tile, n_pages))
        if t + 1 < n_tiles:
            s_next, cb_next = tile_scores(cur, slot, t + 1)
        else:
            s_next, cb_next = tile_scores(query_mats(qlat_next_ref, qrf_next_ref), nslot, 0)
        stats = update(stats, s_cur, cb_cur)
        s_cur, cb_cur = s_next, cb_next
    s0_s[...] = s_cur
    m, l, acc = update(stats, s_new, cb_new)
    ctx_ref[0] = acc / l

    @pl.when(n == n_seq - 1)
    def _():
        wait_fetch(fslot)


def _paged(page_table, qlat, qrf, cnew, kpnew_t, cache_ckv, cache_kpe_t, w):
    n_seq, n_pages = page_table.shape
    nq = qlat.shape[1]
    t_q = cnew.shape[1]
    seq_spec = lambda shape: pl.BlockSpec((1,) + shape, lambda n, pt: (n, 0, 0))
    next_spec = lambda shape: pl.BlockSpec((1,) + shape, lambda n, pt: (jnp.minimum(n + 1, n_seq - 1), 0, 0))
    const = lambda shape: pl.BlockSpec(shape, lambda n, pt: (0,) * len(shape), pipeline_mode=pl.Buffered(1))
    grid_spec = pltpu.PrefetchScalarGridSpec(
        num_scalar_prefetch=1, grid=(n_seq,),
        in_specs=[seq_spec((nq, KV_LORA)), seq_spec((nq, LANES)), next_spec((nq, KV_LORA)), next_spec((nq, LANES)),
                  seq_spec((t_q, KV_LORA)), seq_spec((QK_ROPE, LANES)),
                  const(w['wukt'].shape), const(w['gcol'].shape), const(w['t_past'].shape), const(w['t_new'].shape),
                  pl.BlockSpec(memory_space=pl.ANY), pl.BlockSpec(memory_space=pl.ANY)],
        out_specs=seq_spec((nq, KV_LORA)),
        scratch_shapes=[pltpu.VMEM((PAGED_BUFFERS, n_pages, PAGE_SIZE, KV_LORA), F32),
                        pltpu.VMEM((PAGED_BUFFERS, n_pages, QK_ROPE, PAGE_SIZE), F32),
                        pltpu.VMEM((LANES, KV_LORA), F32), pltpu.VMEM(w['t_past'].shape, F32),
                        pltpu.VMEM((nq, KV_TILE), F32), pltpu.SemaphoreType.DMA((2, PAGED_BUFFERS))])
    return pl.pallas_call(
        functools.partial(_paged_kernel, n_seq=n_seq, n_pages=n_pages),
        grid_spec=grid_spec,
        out_shape=jax.ShapeDtypeStruct((n_seq, nq, KV_LORA), F32),
        compiler_params=_params("arbitrary"),
        name="paged_sample",
    )(page_table, qlat, qrf, qlat, qrf, cnew, kpnew_t, w['wukt'], w['gcol'], w['t_past'], w['t_new'],
      cache_ckv, cache_kpe_t)


def _memkv_kernel(mem_ref, gin_ref, wmk_ref, wmv_ref, gmk_ref, k_ref, v_ref):
    hm = (_rms(mem_ref[...]) * gin_ref[...]).astype(BF16)
    kk = _dot(hm, wmk_ref[...])
    for h in range(MEM_HEADS):
        sl = slice(h * MEM_HEAD_DIM, (h + 1) * MEM_HEAD_DIM)
        k_ref[:, sl] = _rms(kk[:, sl]) * gmk_ref[...]
    v_ref[...] = _dot(hm, wmv_ref[...])


def _memkv(mem, w):
    m = mem.shape[0]
    tm = TOK_TILE
    row = lambda width: pl.BlockSpec((tm, width), lambda i: (i, 0))
    consts = [w['g_mem_in'], w['w_mk'], w['w_mv'], w['g_mk']]
    return pl.pallas_call(
        _memkv_kernel, grid=(m // tm,),
        in_specs=[row(D_MODEL)] + [_const_spec(c.shape) for c in consts],
        out_specs=[row(MEM_WIDTH), row(MEM_WIDTH)],
        out_shape=[jax.ShapeDtypeStruct((m, MEM_WIDTH), F32)] * 2,
        compiler_params=_params("arbitrary"),
        name="memkv_prompt",
    )(mem, *consts)


def _mem_query(x1, gmx_ref, wmq_ref, gmq_ref):
    qm = _dot((_rms(x1) * gmx_ref[...]).astype(BF16), wmq_ref[...])
    heads = []
    for h in range(MEM_HEADS):
        sl = slice(h * MEM_HEAD_DIM, (h + 1) * MEM_HEAD_DIM)
        heads.append((_rms(qm[:, sl]) * (gmq_ref[...] * MEM_SCALE)).astype(BF16))
    return heads


def _merge_prompt_kernel(x_ref, an_ref, bn_ref, wo_ref, gmx_ref, wmq_ref, gmq_ref, mk_ref, mv_ref, wmo_ref,
                         y_ref, o_s):
    x1 = x_ref[...] + _dot(an_ref[...], wo_ref[0:A_WIDTH, :]) + _dot(bn_ref[...], wo_ref[A_WIDTH:, :])
    heads = _mem_query(x1, gmx_ref, wmq_ref, gmq_ref)
    lanes = [slice(h * MEM_HEAD_DIM, (h + 1) * MEM_HEAD_DIM) for h in range(MEM_HEADS)]
    scores = [_dot_nt(heads[h], mk_ref[0, :, sl].astype(BF16)) for h, sl in enumerate(lanes)]
    probs = [jnp.exp(s - jnp.max(s, axis=-1, keepdims=True)) for s in scores]
    for p, sl in zip(probs, lanes):
        o = _dot(p.astype(BF16), mv_ref[0, :, sl].astype(BF16)) / jnp.sum(p, axis=-1, keepdims=True)
        o_s[:, sl] = o.astype(BF16)
    y_ref[...] = x1 + _dot(o_s[...], wmo_ref[...])


def _merge_prompt(x, a_n, b_n, mk, mv, w, seq):
    m = x.shape[0]
    tm = TOK_TILE
    per_seq = seq // tm
    mem_len = mk.shape[1]
    row = lambda width: pl.BlockSpec((tm, width), lambda i: (i, 0))
    mem = pl.BlockSpec((1, mem_len, MEM_WIDTH), lambda i: (i // per_seq, 0, 0))
    c = lambda name: _const_spec(w[name].shape)
    return pl.pallas_call(
        _merge_prompt_kernel, grid=(m // tm,),
        in_specs=[row(D_MODEL), row(A_WIDTH), row(B_WIDTH), c('w_o'), c('g_mem_x'), c('w_mq'),
                  c('g_mq'), mem, mem, c('w_mo')],
        out_specs=row(D_MODEL),
        out_shape=jax.ShapeDtypeStruct((m, D_MODEL), F32),
        scratch_shapes=[pltpu.VMEM((tm, MEM_WIDTH), BF16)],
        compiler_params=_params("arbitrary"),
        name="merge_prompt",
    )(x, a_n, b_n, w['w_o'], w['g_mem_x'], w['w_mq'], w['g_mq'], mk, mv, w['w_mo'])


def _merge_sample_kernel(x_ref, an_ref, ctx_ref, wuvb_ref, gob_ref, wo_ref, gmx_ref, wmq_ref, gmq_ref,
                         x1_ref, qm_ref):
    b = _dot(ctx_ref[...].astype(BF16), wuvb_ref[...])
    bn = (_rms(b) * gob_ref[...]).astype(BF16)
    x1 = x_ref[...] + _dot(an_ref[...], wo_ref[0:A_WIDTH, :]) + _dot(bn, wo_ref[A_WIDTH:, :])
    x1_ref[...] = x1
    heads = _mem_query(x1, gmx_ref, wmq_ref, gmq_ref)
    for h in range(MEM_HEADS):
        qm_ref[:, h * MEM_HEAD_DIM:(h + 1) * MEM_HEAD_DIM] = heads[h]


def _merge_sample(x, a_n, ctx, w):
    m = x.shape[0]
    tm = TOK_TILE
    row = lambda width: pl.BlockSpec((tm, width), lambda i: (i, 0))
    c = lambda name: _const_spec(w[name].shape)
    return pl.pallas_call(
        _merge_sample_kernel, grid=(m // tm,),
        in_specs=[row(D_MODEL), row(A_WIDTH), row(ctx.shape[1]), c('w_uv_heads'), c('g_out_b'), c('w_o'),
                  c('g_mem_x'), c('w_mq'), c('g_mq')],
        out_specs=[row(D_MODEL), row(MEM_WIDTH)],
        out_shape=[jax.ShapeDtypeStruct((m, D_MODEL), F32), jax.ShapeDtypeStruct((m, MEM_WIDTH), BF16)],
        compiler_params=_params("arbitrary"),
        name="merge_sample",
    )(x, a_n, ctx, w['w_uv_heads'], w['g_out_b'], w['w_o'], w['g_mem_x'], w['w_mq'], w['g_mq'])


def _memattn_sample_kernel(q_ref, mk_ref, mv_ref, o_ref):
    nrow, nkey = q_ref.shape[1], mk_ref.shape[1]
    s = jnp.einsum('nqd,nkd->nqk', q_ref[...], mk_ref[...].astype(BF16), preferred_element_type=F32)
    same_head = (lax.broadcasted_iota(jnp.int32, (1, nrow, nkey), 1) % MEM_HEADS
                 == lax.broadcasted_iota(jnp.int32, (1, nrow, nkey), 2) % MEM_HEADS)
    s = jnp.where(same_head, s, NEG_INF)
    p = jnp.exp(s - jnp.max(s, axis=-1, keepdims=True))
    o = jnp.einsum('nqk,nkd->nqd', p.astype(BF16), mv_ref[...].astype(BF16), preferred_element_type=F32)
    o_ref[...] = (o / jnp.sum(p, axis=-1, keepdims=True)).astype(BF16)


def _memattn_sample(qm, mk, mv, first_seq):
    n_seq, nrow, _ = qm.shape
    nkey = mk.shape[1]
    ns = MEM_SEQ_TILE
    first_blk = first_seq // ns
    return pl.pallas_call(
        _memattn_sample_kernel, grid=(n_seq // ns,),
        in_specs=[pl.BlockSpec((ns, nrow, MEM_HEAD_DIM), lambda i: (i, 0, 0)),
                  pl.BlockSpec((ns, nkey, MEM_HEAD_DIM), lambda i: (first_blk + i, 0, 0)),
                  pl.BlockSpec((ns, nkey, MEM_HEAD_DIM), lambda i: (first_blk + i, 0, 0))],
        out_specs=pl.BlockSpec((ns, nrow, MEM_HEAD_DIM), lambda i: (i, 0, 0)),
        out_shape=jax.ShapeDtypeStruct((n_seq, nrow, MEM_HEAD_DIM), BF16),
        compiler_params=_params("arbitrary"),
        name="memattn_sample",
    )(qm, mk, mv)


def _ffn_kernel(*refs, sample, tiles_per_seq):
    if sample:
        (x_ref, o_ref, wmo_ref, gffn_ref, wup_ref, wd_ref, wc_ref, bc_ref, p1_ref, p2_ref,
         y_ref, gate_ref, h_s, g_s, act_s) = refs
        x2 = x_ref[...] + _dot(o_ref[...], wmo_ref[...])
        g_s[0:8, :] = jnp.zeros((8, D_FF), F32)
    else:
        (x_ref, gffn_ref, wup_ref, wd_ref, wc_ref, bc_ref, y_ref, tail_ref, h_s, g_s, act_s, carry_s) = refs
        x2 = x_ref[...]

        @pl.when(pl.program_id(0) == 0)
        def _():
            carry_s[...] = jnp.zeros_like(carry_s)
        g_s[0:8, :] = jnp.where((pl.program_id(0) % tiles_per_seq) == 0, 0.0, carry_s[...])
    tm = x_ref.shape[0]
    y_ref[...] = x2
    h_s[...] = (_rms(x2) * gffn_ref[...]).astype(BF16)
    if sample:
        t_in_seq = lax.broadcasted_iota(jnp.int32, (tm, 1), 0) % tiles_per_seq
    for f in range(N_FF_TILES):
        sl = slice(f * FF_TILE, (f + 1) * FF_TILE)
        h = h_s[...]
        g = _dot(h, wup_ref[:, sl])
        val = _dot(h, wup_ref[:, D_FF + f * FF_TILE:D_FF + (f + 1) * FF_TILE])
        g_s[8:, sl] = g
        g1 = g_s[7:7 + tm, sl]
        g2 = g_s[6:6 + tm, sl]
        if sample:
            gate_ref[:, sl] = g
            g1 = jnp.where(t_in_seq >= 1, g1, p1_ref[:, sl])
            g2 = jnp.where(t_in_seq >= 2, g2, p2_ref[:, sl])
        conv = bc_ref[:, sl] + g2 * wc_ref[0:1, sl] + g1 * wc_ref[1:2, sl] + g * wc_ref[2:3, sl]
        act_s[:, sl] = (conv / (1.0 + jnp.exp(-conv)) * val).astype(BF16)
    if not sample:
        tail = g_s[tm:tm + 8, :]
        carry_s[...] = tail
        tail_ref[0] = tail
    y_ref[...] += _dot(act_s[...], wd_ref[...])


def _ffn_scratch(tm):
    return [pltpu.VMEM((tm, D_MODEL), BF16), pltpu.VMEM((tm + 8, D_FF), F32), pltpu.VMEM((tm, D_FF), BF16)]


def _ffn_prompt(x2, w, n_batch, seq):
    m = x2.shape[0]
    tm = TOK_TILE
    per_seq = seq // tm
    row = pl.BlockSpec((tm, D_MODEL), lambda i: (i, 0))
    c = lambda name: _const_spec(w[name].shape)
    return pl.pallas_call(
        functools.partial(_ffn_kernel, sample=False, tiles_per_seq=per_seq), grid=(m // tm,),
        in_specs=[row, c('g_ffn'), c('w_up'), c('w_down'), c('w_conv'), c('b_conv')],
        out_specs=[row, pl.BlockSpec((1, 8, D_FF), lambda i: (i // per_seq, 0, 0))],
        out_shape=[jax.ShapeDtypeStruct((m, D_MODEL), F32), jax.ShapeDtypeStruct((n_batch, 8, D_FF), F32)],
        scratch_shapes=_ffn_scratch(tm) + [pltpu.VMEM((8, D_FF), F32)],
        compiler_params=_params("arbitrary"),
        name="ffn_prompt",
    )(x2, w['g_ffn'], w['w_up'], w['w_down'], w['w_conv'], w['b_conv'])


def _ffn_sample(x1, o, prev1, prev2, w, t_q):
    m = x1.shape[0]
    tm = TOK_TILE
    row = lambda width: pl.BlockSpec((tm, width), lambda i: (i, 0))
    c = lambda name: _const_spec(w[name].shape)
    return pl.pallas_call(
        functools.partial(_ffn_kernel, sample=True, tiles_per_seq=t_q), grid=(m // tm,),
        in_specs=[row(D_MODEL), row(MEM_WIDTH), c('w_mo'), c('g_ffn'), c('w_up'), c('w_down'),
                  c('w_conv'), c('b_conv'), row(D_FF), row(D_FF)],
        out_specs=[row(D_MODEL), row(D_FF)],
        out_shape=[jax.ShapeDtypeStruct((m, D_MODEL), F32), jax.ShapeDtypeStruct((m, D_FF), F32)],
        scratch_shapes=_ffn_scratch(tm),
        compiler_params=_params("arbitrary"),
        name="ffn_sample",
    )(x1, o, w['w_mo'], w['g_ffn'], w['w_up'], w['w_down'], w['w_conv'], w['b_conv'], prev1, prev2)


def _head_pad(wm, width):
    k = wm.shape[0]
    wm = wm.reshape(k, MLA_HEADS, width)
    return jnp.pad(wm, ((0, 0), (0, 0), (0, HEAD_PAD - width))).reshape(k, MLA_HEADS * HEAD_PAD)


def _rot_half_cols(wm):
    half = QK_ROPE // 2
    return jnp.concatenate([-wm[..., half:], wm[..., :half]], axis=-1)


def _lane_vec(nope, rope):
    return jnp.concatenate([nope, rope, jnp.zeros((LANES - QK_HEAD,), F32)])[None, :]


def _rope_tables(pos):
    half = QK_ROPE // 2
    inv_freq = ROPE_THETA ** (-jnp.arange(half, dtype=F32) / half)
    ang = pos.astype(F32)[:, None] * inv_freq[None, :]
    return jnp.cos(ang), jnp.sin(ang)


def _head_tile_tables(pos):
    cos, sin = _rope_tables(pos)
    n = pos.shape[0]
    cos_t = jnp.concatenate([jnp.ones((n, QK_NOPE), F32), cos, cos, jnp.zeros((n, LANES - QK_HEAD), F32)], axis=1)
    sin_t = jnp.concatenate([jnp.zeros((n, QK_NOPE), F32), sin, sin, jnp.zeros((n, LANES - QK_HEAD), F32)], axis=1)
    return cos_t, sin_t


def _feature_table(pos, cols):
    cos, sin = _rope_tables(pos)
    t = jnp.concatenate([cos, cos, sin, sin], axis=1).T
    return jnp.pad(t, ((0, 0), (0, cols - pos.shape[0])))


def _query_feature_perm():
    half = QK_ROPE // 2
    p = np.zeros((LANES, LANES), np.float32)
    for j in range(QK_ROPE):
        p[QK_NOPE + j, j] = 1.0
    for j in range(half):
        p[QK_NOPE + half + j, QK_ROPE + j] = 1.0
        p[QK_NOPE + j, QK_ROPE + half + j] = -1.0
    return jnp.asarray(p, BF16)


def _prep_weights(l, t_q, g_mix, w_in, ln_v_g, ln_v_b, w_s, b_s, g_q_a, w_uq, g_kv_a, w_uk, w_uv, g_qk_q, g_qk_k,
                  g_out_a, g_out_b, w_o, g_mem_x, g_mem_in, w_mq, w_mk, w_mv, g_mq, g_mk, w_mo, g_ffn, w_up,
                  w_conv, b_conv, w_down):
    half = QK_ROPE // 2
    o_kpe = 2 * A_WIDTH + Q_LORA + KV_LORA
    vec = lambda a: a.reshape(1, -1).astype(F32)
    w = {}
    wkpe = w_in[l][:, o_kpe:]
    lane_tile = lambda blk: jnp.pad(blk, ((0, 0), (QK_NOPE, LANES - QK_HEAD)))
    w['w_in'] = jnp.concatenate([w_in[l][:, :o_kpe], lane_tile(wkpe), lane_tile(_rot_half_cols(wkpe))],
                                axis=1).astype(BF16)
    w['g_mix'] = vec(g_mix[l])
    w['ln_g'] = vec(ln_v_g[l])
    w['ln_b'] = vec(ln_v_b[l])
    w['g_q_a'] = vec(g_q_a[l])
    w['g_kv_a'] = vec(g_kv_a[l])
    w['g_out_a'] = vec(g_out_a[l])
    w['g_out_b'] = vec(g_out_b[l])
    w['g_out_b_col'] = g_out_b[l][:, None]
    wq = w_uq[l].reshape(Q_LORA, MLA_HEADS, QK_HEAD)
    wq_rot = jnp.concatenate([jnp.zeros((Q_LORA, MLA_HEADS, QK_NOPE), F32), _rot_half_cols(wq[..., QK_NOPE:])], axis=-1)
    w['wq'] = _head_pad(wq.reshape(Q_LORA, -1), QK_HEAD).astype(BF16)
    w['wq_rot'] = _head_pad(wq_rot.reshape(Q_LORA, -1), QK_HEAD).astype(BF16)
    w['wk'] = _head_pad(w_uk[l], QK_NOPE).astype(BF16)
    w['wvt'] = w_uv[l].T.astype(BF16)
    gq, gk = g_qk_q[l], g_qk_k[l]
    sw = lambda g: jnp.concatenate([g[QK_NOPE + half:], g[QK_NOPE:QK_NOPE + half]])
    zeros_n = jnp.zeros((QK_NOPE,), F32)
    w['gq'] = _lane_vec(gq[:QK_NOPE], gq[QK_NOPE:])
    w['gq_sw'] = _lane_vec(zeros_n, sw(gq))
    w['gk'] = _lane_vec(gk[:QK_NOPE], gk[QK_NOPE:])
    w['gk_sw'] = _lane_vec(zeros_n, sw(gk))
    w['gk_nope'] = _lane_vec(gk[:QK_NOPE], jnp.zeros((QK_ROPE,), F32))
    w['pq'] = _query_feature_perm()
    w['wukt'] = w_uk[l].T.astype(BF16)
    gr = gk[QK_NOPE:]
    w['gcol'] = jnp.concatenate([gr, gr])[:, None]
    wuv = w_uv[l].reshape(KV_LORA, MLA_HEADS, V_HEAD)
    eye = jnp.eye(MLA_HEADS, dtype=F32)
    w['w_uv_heads'] = (wuv[None, :, :, :] * eye[:, None, :, None]).reshape(MLA_HEADS * KV_LORA, B_WIDTH).astype(BF16)
    tril = jnp.tril(jnp.ones((CHUNK, CHUNK), F32))
    w['ws_prompt'] = (w_s[l] * tril).astype(BF16)
    w['bs_prompt'] = jnp.repeat(b_s[l].T, A_GROUP_DIM, axis=1)
    reps = CHUNK // t_q
    blk = w_s[l][:, :t_q, :t_q] * jnp.tril(jnp.ones((t_q, t_q), F32))
    w['ws_sample'] = jnp.einsum('ij,gts->gitjs', jnp.eye(reps, dtype=F32), blk).reshape(A_GROUPS, CHUNK, CHUNK).astype(BF16)
    w['bs_sample'] = jnp.repeat(jnp.tile(b_s[l][:, :t_q].T, (reps, 1)), A_GROUP_DIM, axis=1)
    w['w_o'] = w_o[l].astype(BF16)
    w['g_mem_x'] = vec(g_mem_x[l])
    w['g_mem_in'] = vec(g_mem_in[l])
    w['w_mq'] = w_mq[l].astype(BF16)
    w['w_mk'] = w_mk[l].astype(BF16)
    w['w_mv'] = w_mv[l].astype(BF16)
    w['g_mq'] = vec(g_mq[l])
    w['g_mk'] = vec(g_mk[l])
    w['w_mo'] = w_mo[l].astype(BF16)
    w['g_ffn'] = vec(g_ffn[l])
    w['w_up'] = w_up[l].astype(BF16)
    w['w_down'] = w_down[l].astype(BF16)
    w['w_conv'] = w_conv[l]
    w['b_conv'] = vec(b_conv[l])
    return w


def kernel(x_prompt, x_sample, cache_ckv, cache_kpe, cache_mem_k, cache_mem_v, state_ffn_conv, page_table, mem_prompt, g_mix, w_in, ln_v_g, ln_v_b, w_s, b_s, g_q_a, w_uq, g_kv_a, w_uk, w_uv, g_qk_q, g_qk_k, g_out_a, g_out_b, w_o, g_mem_x, g_mem_in, w_mq, w_mk, w_mv, g_mq, g_mk, w_mo, g_ffn, w_up, w_conv, b_conv, w_down):
    n_p, seq, _ = x_prompt.shape
    n_s, t_q, _ = x_sample.shape
    depth = g_mix.shape[0]
    mem_len = mem_prompt.shape[1]
    n_pages = page_table.shape[1]
    past = n_pages * PAGE_SIZE
    assert seq % TOK_TILE == 0 and TOK_TILE % ATT_TILE == 0 and TOK_TILE % CHUNK == 0
    assert (n_s * t_q) % TOK_TILE == 0 and TOK_TILE % t_q == 0 and CHUNK % t_q == 0 and CONV_W - 1 <= t_q <= LANES
    assert past % KV_TILE == 0 and KV_TILE % PAGE_SIZE == 0 and n_s % MEM_SEQ_TILE == 0
    assert (n_p * mem_len) % TOK_TILE == 0

    pos_p = jnp.arange(seq)
    pos_s = past + jnp.arange(t_q)
    cos_p, sin_p = _head_tile_tables(pos_p)
    cos_s, sin_s = _head_tile_tables(jnp.tile(pos_s, TOK_TILE // t_q))
    n_kv_tiles = past // KV_TILE
    t_past = _feature_table(jnp.arange(past), past).reshape(2 * QK_ROPE, n_kv_tiles, KV_TILE).transpose(1, 0, 2)
    t_new = _feature_table(pos_s, LANES)
    n_phys = cache_ckv.shape[1]
    ckv_pages = cache_ckv.reshape(depth * n_phys, PAGE_SIZE, KV_LORA)
    kpe_pages_t = jnp.swapaxes(cache_kpe, -1, -2).reshape(depth * n_phys, QK_ROPE, PAGE_SIZE)
    mem_rows = mem_len * MEM_HEADS
    mem_k_rows = cache_mem_k.reshape(depth * n_s, mem_rows, MEM_HEAD_DIM)
    mem_v_rows = cache_mem_v.reshape(depth * n_s, mem_rows, MEM_HEAD_DIM)

    xp = x_prompt.reshape(n_p * seq, D_MODEL)
    xs = x_sample.reshape(n_s * t_q, D_MODEL)
    outs = {k: [] for k in ('p_ckv', 'p_kpe', 'p_mk', 'p_mv', 'p_conv', 's_ckv', 's_kpe', 's_chunk_v', 's_conv')}
    for l in range(depth):
        w = _prep_weights(l, t_q, g_mix, w_in, ln_v_g, ln_v_b, w_s, b_s, g_q_a, w_uq, g_kv_a, w_uk, w_uv, g_qk_q,
                          g_qk_k, g_out_a, g_out_b, w_o, g_mem_x, g_mem_in, w_mq, w_mk, w_mv, g_mq, g_mk, w_mo,
                          g_ffn, w_up, w_conv, b_conv, w_down)
        w['t_past'], w['t_new'] = t_past, t_new

        wp = dict(w, ws=w['ws_prompt'], bs=w['bs_prompt'])
        q, k, vt, ckvn, kpe, a_n = _inproj(xp, cos_p, sin_p, wp, sample=False)
        b_n = _flash(q, k, vt, w['g_out_b_col'], n_p, seq)
        mk, mv = _memkv(mem_prompt.reshape(n_p * mem_len, D_MODEL), w)
        mk3 = mk.reshape(n_p, mem_len, MEM_WIDTH)
        mv3 = mv.reshape(n_p, mem_len, MEM_WIDTH)
        x2 = _merge_prompt(xp, a_n, b_n, mk3, mv3, w, seq)
        xp, tail = _ffn_prompt(x2, w, n_p, seq)
        outs['p_ckv'].append(ckvn.reshape(n_p, seq, KV_LORA))
        outs['p_kpe'].append(kpe.reshape(n_p, seq, QK_ROPE))
        outs['p_mk'].append(mk.reshape(n_p, mem_len, MEM_HEADS, MEM_HEAD_DIM))
        outs['p_mv'].append(mv.reshape(n_p, mem_len, MEM_HEADS, MEM_HEAD_DIM))
        outs['p_conv'].append(tail[:, 8 - (CONV_W - 1):, :])

        wsm = dict(w, ws=w['ws_sample'], bs=w['bs_sample'])
        qlat, qrf, ckvn_s, kpe_s, a_ns, vg = _inproj(xs, cos_s, sin_s, wsm, sample=True)
        nq = t_q * MLA_HEADS
        kpnew_t = jnp.pad(jnp.swapaxes(kpe_s.reshape(n_s, t_q, QK_ROPE), 1, 2), ((0, 0), (0, 0), (0, LANES - t_q)))
        ctx = _paged(page_table + l * n_phys, qlat.reshape(n_s, nq, KV_LORA), qrf.reshape(n_s, nq, LANES),
                     ckvn_s.reshape(n_s, t_q, KV_LORA), kpnew_t, ckv_pages, kpe_pages_t, w)
        x1, qm = _merge_sample(xs, a_ns, ctx.reshape(n_s * t_q, MLA_HEADS * KV_LORA), w)
        o = _memattn_sample(qm.reshape(n_s, t_q * MEM_HEADS, MEM_HEAD_DIM), mem_k_rows, mem_v_rows, l * n_s)
        st = state_ffn_conv[l]
        zero = jnp.zeros_like(st[:, :1])
        prev1 = jnp.concatenate([st[:, 1:2]] + [zero] * (t_q - 1), axis=1)
        prev2 = jnp.concatenate([st[:, 0:1], st[:, 1:2]] + [zero] * (t_q - 2), axis=1)
        flat = lambda a: a.reshape(n_s * t_q, D_FF)
        xs, gate = _ffn_sample(x1, o.reshape(n_s * t_q, MEM_WIDTH), flat(prev1), flat(prev2), w, t_q)
        gate = gate.reshape(n_s, t_q, D_FF)
        outs['s_ckv'].append(ckvn_s.reshape(n_s, t_q, KV_LORA))
        outs['s_kpe'].append(kpe_s.reshape(n_s, t_q, QK_ROPE))
        outs['s_chunk_v'].append(vg.reshape(n_s, t_q, A_GROUPS, A_GROUP_DIM))
        outs['s_conv'].append(gate[:, t_q - (CONV_W - 1):, :])

    return (xp.reshape(n_p, seq, D_MODEL), xs.reshape(n_s, t_q, D_MODEL),
            jnp.stack(outs['p_ckv']), jnp.stack(outs['p_kpe']), jnp.stack(outs['p_mk']), jnp.stack(outs['p_mv']),
            jnp.stack(outs['p_conv']), jnp.stack(outs['s_ckv']), jnp.stack(outs['s_kpe']),
            jnp.stack(outs['s_chunk_v']), jnp.stack(outs['s_conv']))
```

```python
import functools

import numpy as np
import jax
import jax.numpy as jnp
from jax import lax
from jax.experimental import pallas as pl
from jax.experimental.pallas import tpu as pltpu

D_MODEL = 1024
A_GROUPS = 8
A_GROUP_DIM = 64
A_WIDTH = A_GROUPS * A_GROUP_DIM
CHUNK = 128
MLA_HEADS = 8
Q_LORA = 256
KV_LORA = 256
QK_NOPE = 64
QK_ROPE = 32
QK_HEAD = QK_NOPE + QK_ROPE
V_HEAD = 64
B_WIDTH = MLA_HEADS * V_HEAD
ROPE_THETA = 10000.0
PAGE_SIZE = 128
MEM_HEADS = 4
MEM_HEAD_DIM = 128
MEM_WIDTH = MEM_HEADS * MEM_HEAD_DIM
D_FF = 2816
CONV_W = 3
EPS = 1e-6
NEG_INF = -1e30

LANES = 128
HEAD_PAD = LANES
QK_SCALE = QK_HEAD ** -0.5
LOG2E = 1.4426950408889634
V_AUG = V_HEAD + 16
MEM_SCALE = MEM_HEAD_DIM ** -0.5

TOK_TILE = 512
FF_TILE = 256
N_FF_TILES = D_FF // FF_TILE
ATT_TILE = 256
KV_TILE = 1024
MEM_SEQ_TILE = 8
VMEM_LIMIT = 56 * 1024 * 1024

F32 = jnp.float32
BF16 = jnp.bfloat16


def _dot(a, b):
    return jnp.dot(a, b, preferred_element_type=F32)


def _dot_nt(a, b):
    return lax.dot_general(a, b, (((1,), (1,)), ((), ())), preferred_element_type=F32)


def _rms(x):
    return x * lax.rsqrt(jnp.mean(x * x, axis=-1, keepdims=True) + EPS)


def _gelu(x):
    return jax.nn.gelu(x, approximate=True)


def _const_spec(shape):
    nd = len(shape)
    return pl.BlockSpec(shape, lambda *_: (0,) * nd, pipeline_mode=pl.Buffered(1))


def _params(*sem):
    return pltpu.CompilerParams(dimension_semantics=sem, vmem_limit_bytes=VMEM_LIMIT)


def _inproj_kernel(x_ref, gmix_ref, win_ref, lng_ref, lnb_ref, ws_ref, bs_ref, gqa_ref, wq_ref, wqr_ref,
                   gkva_ref, wk_ref, wv_ref, gq_ref, gqs_ref, gk_ref, gks_ref, cos_ref, sin_ref, goa_ref,
                   pq_ref, *rest, sample):
    if sample:
        qlat_ref, qrf_ref, ckvn_ref, kpe_ref, an_ref, vg_ref, zu_s, vgb_s, a_s = rest
    else:
        q_ref, k_ref, v_ref, ckvn_ref, kpe_ref, an_ref, zu_s, vgb_s, a_s = rest
    tm = x_ref.shape[0]
    xn = (_rms(x_ref[...]) * gmix_ref[...]).astype(BF16)

    o_cq = 2 * A_WIDTH
    o_ckv = o_cq + Q_LORA
    o_kpe = o_ckv + KV_LORA
    zv = _dot(xn, win_ref[:, A_WIDTH:2 * A_WIDTH])
    cq = _dot(xn, win_ref[:, o_cq:o_ckv])
    ckv = _dot(xn, win_ref[:, o_ckv:o_kpe])
    kpt = _dot(xn, win_ref[:, o_kpe:o_kpe + LANES])
    if not sample:
        kpr = _dot(xn, win_ref[:, o_kpe + LANES:o_kpe + 2 * LANES])
    zu_s[...] = _dot(xn, win_ref[:, 0:A_WIDTH])

    cqn = (_rms(cq) * gqa_ref[...]).astype(BF16)
    ckvn = _rms(ckv) * gkva_ref[...]
    ckvn_ref[...] = ckvn
    kpe_ref[...] = kpt[:, QK_NOPE:QK_HEAD]
    qall = _dot(cqn, wq_ref[...])
    qrall = _dot(cqn, wqr_ref[...])
    if not sample:
        cb = ckvn.astype(BF16)
        kall = _dot(cb, wk_ref[...])
        vt = _dot_nt(wv_ref[...], cb).astype(BF16)
        ones = jnp.ones((V_AUG - V_HEAD, ATT_TILE), BF16)
        for c in range(tm // ATT_TILE):
            cols = slice(c * ATT_TILE, (c + 1) * ATT_TILE)
            for h in range(MLA_HEADS):
                v_ref[c, h * V_AUG:h * V_AUG + V_HEAD, :] = vt[h * V_HEAD:(h + 1) * V_HEAD, cols]
                v_ref[c, h * V_AUG + V_HEAD:(h + 1) * V_AUG, :] = ones

    gv = _gelu(zv)
    low = lax.broadcasted_iota(jnp.int32, (1, LANES), 1) < A_GROUP_DIM
    inv_g = 1.0 / A_GROUP_DIM
    for p in range(A_WIDTH // LANES):
        sl = slice(p * LANES, (p + 1) * LANES)
        t = gv[:, sl]
        s_lo = jnp.sum(jnp.where(low, t, 0.0), axis=-1, keepdims=True)
        s_hi = jnp.sum(jnp.where(low, 0.0, t), axis=-1, keepdims=True)
        tc = t - jnp.where(low, s_lo, s_hi) * inv_g
        tc2 = tc * tc
        v_lo = jnp.sum(jnp.where(low, tc2, 0.0), axis=-1, keepdims=True)
        v_hi = jnp.sum(jnp.where(low, 0.0, tc2), axis=-1, keepdims=True)
        y = tc * lax.rsqrt(jnp.where(low, v_lo, v_hi) * inv_g + EPS) * lng_ref[:, sl] + lnb_ref[:, sl]
        if sample:
            vg_ref[:, sl] = y
        vgb_s[:, sl] = y.astype(BF16)
    for c in range(tm // CHUNK):
        rows = slice(c * CHUNK, (c + 1) * CHUNK)
        for p in range(A_WIDTH // LANES):
            sl = slice(p * LANES, (p + 1) * LANES)
            vp = vgb_s[rows, sl]
            mixed = jnp.where(low, _dot(ws_ref[2 * p], vp), _dot(ws_ref[2 * p + 1], vp)) + bs_ref[:, sl]
            a_s[rows, sl] = _gelu(zu_s[rows, sl]) * mixed
    an_ref[...] = (_rms(a_s[...]) * goa_ref[...]).astype(BF16)

    cosv = cos_ref[...]
    sinv = sin_ref[...]
    q_scale = QK_SCALE if sample else QK_SCALE * LOG2E
    qa = gq_ref[...] * cosv * q_scale
    qb = gqs_ref[...] * sinv * q_scale
    inv_d = 1.0 / QK_HEAD
    if not sample:
        ka = gk_ref[...] * cosv
        kb = gks_ref[...] * sinv
    for h in range(MLA_HEADS):
        sl = slice(h * HEAD_PAD, (h + 1) * HEAD_PAD)
        qh = qall[:, sl]
        r = lax.rsqrt(jnp.sum(qh * qh, axis=-1, keepdims=True) * inv_d + EPS)
        qf = r * (qh * qa + qrall[:, sl] * qb)
        if sample:
            qg = (qf * gk_ref[...]).astype(BF16)
            qlat_ref[:, h * KV_LORA:(h + 1) * KV_LORA] = _dot_nt(qg, wk_ref[:, sl]).astype(BF16)
            qrf_ref[:, sl] = _dot(qf.astype(BF16), pq_ref[...]).astype(BF16)
        else:
            q_ref[:, sl] = qf.astype(BF16)
            kh = kall[:, sl] + kpt
            rk = lax.rsqrt(jnp.sum(kh * kh, axis=-1, keepdims=True) * inv_d + EPS)
            k_ref[:, sl] = (rk * (kh * ka + kpr * kb)).astype(BF16)


def _inproj(x, cos, sin, w, *, sample):
    m = x.shape[0]
    tm = TOK_TILE
    n_tiles = m // tm
    n_pos_tiles = cos.shape[0] // tm
    row = lambda width: pl.BlockSpec((tm, width), lambda i: (i, 0))
    pos = pl.BlockSpec((tm, LANES), lambda i: (i % n_pos_tiles, 0))
    consts = [w['g_mix'], w['w_in'], w['ln_g'], w['ln_b'], w['ws'], w['bs'], w['g_q_a'], w['wq'], w['wq_rot'],
              w['g_kv_a'], w['wk'], w['wvt'], w['gq'], w['gq_sw'], w['gk_nope'] if sample else w['gk'], w['gk_sw']]
    in_specs = ([row(D_MODEL)] + [_const_spec(c.shape) for c in consts] + [pos, pos]
                + [_const_spec(w['g_out_a'].shape), _const_spec(w['pq'].shape)])
    args = [x] + consts + [cos, sin, w['g_out_a'], w['pq']]
    wide = MLA_HEADS * HEAD_PAD
    if sample:
        out_shape = [jax.ShapeDtypeStruct((m, MLA_HEADS * KV_LORA), BF16), jax.ShapeDtypeStruct((m, wide), BF16),
                     jax.ShapeDtypeStruct((m, KV_LORA), F32), jax.ShapeDtypeStruct((m, QK_ROPE), F32),
                     jax.ShapeDtypeStruct((m, A_WIDTH), BF16), jax.ShapeDtypeStruct((m, A_WIDTH), F32)]
        out_specs = [row(MLA_HEADS * KV_LORA), row(wide), row(KV_LORA), row(QK_ROPE), row(A_WIDTH), row(A_WIDTH)]
    else:
        att_tiles = tm // ATT_TILE
        out_shape = [jax.ShapeDtypeStruct((m, wide), BF16)] * 2 + [
            jax.ShapeDtypeStruct((m // ATT_TILE, MLA_HEADS * V_AUG, ATT_TILE), BF16),
            jax.ShapeDtypeStruct((m, KV_LORA), F32), jax.ShapeDtypeStruct((m, QK_ROPE), F32),
            jax.ShapeDtypeStruct((m, A_WIDTH), BF16)]
        out_specs = [row(wide)] * 2 + [pl.BlockSpec((att_tiles, MLA_HEADS * V_AUG, ATT_TILE), lambda i: (i, 0, 0)),
                                       row(KV_LORA), row(QK_ROPE), row(A_WIDTH)]
    return pl.pallas_call(
        functools.partial(_inproj_kernel, sample=sample),
        grid=(n_tiles,), in_specs=in_specs, out_specs=out_specs, out_shape=out_shape,
        scratch_shapes=[pltpu.VMEM((tm, A_WIDTH), F32), pltpu.VMEM((tm, A_WIDTH), BF16),
                        pltpu.VMEM((tm, A_WIDTH), F32)],
        compiler_params=_params("arbitrary"),
        name="inproj_sample" if sample else "inproj_prompt",
    )(*args)


def _flash_kernel(q_ref, k_ref, vt_ref, gob_ref, o_ref):
    i = pl.program_id(1)
    t = ATT_TILE
    causal = lax.broadcasted_iota(jnp.int32, (t, t), 0) <= lax.broadcasted_iota(jnp.int32, (t, t), 1)

    def step(j, carry, masked):
        start = pl.multiple_of(j * t, t)
        vt = vt_ref[j]
        heads = [slice(h * HEAD_PAD, (h + 1) * HEAD_PAD) for h in range(MLA_HEADS)]
        scores = [_dot_nt(k_ref[pl.ds(start, t), sl], q_ref[:, sl]) for sl in heads]
        stats = []
        for h in range(MLA_HEADS):
            m = carry[h][0]
            s = jnp.where(causal, scores[h], NEG_INF) if masked else scores[h]
            m_new = jnp.maximum(m, jnp.max(s, axis=0, keepdims=True))
            stats.append((m_new, jnp.exp2(m - m_new), jnp.exp2(s - m_new).astype(BF16)))
        new = []
        for h, (m_new, alpha, p) in enumerate(stats):
            pv = _dot(vt[h * V_AUG:(h + 1) * V_AUG, :], p)
            _, l, acc = carry[h]
            new.append((m_new, alpha * l + pv[V_HEAD:V_HEAD + 1], alpha * acc + pv[:V_HEAD]))
        return tuple(new)

    init = tuple((jnp.full((1, t), NEG_INF, F32), jnp.zeros((1, t), F32), jnp.zeros((V_HEAD, t), F32))
                 for _ in range(MLA_HEADS))
    carry = lax.fori_loop(0, i, functools.partial(step, masked=False), init)
    carry = step(i, carry, True)
    o_t = jnp.concatenate([acc / l for _, l, acc in carry], axis=0)
    bn_t = o_t * lax.rsqrt(jnp.mean(o_t * o_t, axis=0, keepdims=True) + EPS) * gob_ref[...]
    o_ref[...] = bn_t.T.astype(BF16)


def _flash(q, k, vt, g_out_b_col, n_batch, seq):
    m = q.shape[0]
    nq = seq // ATT_TILE
    wide = MLA_HEADS * HEAD_PAD
    return pl.pallas_call(
        _flash_kernel,
        grid=(n_batch, nq),
        in_specs=[pl.BlockSpec((ATT_TILE, wide), lambda b, i: (b * nq + i, 0)),
                  pl.BlockSpec((seq, wide), lambda b, i: (b, 0)),
                  pl.BlockSpec((nq, MLA_HEADS * V_AUG, ATT_TILE), lambda b, i: (b, 0, 0)),
                  pl.BlockSpec((B_WIDTH, 1), lambda b, i: (0, 0))],
        out_specs=pl.BlockSpec((ATT_TILE, B_WIDTH), lambda b, i: (b * nq + i, 0)),
        out_shape=jax.ShapeDtypeStruct((m, B_WIDTH), BF16),
        compiler_params=_params("arbitrary", "arbitrary"),
        name="flash_prompt",
    )(q, k, vt, g_out_b_col)


def _paged_kernel(pt_ref, qlat_ref, qrf_ref, qlat_next_ref, qrf_next_ref, cnew_ref, kpnew_ref, wukt_ref, gcol_ref,
                  tpast_ref, tnew_ref, ckv_hbm, kpe_hbm, ctx_ref, cbuf, pbuf, cn_s, tg_s, s0_s, sems, *, n_seq, n_pages):
    n = pl.program_id(0)
    slot = n % 2
    ppt = KV_TILE // PAGE_SIZE
    n_tiles = n_pages // ppt

    def start_fetch(seq, sl, first=0, last=n_pages):
        for p in range(first, last):
            page = pt_ref[seq, p]
            pltpu.make_async_copy(ckv_hbm.at[page], cbuf.at[sl, p], sems.at[0, sl]).start()
            pltpu.make_async_copy(kpe_hbm.at[page], pbuf.at[sl, p], sems.at[1, sl]).start()

    def wait_fetch(sl):
        pltpu.make_async_copy(ckv_hbm.at[pl.ds(0, n_pages)], cbuf.at[sl], sems.at[0, sl]).wait()
        pltpu.make_async_copy(kpe_hbm.at[pl.ds(0, n_pages)], pbuf.at[sl], sems.at[1, sl]).wait()

    nq = qlat_ref.shape[1]
    t_q = nq // MLA_HEADS
    n_nope = MLA_HEADS * QK_NOPE
    n_feat = 3 * QK_ROPE
    ones_rows = jnp.where(lax.broadcasted_iota(jnp.int32, (16, n_feat), 1) >= 2 * QK_ROPE, 1.0, 0.0).astype(BF16)

    def query_mats(ql_ref, qr_ref):
        a_mat = jnp.concatenate([wukt_ref[...], ql_ref[0]], axis=0)
        l_mat = jnp.concatenate([qr_ref[0][:, :n_feat], ones_rows], axis=0)
        return a_mat, l_mat

    def scores(mats, cb, kpt, tg):
        a_mat, l_mat = mats
        tk = cb.shape[0]
        m1 = _dot_nt(a_mat, cb)
        knt = m1[:n_nope]
        ssq = jnp.sum((knt * knt).reshape(MLA_HEADS, QK_NOPE, tk), axis=1)
        feat = jnp.concatenate([kpt * tg[:QK_ROPE], kpt * tg[QK_ROPE:], kpt * kpt], axis=0)
        m2 = _dot(l_mat, feat.astype(BF16))
        r = lax.rsqrt((ssq + m2[nq:nq + MLA_HEADS]) * (1.0 / QK_HEAD) + EPS)
        s = (m1[n_nope:] + m2[:nq]).reshape(t_q, MLA_HEADS, tk) * r[None]
        return s.reshape(nq, tk)

    def update(carry, s, cb):
        m, l, acc = carry
        m_new = jnp.maximum(m, jnp.max(s, axis=-1, keepdims=True))
        alpha = jnp.exp(m - m_new)
        p = jnp.exp(s - m_new)
        l = alpha * l + jnp.sum(p, axis=-1, keepdims=True)
        acc = alpha * acc + _dot(p.astype(BF16), cb)
        return m_new, l, acc

    def tile_scores(mats, sl, t):
        cb = cbuf[sl, t * ppt:(t + 1) * ppt].reshape(KV_TILE, KV_LORA).astype(BF16)
        kpt = jnp.concatenate([pbuf[sl, t * ppt + i] for i in range(ppt)], axis=1)
        return scores(mats, cb, kpt, tg_s[t]), cb

    @pl.when(n == 0)
    def _():
        cn_s[...] = jnp.zeros_like(cn_s)
        for t in range(n_tiles):
            tg_s[t] = tpast_ref[t] * gcol_ref[...]
        start_fetch(0, 0)
        wait_fetch(0)
        s0_s[...] = tile_scores(query_mats(qlat_ref, qrf_ref), 0, 0)[0]

    cur = query_mats(qlat_ref, qrf_ref)
    nslot = 1 - slot
    nseq = jnp.minimum(n + 1, n_seq - 1)
    per_tile = -(-n_pages // (n_tiles - 2))
    cn_s[0:t_q, :] = cnew_ref[0]
    cb_new = cn_s[...].astype(BF16)
    s_new = scores(cur, cb_new, kpnew_ref[0], tnew_ref[...] * gcol_ref[...])
    key = lax.broadcasted_iota(jnp.int32, (nq, LANES), 1)
    qry = lax.broadcasted_iota(jnp.int32, (nq, LANES), 0) // MLA_HEADS
    s_new = jnp.where(key <= qry, s_new, NEG_INF)
    s_cur = s0_s[...]
    cb_cur = cbuf[slot, 0:ppt].reshape(KV_TILE, KV_LORA).astype(BF16)
    stats = (jnp.full((nq, 1), NEG_INF, F32), jnp.zeros((nq, 1), F32), jnp.zeros((nq, KV_LORA), F32))
    for t in range(n_tiles):
        start_fetch(nseq, nslot, min(t * per_tile, n_pages), min((t + 1) * per_tile, n_pages))
        if t + 1 < n_tiles:
            s_next, cb_next = tile_scores(cur, slot, t + 1)
        else:
            wait_fetch(nslot)
            s_next, cb_next = tile_scores(query_mats(qlat_next_ref, qrf_next_ref), nslot, 0)
        stats = update(stats, s_cur, cb_cur)
        s_cur, cb_cur = s_next, cb_next
    s0_s[...] = s_cur
    m, l, acc = update(stats, s_new, cb_new)
    ctx_ref[0] = acc / l


def _paged(page_table, qlat, qrf, cnew, kpnew_t, cache_ckv, cache_kpe_t, w):
    n_seq, n_pages = page_table.shape
    nq = qlat.shape[1]
    t_q = cnew.shape[1]
    seq_spec = lambda shape: pl.BlockSpec((1,) + shape, lambda n, pt: (n, 0, 0))
    next_spec = lambda shape: pl.BlockSpec((1,) + shape, lambda n, pt: (jnp.minimum(n + 1, n_seq - 1), 0, 0))
    const = lambda shape: pl.BlockSpec(shape, lambda n, pt: (0,) * len(shape), pipeline_mode=pl.Buffered(1))
    grid_spec = pltpu.PrefetchScalarGridSpec(
        num_scalar_prefetch=1, grid=(n_seq,),
        in_specs=[seq_spec((nq, KV_LORA)), seq_spec((nq, LANES)), next_spec((nq, KV_LORA)), next_spec((nq, LANES)),
                  seq_spec((t_q, KV_LORA)), seq_spec((QK_ROPE, LANES)),
                  const(w['wukt'].shape), const(w['gcol'].shape), const(w['t_past'].shape), const(w['t_new'].shape),
                  pl.BlockSpec(memory_space=pl.ANY), pl.BlockSpec(memory_space=pl.ANY)],
        out_specs=seq_spec((nq, KV_LORA)),
        scratch_shapes=[pltpu.VMEM((2, n_pages, PAGE_SIZE, KV_LORA), F32),
                        pltpu.VMEM((2, n_pages, QK_ROPE, PAGE_SIZE), F32),
                        pltpu.VMEM((LANES, KV_LORA), F32), pltpu.VMEM(w['t_past'].shape, F32),
                        pltpu.VMEM((nq, KV_TILE), F32), pltpu.SemaphoreType.DMA((2, 2))])
    return pl.pallas_call(
        functools.partial(_paged_kernel, n_seq=n_seq, n_pages=n_pages),
        grid_spec=grid_spec,
        out_shape=jax.ShapeDtypeStruct((n_seq, nq, KV_LORA), F32),
        compiler_params=_params("arbitrary"),
        name="paged_sample",
    )(page_table, qlat, qrf, qlat, qrf, cnew, kpnew_t, w['wukt'], w['gcol'], w['t_past'], w['t_new'],
      cache_ckv, cache_kpe_t)


def _memkv_kernel(mem_ref, gin_ref, wmk_ref, wmv_ref, gmk_ref, k_ref, v_ref):
    hm = (_rms(mem_ref[...]) * gin_ref[...]).astype(BF16)
    kk = _dot(hm, wmk_ref[...])
    for h in range(MEM_HEADS):
        sl = slice(h * MEM_HEAD_DIM, (h + 1) * MEM_HEAD_DIM)
        k_ref[:, sl] = _rms(kk[:, sl]) * gmk_ref[...]
    v_ref[...] = _dot(hm, wmv_ref[...])


def _memkv(mem, w):
    m = mem.shape[0]
    tm = TOK_TILE
    row = lambda width: pl.BlockSpec((tm, width), lambda i: (i, 0))
    consts = [w['g_mem_in'], w['w_mk'], w['w_mv'], w['g_mk']]
    return pl.pallas_call(
        _memkv_kernel, grid=(m // tm,),
        in_specs=[row(D_MODEL)] + [_const_spec(c.shape) for c in consts],
        out_specs=[row(MEM_WIDTH), row(MEM_WIDTH)],
        out_shape=[jax.ShapeDtypeStruct((m, MEM_WIDTH), F32)] * 2,
        compiler_params=_params("arbitrary"),
        name="memkv_prompt",
    )(mem, *consts)


def _mem_query(x1, gmx_ref, wmq_ref, gmq_ref):
    qm = _dot((_rms(x1) * gmx_ref[...]).astype(BF16), wmq_ref[...])
    heads = []
    for h in range(MEM_HEADS):
        sl = slice(h * MEM_HEAD_DIM, (h + 1) * MEM_HEAD_DIM)
        heads.append((_rms(qm[:, sl]) * (gmq_ref[...] * MEM_SCALE)).astype(BF16))
    return heads


def _merge_prompt_kernel(x_ref, an_ref, bn_ref, wo_ref, gmx_ref, wmq_ref, gmq_ref, mk_ref, mv_ref, wmo_ref,
                         y_ref, o_s):
    x1 = x_ref[...] + _dot(an_ref[...], wo_ref[0:A_WIDTH, :]) + _dot(bn_ref[...], wo_ref[A_WIDTH:, :])
    heads = _mem_query(x1, gmx_ref, wmq_ref, gmq_ref)
    lanes = [slice(h * MEM_HEAD_DIM, (h + 1) * MEM_HEAD_DIM) for h in range(MEM_HEADS)]
    scores = [_dot_nt(heads[h], mk_ref[0, :, sl].astype(BF16)) for h, sl in enumerate(lanes)]
    probs = [jnp.exp(s - jnp.max(s, axis=-1, keepdims=True)) for s in scores]
    for p, sl in zip(probs, lanes):
        o = _dot(p.astype(BF16), mv_ref[0, :, sl].astype(BF16)) / jnp.sum(p, axis=-1, keepdims=True)
        o_s[:, sl] = o.astype(BF16)
    y_ref[...] = x1 + _dot(o_s[...], wmo_ref[...])


def _merge_prompt(x, a_n, b_n, mk, mv, w, seq):
    m = x.shape[0]
    tm = TOK_TILE
    per_seq = seq // tm
    mem_len = mk.shape[1]
    row = lambda width: pl.BlockSpec((tm, width), lambda i: (i, 0))
    mem = pl.BlockSpec((1, mem_len, MEM_WIDTH), lambda i: (i // per_seq, 0, 0))
    c = lambda name: _const_spec(w[name].shape)
    return pl.pallas_call(
        _merge_prompt_kernel, grid=(m // tm,),
        in_specs=[row(D_MODEL), row(A_WIDTH), row(B_WIDTH), c('w_o'), c('g_mem_x'), c('w_mq'),
                  c('g_mq'), mem, mem, c('w_mo')],
        out_specs=row(D_MODEL),
        out_shape=jax.ShapeDtypeStruct((m, D_MODEL), F32),
        scratch_shapes=[pltpu.VMEM((tm, MEM_WIDTH), BF16)],
        compiler_params=_params("arbitrary"),
        name="merge_prompt",
    )(x, a_n, b_n, w['w_o'], w['g_mem_x'], w['w_mq'], w['g_mq'], mk, mv, w['w_mo'])


def _merge_sample_kernel(x_ref, an_ref, ctx_ref, wuvb_ref, gob_ref, wo_ref, gmx_ref, wmq_ref, gmq_ref,
                         x1_ref, qm_ref):
    b = _dot(ctx_ref[...].astype(BF16), wuvb_ref[...])
    bn = (_rms(b) * gob_ref[...]).astype(BF16)
    x1 = x_ref[...] + _dot(an_ref[...], wo_ref[0:A_WIDTH, :]) + _dot(bn, wo_ref[A_WIDTH:, :])
    x1_ref[...] = x1
    heads = _mem_query(x1, gmx_ref, wmq_ref, gmq_ref)
    for h in range(MEM_HEADS):
        qm_ref[:, h * MEM_HEAD_DIM:(h + 1) * MEM_HEAD_DIM] = heads[h]


def _merge_sample(x, a_n, ctx, w):
    m = x.shape[0]
    tm = TOK_TILE
    row = lambda width: pl.BlockSpec((tm, width), lambda i: (i, 0))
    c = lambda name: _const_spec(w[name].shape)
    return pl.pallas_call(
        _merge_sample_kernel, grid=(m // tm,),
        in_specs=[row(D_MODEL), row(A_WIDTH), row(ctx.shape[1]), c('w_uv_heads'), c('g_out_b'), c('w_o'),
                  c('g_mem_x'), c('w_mq'), c('g_mq')],
        out_specs=[row(D_MODEL), row(MEM_WIDTH)],
        out_shape=[jax.ShapeDtypeStruct((m, D_MODEL), F32), jax.ShapeDtypeStruct((m, MEM_WIDTH), BF16)],
        compiler_params=_params("arbitrary"),
        name="merge_sample",
    )(x, a_n, ctx, w['w_uv_heads'], w['g_out_b'], w['w_o'], w['g_mem_x'], w['w_mq'], w['g_mq'])


def _memattn_sample_kernel(q_ref, mk_ref, mv_ref, o_ref):
    nrow, nkey = q_ref.shape[1], mk_ref.shape[1]
    s = jnp.einsum('nqd,nkd->nqk', q_ref[...], mk_ref[...].astype(BF16), preferred_element_type=F32)
    same_head = (lax.broadcasted_iota(jnp.int32, (1, nrow, nkey), 1) % MEM_HEADS
                 == lax.broadcasted_iota(jnp.int32, (1, nrow, nkey), 2) % MEM_HEADS)
    s = jnp.where(same_head, s, NEG_INF)
    p = jnp.exp(s - jnp.max(s, axis=-1, keepdims=True))
    o = jnp.einsum('nqk,nkd->nqd', p.astype(BF16), mv_ref[...].astype(BF16), preferred_element_type=F32)
    o_ref[...] = (o / jnp.sum(p, axis=-1, keepdims=True)).astype(BF16)


def _memattn_sample(qm, mk, mv, first_seq):
    n_seq, nrow, _ = qm.shape
    nkey = mk.shape[1]
    ns = MEM_SEQ_TILE
    first_blk = first_seq // ns
    return pl.pallas_call(
        _memattn_sample_kernel, grid=(n_seq // ns,),
        in_specs=[pl.BlockSpec((ns, nrow, MEM_HEAD_DIM), lambda i: (i, 0, 0)),
                  pl.BlockSpec((ns, nkey, MEM_HEAD_DIM), lambda i: (first_blk + i, 0, 0)),
                  pl.BlockSpec((ns, nkey, MEM_HEAD_DIM), lambda i: (first_blk + i, 0, 0))],
        out_specs=pl.BlockSpec((ns, nrow, MEM_HEAD_DIM), lambda i: (i, 0, 0)),
        out_shape=jax.ShapeDtypeStruct((n_seq, nrow, MEM_HEAD_DIM), BF16),
        compiler_params=_params("arbitrary"),
        name="memattn_sample",
    )(qm, mk, mv)


def _ffn_kernel(*refs, sample, tiles_per_seq):
    if sample:
        (x_ref, o_ref, wmo_ref, gffn_ref, wup_ref, wd_ref, wc_ref, bc_ref, p1_ref, p2_ref,
         y_ref, gate_ref, h_s, g_s, act_s) = refs
        x2 = x_ref[...] + _dot(o_ref[...], wmo_ref[...])
        g_s[0:8, :] = jnp.zeros((8, D_FF), F32)
    else:
        (x_ref, gffn_ref, wup_ref, wd_ref, wc_ref, bc_ref, y_ref, tail_ref, h_s, g_s, act_s, carry_s) = refs
        x2 = x_ref[...]

        @pl.when(pl.program_id(0) == 0)
        def _():
            carry_s[...] = jnp.zeros_like(carry_s)
        g_s[0:8, :] = jnp.where((pl.program_id(0) % tiles_per_seq) == 0, 0.0, carry_s[...])
    tm = x_ref.shape[0]
    y_ref[...] = x2
    h_s[...] = (_rms(x2) * gffn_ref[...]).astype(BF16)
    if sample:
        t_in_seq = lax.broadcasted_iota(jnp.int32, (tm, 1), 0) % tiles_per_seq
    for f in range(N_FF_TILES):
        sl = slice(f * FF_TILE, (f + 1) * FF_TILE)
        h = h_s[...]
        g = _dot(h, wup_ref[:, sl])
        val = _dot(h, wup_ref[:, D_FF + f * FF_TILE:D_FF + (f + 1) * FF_TILE])
        g_s[8:, sl] = g
        g1 = g_s[7:7 + tm, sl]
        g2 = g_s[6:6 + tm, sl]
        if sample:
            gate_ref[:, sl] = g
            g1 = jnp.where(t_in_seq >= 1, g1, p1_ref[:, sl])
            g2 = jnp.where(t_in_seq >= 2, g2, p2_ref[:, sl])
        conv = bc_ref[:, sl] + g2 * wc_ref[0:1, sl] + g1 * wc_ref[1:2, sl] + g * wc_ref[2:3, sl]
        act_s[:, sl] = (conv / (1.0 + jnp.exp(-conv)) * val).astype(BF16)
    if not sample:
        tail = g_s[tm:tm + 8, :]
        carry_s[...] = tail
        tail_ref[0] = tail
    y_ref[...] += _dot(act_s[...], wd_ref[...])


def _ffn_scratch(tm):
    return [pltpu.VMEM((tm, D_MODEL), BF16), pltpu.VMEM((tm + 8, D_FF), F32), pltpu.VMEM((tm, D_FF), BF16)]


def _ffn_prompt(x2, w, n_batch, seq):
    m = x2.shape[0]
    tm = TOK_TILE
    per_seq = seq // tm
    row = pl.BlockSpec((tm, D_MODEL), lambda i: (i, 0))
    c = lambda name: _const_spec(w[name].shape)
    return pl.pallas_call(
        functools.partial(_ffn_kernel, sample=False, tiles_per_seq=per_seq), grid=(m // tm,),
        in_specs=[row, c('g_ffn'), c('w_up'), c('w_down'), c('w_conv'), c('b_conv')],
        out_specs=[row, pl.BlockSpec((1, 8, D_FF), lambda i: (i // per_seq, 0, 0))],
        out_shape=[jax.ShapeDtypeStruct((m, D_MODEL), F32), jax.ShapeDtypeStruct((n_batch, 8, D_FF), F32)],
        scratch_shapes=_ffn_scratch(tm) + [pltpu.VMEM((8, D_FF), F32)],
        compiler_params=_params("arbitrary"),
        name="ffn_prompt",
    )(x2, w['g_ffn'], w['w_up'], w['w_down'], w['w_conv'], w['b_conv'])


def _ffn_sample(x1, o, prev1, prev2, w, t_q):
    m = x1.shape[0]
    tm = TOK_TILE
    row = lambda width: pl.BlockSpec((tm, width), lambda i: (i, 0))
    c = lambda name: _const_spec(w[name].shape)
    return pl.pallas_call(
        functools.partial(_ffn_kernel, sample=True, tiles_per_seq=t_q), grid=(m // tm,),
        in_specs=[row(D_MODEL), row(MEM_WIDTH), c('w_mo'), c('g_ffn'), c('w_up'), c('w_down'),
                  c('w_conv'), c('b_conv'), row(D_FF), row(D_FF)],
        out_specs=[row(D_MODEL), row(D_FF)],
        out_shape=[jax.ShapeDtypeStruct((m, D_MODEL), F32), jax.ShapeDtypeStruct((m, D_FF), F32)],
        scratch_shapes=_ffn_scratch(tm),
        compiler_params=_params("arbitrary"),
        name="ffn_sample",
    )(x1, o, w['w_mo'], w['g_ffn'], w['w_up'], w['w_down'], w['w_conv'], w['b_conv'], prev1, prev2)


def _head_pad(wm, width):
    k = wm.shape[0]
    wm = wm.reshape(k, MLA_HEADS, width)
    return jnp.pad(wm, ((0, 0), (0, 0), (0, HEAD_PAD - width))).reshape(k, MLA_HEADS * HEAD_PAD)


def _rot_half_cols(wm):
    half = QK_ROPE // 2
    return jnp.concatenate([-wm[..., half:], wm[..., :half]], axis=-1)


def _lane_vec(nope, rope):
    return jnp.concatenate([nope, rope, jnp.zeros((LANES - QK_HEAD,), F32)])[None, :]


def _rope_tables(pos):
    half = QK_ROPE // 2
    inv_freq = ROPE_THETA ** (-jnp.arange(half, dtype=F32) / half)
    ang = pos.astype(F32)[:, None] * inv_freq[None, :]
    return jnp.cos(ang), jnp.sin(ang)


def _head_tile_tables(pos):
    cos, sin = _rope_tables(pos)
    n = pos.shape[0]
    cos_t = jnp.concatenate([jnp.ones((n, QK_NOPE), F32), cos, cos, jnp.zeros((n, LANES - QK_HEAD), F32)], axis=1)
    sin_t = jnp.concatenate([jnp.zeros((n, QK_NOPE), F32), sin, sin, jnp.zeros((n, LANES - QK_HEAD), F32)], axis=1)
    return cos_t, sin_t


def _feature_table(pos, cols):
    cos, sin = _rope_tables(pos)
    t = jnp.concatenate([cos, cos, sin, sin], axis=1).T
    return jnp.pad(t, ((0, 0), (0, cols - pos.shape[0])))


def _query_feature_perm():
    half = QK_ROPE // 2
    p = np.zeros((LANES, LANES), np.float32)
    for j in range(QK_ROPE):
        p[QK_NOPE + j, j] = 1.0
    for j in range(half):
        p[QK_NOPE + half + j, QK_ROPE + j] = 1.0
        p[QK_NOPE + j, QK_ROPE + half + j] = -1.0
    return jnp.asarray(p, BF16)


def _prep_weights(l, t_q, g_mix, w_in, ln_v_g, ln_v_b, w_s, b_s, g_q_a, w_uq, g_kv_a, w_uk, w_uv, g_qk_q, g_qk_k,
                  g_out_a, g_out_b, w_o, g_mem_x, g_mem_in, w_mq, w_mk, w_mv, g_mq, g_mk, w_mo, g_ffn, w_up,
                  w_conv, b_conv, w_down):
    half = QK_ROPE // 2
    o_kpe = 2 * A_WIDTH + Q_LORA + KV_LORA
    vec = lambda a: a.reshape(1, -1).astype(F32)
    w = {}
    wkpe = w_in[l][:, o_kpe:]
    lane_tile = lambda blk: jnp.pad(blk, ((0, 0), (QK_NOPE, LANES - QK_HEAD)))
    w['w_in'] = jnp.concatenate([w_in[l][:, :o_kpe], lane_tile(wkpe), lane_tile(_rot_half_cols(wkpe))],
                                axis=1).astype(BF16)
    w['g_mix'] = vec(g_mix[l])
    w['ln_g'] = vec(ln_v_g[l])
    w['ln_b'] = vec(ln_v_b[l])
    w['g_q_a'] = vec(g_q_a[l])
    w['g_kv_a'] = vec(g_kv_a[l])
    w['g_out_a'] = vec(g_out_a[l])
    w['g_out_b'] = vec(g_out_b[l])
    w['g_out_b_col'] = g_out_b[l][:, None]
    wq = w_uq[l].reshape(Q_LORA, MLA_HEADS, QK_HEAD)
    wq_rot = jnp.concatenate([jnp.zeros((Q_LORA, MLA_HEADS, QK_NOPE), F32), _rot_half_cols(wq[..., QK_NOPE:])], axis=-1)
    w['wq'] = _head_pad(wq.reshape(Q_LORA, -1), QK_HEAD).astype(BF16)
    w['wq_rot'] = _head_pad(wq_rot.reshape(Q_LORA, -1), QK_HEAD).astype(BF16)
    w['wk'] = _head_pad(w_uk[l], QK_NOPE).astype(BF16)
    w['wvt'] = w_uv[l].T.astype(BF16)
    gq, gk = g_qk_q[l], g_qk_k[l]
    sw = lambda g: jnp.concatenate([g[QK_NOPE + half:], g[QK_NOPE:QK_NOPE + half]])
    zeros_n = jnp.zeros((QK_NOPE,), F32)
    w['gq'] = _lane_vec(gq[:QK_NOPE], gq[QK_NOPE:])
    w['gq_sw'] = _lane_vec(zeros_n, sw(gq))
    w['gk'] = _lane_vec(gk[:QK_NOPE], gk[QK_NOPE:])
    w['gk_sw'] = _lane_vec(zeros_n, sw(gk))
    w['gk_nope'] = _lane_vec(gk[:QK_NOPE], jnp.zeros((QK_ROPE,), F32))
    w['pq'] = _query_feature_perm()
    w['wukt'] = w_uk[l].T.astype(BF16)
    gr = gk[QK_NOPE:]
    w['gcol'] = jnp.concatenate([gr, gr])[:, None]
    wuv = w_uv[l].reshape(KV_LORA, MLA_HEADS, V_HEAD)
    eye = jnp.eye(MLA_HEADS, dtype=F32)
    w['w_uv_heads'] = (wuv[None, :, :, :] * eye[:, None, :, None]).reshape(MLA_HEADS * KV_LORA, B_WIDTH).astype(BF16)
    tril = jnp.tril(jnp.ones((CHUNK, CHUNK), F32))
    w['ws_prompt'] = (w_s[l] * tril).astype(BF16)
    w['bs_prompt'] = jnp.repeat(b_s[l].T, A_GROUP_DIM, axis=1)
    reps = CHUNK // t_q
    blk = w_s[l][:, :t_q, :t_q] * jnp.tril(jnp.ones((t_q, t_q), F32))
    w['ws_sample'] = jnp.einsum('ij,gts->gitjs', jnp.eye(reps, dtype=F32), blk).reshape(A_GROUPS, CHUNK, CHUNK).astype(BF16)
    w['bs_sample'] = jnp.repeat(jnp.tile(b_s[l][:, :t_q].T, (reps, 1)), A_GROUP_DIM, axis=1)
    w['w_o'] = w_o[l].astype(BF16)
    w['g_mem_x'] = vec(g_mem_x[l])
    w['g_mem_in'] = vec(g_mem_in[l])
    w['w_mq'] = w_mq[l].astype(BF16)
    w['w_mk'] = w_mk[l].astype(BF16)
    w['w_mv'] = w_mv[l].astype(BF16)
    w['g_mq'] = vec(g_mq[l])
    w['g_mk'] = vec(g_mk[l])
    w['w_mo'] = w_mo[l].astype(BF16)
    w['g_ffn'] = vec(g_ffn[l])
    w['w_up'] = w_up[l].astype(BF16)
    w['w_down'] = w_down[l].astype(BF16)
    w['w_conv'] = w_conv[l]
    w['b_conv'] = vec(b_conv[l])
    return w


def kernel(x_prompt, x_sample, cache_ckv, cache_kpe, cache_mem_k, cache_mem_v, state_ffn_conv, page_table, mem_prompt, g_mix, w_in, ln_v_g, ln_v_b, w_s, b_s, g_q_a, w_uq, g_kv_a, w_uk, w_uv, g_qk_q, g_qk_k, g_out_a, g_out_b, w_o, g_mem_x, g_mem_in, w_mq, w_mk, w_mv, g_mq, g_mk, w_mo, g_ffn, w_up, w_conv, b_conv, w_down):
    n_p, seq, _ = x_prompt.shape
    n_s, t_q, _ = x_sample.shape
    depth = g_mix.shape[0]
    mem_len = mem_prompt.shape[1]
    n_pages = page_table.shape[1]
    past = n_pages * PAGE_SIZE
    assert seq % TOK_TILE == 0 and TOK_TILE % ATT_TILE == 0 and TOK_TILE % CHUNK == 0
    assert (n_s * t_q) % TOK_TILE == 0 and TOK_TILE % t_q == 0 and CHUNK % t_q == 0 and CONV_W - 1 <= t_q <= LANES
    assert past % KV_TILE == 0 and KV_TILE % PAGE_SIZE == 0 and n_s % MEM_SEQ_TILE == 0
    assert (n_p * mem_len) % TOK_TILE == 0

    pos_p = jnp.arange(seq)
    pos_s = past + jnp.arange(t_q)
    cos_p, sin_p = _head_tile_tables(pos_p)
    cos_s, sin_s = _head_tile_tables(jnp.tile(pos_s, TOK_TILE // t_q))
    n_kv_tiles = past // KV_TILE
    t_past = _feature_table(jnp.arange(past), past).reshape(2 * QK_ROPE, n_kv_tiles, KV_TILE).transpose(1, 0, 2)
    t_new = _feature_table(pos_s, LANES)
    n_phys = cache_ckv.shape[1]
    ckv_pages = cache_ckv.reshape(depth * n_phys, PAGE_SIZE, KV_LORA)
    kpe_pages_t = jnp.swapaxes(cache_kpe, -1, -2).reshape(depth * n_phys, QK_ROPE, PAGE_SIZE)
    mem_rows = mem_len * MEM_HEADS
    mem_k_rows = cache_mem_k.reshape(depth * n_s, mem_rows, MEM_HEAD_DIM)
    mem_v_rows = cache_mem_v.reshape(depth * n_s, mem_rows, MEM_HEAD_DIM)

    xp = x_prompt.reshape(n_p * seq, D_MODEL)
    xs = x_sample.reshape(n_s * t_q, D_MODEL)
    outs = {k: [] for k in ('p_ckv', 'p_kpe', 'p_mk', 'p_mv', 'p_conv', 's_ckv', 's_kpe', 's_chunk_v', 's_conv')}
    for l in range(depth):
        w = _prep_weights(l, t_q, g_mix, w_in, ln_v_g, ln_v_b, w_s, b_s, g_q_a, w_uq, g_kv_a, w_uk, w_uv, g_qk_q,
                          g_qk_k, g_out_a, g_out_b, w_o, g_mem_x, g_mem_in, w_mq, w_mk, w_mv, g_mq, g_mk, w_mo,
                          g_ffn, w_up, w_conv, b_conv, w_down)
        w['t_past'], w['t_new'] = t_past, t_new

        wp = dict(w, ws=w['ws_prompt'], bs=w['bs_prompt'])
        q, k, vt, ckvn, kpe, a_n = _inproj(xp, cos_p, sin_p, wp, sample=False)
        b_n = _flash(q, k, vt, w['g_out_b_col'], n_p, seq)
        mk, mv = _memkv(mem_prompt.reshape(n_p * mem_len, D_MODEL), w)
        mk3 = mk.reshape(n_p, mem_len, MEM_WIDTH)
        mv3 = mv.reshape(n_p, mem_len, MEM_WIDTH)
        x2 = _merge_prompt(xp, a_n, b_n, mk3, mv3, w, seq)
        xp, tail = _ffn_prompt(x2, w, n_p, seq)
        outs['p_ckv'].append(ckvn.reshape(n_p, seq, KV_LORA))
        outs['p_kpe'].append(kpe.reshape(n_p, seq, QK_ROPE))
        outs['p_mk'].append(mk.reshape(n_p, mem_len, MEM_HEADS, MEM_HEAD_DIM))
        outs['p_mv'].append(mv.reshape(n_p, mem_len, MEM_HEADS, MEM_HEAD_DIM))
        outs['p_conv'].append(tail[:, 8 - (CONV_W - 1):, :])

        wsm = dict(w, ws=w['ws_sample'], bs=w['bs_sample'])
        qlat, qrf, ckvn_s, kpe_s, a_ns, vg = _inproj(xs, cos_s, sin_s, wsm, sample=True)
        nq = t_q * MLA_HEADS
        kpnew_t = jnp.pad(jnp.swapaxes(kpe_s.reshape(n_s, t_q, QK_ROPE), 1, 2), ((0, 0), (0, 0), (0, LANES - t_q)))
        ctx = _paged(page_table + l * n_phys, qlat.reshape(n_s, nq, KV_LORA), qrf.reshape(n_s, nq, LANES),
                     ckvn_s.reshape(n_s, t_q, KV_LORA), kpnew_t, ckv_pages, kpe_pages_t, w)
        x1, qm = _merge_sample(xs, a_ns, ctx.reshape(n_s * t_q, MLA_HEADS * KV_LORA), w)
        o = _memattn_sample(qm.reshape(n_s, t_q * MEM_HEADS, MEM_HEAD_DIM), mem_k_rows, mem_v_rows, l * n_s)
        st = state_ffn_conv[l]
        zero = jnp.zeros_like(st[:, :1])
        prev1 = jnp.concatenate([st[:, 1:2]] + [zero] * (t_q - 1), axis=1)
        prev2 = jnp.concatenate([st[:, 0:1], st[:, 1:2]] + [zero] * (t_q - 2), axis=1)
        flat = lambda a: a.reshape(n_s * t_q, D_FF)
        xs, gate = _ffn_sample(x1, o.reshape(n_s * t_q, MEM_WIDTH), flat(prev1), flat(prev2), w, t_q)
        gate = gate.reshape(n_s, t_q, D_FF)
        outs['s_ckv'].append(ckvn_s.reshape(n_s, t_q, KV_LORA))
        outs['s_kpe'].append(kpe_s.reshape(n_s, t_q, QK_ROPE))
        outs['s_chunk_v'].append(vg.reshape(n_s, t_q, A_GROUPS, A_GROUP_DIM))
        outs['s_conv'].append(gate[:, t_q - (CONV_W - 1):, :])

    return (xp.reshape(n_p, seq, D_MODEL), xs.reshape(n_s, t_q, D_MODEL),
            jnp.stack(outs['p_ckv']), jnp.stack(outs['p_kpe']), jnp.stack(outs['p_mk']), jnp.stack(outs['p_mv']),
            jnp.stack(outs['p_conv']), jnp.stack(outs['s_ckv']), jnp.stack(outs['s_kpe']),
            jnp.stack(outs['s_chunk_v']), jnp.stack(outs['s_conv']))
```
